```python
import math
import jax, jax.numpy as jnp
from jax import lax
import numpy as np

D_MODEL = 4096
BATCH = 1
SEQ = 16384
DEPTH = 4

GRID_W = 64
CTX_LEN = 256
N_MIXERS = 3
MIXER_POOL = 0
MIXER_CHUNK_MLP = 1
MIXER_DIFF_ATTN = 2
POOL_WINDOWS = (2, 4, 8, 16)
POOL_GROUP = D_MODEL // len(POOL_WINDOWS)
CHUNK = 128
GMLP_WIDTH = D_MODEL
GMLP_GROUPS = 32
GMLP_GROUP_DIM = GMLP_WIDTH // GMLP_GROUPS
DA_HEADS = 16
DA_HEAD_DIM = D_MODEL // (2 * DA_HEADS)
ROPE_AXIS_DIM = DA_HEAD_DIM // 2
ROPE_THETA = 10000.0
Q_BLOCK = 128
N_EXPERTS = 16
EC_CAPACITY_FACTOR = 2
EXPERT_HIDDEN = 384
NORM_EPS = 1e-6

kernel_name = 'hybrid_pool_gmlp_diffattn_ec_moe_dit'


def _n_layers_of(kind):
    return len(range(kind, DEPTH, N_MIXERS))


def _ctx_read_after(i):
    return any(j % N_MIXERS == MIXER_DIFF_ATTN for j in range(i + 1, DEPTH))


def _rmsnorm(x, g):
    xf = x.astype(jnp.float32)
    y = xf * lax.rsqrt(jnp.mean(xf * xf, axis=-1, keepdims=True) + NORM_EPS)
    return (y * g.astype(jnp.float32)).astype(x.dtype)


def _modulate(h, shift, scale):
    return h * (1 + scale[:, None, :]) + shift[:, None, :]


def _pool_mixer(h, w, scale):
    B, n, D = h.shape
    G = len(POOL_WINDOWS)
    hf = h.astype(jnp.float32).reshape(B, n, G, POOL_GROUP)
    cs = jnp.concatenate([jnp.zeros((B, 1, G, POOL_GROUP), jnp.float32),
                          jnp.cumsum(hf, axis=1)], axis=1)
    half = jnp.array(POOL_WINDOWS, jnp.int32) // 2
    t = jnp.arange(n, dtype=jnp.int32)[:, None]
    lo = jnp.clip(t - half, 0, n)
    hi = jnp.clip(t + half, 0, n)
    g_idx = jnp.arange(G, dtype=jnp.int32)[None, :]
    win_sum = cs[:, hi, g_idx] - cs[:, lo, g_idx]
    mean = win_sum / (hi - lo).astype(jnp.float32)[None, :, :, None]
    d = (mean - hf).astype(h.dtype)
    y = jnp.einsum('bngi,gio->bngo', d, w).reshape(B, n, D)
    return y * scale


def _chunk_mlp_mixer(h, w_in, v_g, w_s, b_s, w_out):
    B, n, _ = h.shape
    u, v = jnp.split(jax.nn.gelu(h @ w_in), 2, axis=-1)
    v = _rmsnorm(v, v_g).reshape(B, n // CHUNK, CHUNK, GMLP_GROUPS, GMLP_GROUP_DIM)
    sv = jnp.einsum('gts,bcsgd->bctgd', w_s, v) + b_s.T[None, None, :, :, None]
    return (u * sv.reshape(B, n, GMLP_WIDTH)) @ w_out


def _axial_rope_tables(n):
    rows = n // GRID_W
    row = jnp.repeat(jnp.arange(rows, dtype=jnp.float32), GRID_W)
    col = jnp.tile(jnp.arange(GRID_W, dtype=jnp.float32), rows)
    half = ROPE_AXIS_DIM // 2
    freqs = ROPE_THETA ** (-jnp.arange(half, dtype=jnp.float32) / half)
    ang_r = row[:, None] * freqs
    ang_c = col[:, None] * freqs
    return jnp.cos(ang_r), jnp.sin(ang_r), jnp.cos(ang_c), jnp.sin(ang_c)


def _rope_1d(x, cos, sin):
    half = x.shape[-1] // 2
    shp = (1, cos.shape[0]) + (1,) * (x.ndim - 3) + (half,)
    cos, sin = cos.reshape(shp), sin.reshape(shp)
    x1, x2 = x[..., :half], x[..., half:]
    return jnp.concatenate([x1 * cos - x2 * sin, x1 * sin + x2 * cos], axis=-1).astype(x.dtype)


def _rope_axial(x, tabs):
    cos_r, sin_r, cos_c, sin_c = tabs
    return jnp.concatenate([_rope_1d(x[..., :ROPE_AXIS_DIM], cos_r, sin_r),
                            _rope_1d(x[..., ROPE_AXIS_DIM:], cos_c, sin_c)], axis=-1)


def _diff_softmax(q, k, v, lam):
    B, n, H, _, dh = q.shape
    qb = jnp.moveaxis(q.reshape(B, n // Q_BLOCK, Q_BLOCK, H, 2, dh), 1, 0)
    scale = dh ** -0.5

    def block(qblk):
        s = jnp.einsum('bqhcd,bkhcd->bhcqk', qblk, k, preferred_element_type=jnp.float32) * scale
        p = jax.nn.softmax(s, axis=-1)
        a = p[:, :, 0] - lam * p[:, :, 1]
        return jnp.einsum('bhqk,bkhe->bqhe', a.astype(v.dtype), v)

    o = lax.map(block, qb)
    return jnp.moveaxis(o, 0, 1).reshape(B, n, H, 2 * dh)


def _diff_attn_mixer(h, hc, want_ctx_out, w_q, w_k, w_v, w_o, q_g, k_g,
                     lq1, lk1, lq2, lk2, sub_g, lam_init):
    B, n, _ = h.shape
    Lc = hc.shape[1]

    def heads_qk(z, L):
        return z.reshape(B, L, DA_HEADS, 2, DA_HEAD_DIM)

    tabs = _axial_rope_tables(n)
    q = _rope_axial(_rmsnorm(heads_qk(h @ w_q, n), q_g), tabs)
    k = _rope_axial(_rmsnorm(heads_qk(h @ w_k, n), k_g), tabs)
    v = (h @ w_v).reshape(B, n, DA_HEADS, 2 * DA_HEAD_DIM)
    kc = _rmsnorm(heads_qk(hc @ w_k, Lc), k_g)
    vc = (hc @ w_v).reshape(B, Lc, DA_HEADS, 2 * DA_HEAD_DIM)
    f32 = jnp.float32
    lam = (jnp.exp(jnp.sum(lq1.astype(f32) * lk1.astype(f32)))
           - jnp.exp(jnp.sum(lq2.astype(f32) * lk2.astype(f32))) + lam_init)

    def finish(o, L):
        return (_rmsnorm(o, sub_g) * (1 - lam_init)).reshape(B, L, D_MODEL) @ w_o

    y = finish(_diff_softmax(q, jnp.concatenate([kc, k], axis=1),
                             jnp.concatenate([vc, v], axis=1), lam), n)
    if not want_ctx_out:
        return y, None
    qc = _rmsnorm(heads_qk(hc @ w_q, Lc), q_g)
    yc = finish(_diff_softmax(qc, kc, vc, lam), Lc)
    return y, yc


def _expert_choice_ffn(h, w_r, w_gate, w_up, w_down):
    B, n, D = h.shape
    cap = EC_CAPACITY_FACTOR * n // N_EXPERTS
    aff = jax.nn.softmax(jnp.einsum('bnd,de->bne', h, w_r, preferred_element_type=jnp.float32), axis=-1)
    gate, idx = lax.top_k(jnp.swapaxes(aff, 1, 2), cap)

    def one_set(hs, gs, ids):
        xs = hs[ids]
        a = jnp.einsum('ecd,edf->ecf', xs, w_gate)
        b = jnp.einsum('ecd,edf->ecf', xs, w_up)
        y = jnp.einsum('ecf,efd->ecd', jax.nn.silu(a) * b, w_down) * gs[..., None].astype(hs.dtype)
        return jnp.zeros_like(hs).at[ids.reshape(-1)].add(y.reshape(-1, D))

    return jax.vmap(one_set)(h, gate, idx)


def setup_inputs(seed: int = 0) -> dict:
    key = jax.random.key(seed)
    ks = iter(jax.random.split(key, 32))

    def nrm(shape, s):
        return jax.random.normal(next(ks), shape, jnp.float32) * s

    D, E, F = D_MODEL, N_EXPERTS, EXPERT_HIDDEN
    n_pool = _n_layers_of(MIXER_POOL)
    n_chunk = _n_layers_of(MIXER_CHUNK_MLP)
    n_diff = _n_layers_of(MIXER_DIFF_ATTN)
    G = len(POOL_WINDOWS)
    return {
        'x': nrm((BATCH, SEQ, D), 1.0),
        'c': nrm((BATCH, D), 1.0),
        'ctx': nrm((BATCH, CTX_LEN, D), 1.0),
        'c_ctx': nrm((D,), 1.0),
        'ada_w': nrm((DEPTH, D, 6 * D), 0.5 * D ** -0.5),
        'ada_b': nrm((DEPTH, 6 * D), 0.01),
        'norm1_g': 1.0 + nrm((DEPTH, D), 0.02),
        'norm2_g': 1.0 + nrm((DEPTH, D), 0.02),
        'router_w': nrm((DEPTH, D, E), D ** -0.5),
        'exp_w_gate': nrm((DEPTH, E, D, F), D ** -0.5),
        'exp_w_up': nrm((DEPTH, E, D, F), D ** -0.5),
        'exp_w_down': nrm((DEPTH, E, F, D), F ** -0.5),
        'pool_w': nrm((n_pool, G, POOL_GROUP, POOL_GROUP), POOL_GROUP ** -0.5),
        'pool_scale': 1.0 + nrm((n_pool, D), 0.1),
        'gm_w_in': nrm((n_chunk, D, 2 * GMLP_WIDTH), D ** -0.5),
        'gm_v_g': 1.0 + nrm((n_chunk, GMLP_WIDTH), 0.02),
        'gm_w_s': nrm((n_chunk, GMLP_GROUPS, CHUNK, CHUNK), CHUNK ** -0.5),
        'gm_b_s': 1.0 + nrm((n_chunk, GMLP_GROUPS, CHUNK), 0.1),
        'gm_w_out': nrm((n_chunk, GMLP_WIDTH, D), GMLP_WIDTH ** -0.5),
        'da_w_q': nrm((n_diff, D, DA_HEADS * 2 * DA_HEAD_DIM), D ** -0.5),
        'da_w_k': nrm((n_diff, D, DA_HEADS * 2 * DA_HEAD_DIM), D ** -0.5),
        'da_w_v': nrm((n_diff, D, DA_HEADS * 2 * DA_HEAD_DIM), D ** -0.5),
        'da_w_o': nrm((n_diff, DA_HEADS * 2 * DA_HEAD_DIM, D), D ** -0.5),
        'da_q_g': 1.0 + nrm((n_diff, DA_HEAD_DIM), 0.02),
        'da_k_g': 1.0 + nrm((n_diff, DA_HEAD_DIM), 0.02),
        'da_lam_q1': nrm((n_diff, DA_HEAD_DIM), 0.1),
        'da_lam_k1': nrm((n_diff, DA_HEAD_DIM), 0.1),
        'da_lam_q2': nrm((n_diff, DA_HEAD_DIM), 0.1),
        'da_lam_k2': nrm((n_diff, DA_HEAD_DIM), 0.1),
        'da_sub_g': 1.0 + nrm((n_diff, 2 * DA_HEAD_DIM), 0.02),
    }


def reference(x, c, ctx, c_ctx, ada_w, ada_b, norm1_g, norm2_g, router_w,
              exp_w_gate, exp_w_up, exp_w_down, pool_w, pool_scale,
              gm_w_in, gm_v_g, gm_w_s, gm_b_s, gm_w_out,
              da_w_q, da_w_k, da_w_v, da_w_o, da_q_g, da_k_g,
              da_lam_q1, da_lam_k1, da_lam_q2, da_lam_k2, da_sub_g):
    silu_c = jax.nn.silu(c)
    silu_cc = jax.nn.silu(c_ctx)[None]
    for i in range(DEPTH):
        kind, slot = i % N_MIXERS, i // N_MIXERS
        keep_ctx = _ctx_read_after(i)
        ctx_in = keep_ctx or kind == MIXER_DIFF_ATTN
        sh1, sc1, g1, sh2, sc2, g2 = jnp.split(silu_c @ ada_w[i] + ada_b[i], 6, axis=-1)
        h = _modulate(_rmsnorm(x, norm1_g[i]), sh1, sc1)
        if ctx_in:
            csh1, csc1, cg1, csh2, csc2, cg2 = jnp.split(silu_cc @ ada_w[i] + ada_b[i], 6, axis=-1)
            hc = _modulate(_rmsnorm(ctx, norm1_g[i]), csh1, csc1)
        yc = None
        if kind == MIXER_POOL:
            y = _pool_mixer(h, pool_w[slot], pool_scale[slot])
            if keep_ctx:
                yc = _pool_mixer(hc, pool_w[slot], pool_scale[slot])
        elif kind == MIXER_CHUNK_MLP:
            y = _chunk_mlp_mixer(h, gm_w_in[slot], gm_v_g[slot], gm_w_s[slot], gm_b_s[slot], gm_w_out[slot])
            if keep_ctx:
                yc = _chunk_mlp_mixer(hc, gm_w_in[slot], gm_v_g[slot], gm_w_s[slot], gm_b_s[slot], gm_w_out[slot])
        else:
            lam_init = 0.8 - 0.6 * math.exp(-0.3 * i)
            y, yc = _diff_attn_mixer(h, hc, keep_ctx, da_w_q[slot], da_w_k[slot], da_w_v[slot], da_w_o[slot],
                                     da_q_g[slot], da_k_g[slot], da_lam_q1[slot], da_lam_k1[slot],
                                     da_lam_q2[slot], da_lam_k2[slot], da_sub_g[slot], lam_init)
        x = x + g1[:, None, :] * y
        h2 = _modulate(_rmsnorm(x, norm2_g[i]), sh2, sc2)
        x = x + g2[:, None, :] * _expert_choice_ffn(h2, router_w[i], exp_w_gate[i], exp_w_up[i], exp_w_down[i])
        if keep_ctx:
            ctx = ctx + cg1[:, None, :] * yc
            hc2 = _modulate(_rmsnorm(ctx, norm2_g[i]), csh2, csc2)
            ctx = ctx + cg2[:, None, :] * _expert_choice_ffn(hc2, router_w[i], exp_w_gate[i], exp_w_up[i], exp_w_down[i])
    return x
```

```python
import functools
import math

import jax
import jax.numpy as jnp
from jax import lax
from jax.experimental import pallas as pl
from jax.experimental.pallas import tpu as pltpu

F32 = jnp.float32
BF16 = jnp.bfloat16
I32 = jnp.int32

NORM_EPS = 1e-6
LANES = 128
GRID_W = 64
CHUNK = 128
POOL_WINDOWS = (2, 4, 8, 16)
HEAD_DIM = 128
ROPE_THETA = 10000.0
EC_CAPACITY_FACTOR = 2
N_MIXERS = 3
VMEM_LIMIT = 56 * 1024 * 1024


def _params(*sem):
    return pltpu.CompilerParams(dimension_semantics=sem, vmem_limit_bytes=VMEM_LIMIT)


def _tile(n, t):
    t = min(n, t)
    assert n % t == 0, (n, t)
    return t


def _norm_mod(x, g, sh, sc):
    ms = jnp.mean(x * x, axis=-1, keepdims=True)
    return (x * lax.rsqrt(ms + NORM_EPS) * g) * (1.0 + sc) + sh


def _adaln_kernel(s_ref, w_ref, b_ref, o_ref):
    s = s_ref[...]
    s = s * jax.nn.sigmoid(s)
    o_ref[0] = jnp.dot(s, w_ref[0], preferred_element_type=F32) + b_ref[0]


def adaln(s8, ada_w, ada_b):
    depth, d, n6 = ada_w.shape
    tn = _tile(n6, 1024)
    return pl.pallas_call(
        _adaln_kernel,
        grid=(depth, n6 // tn),
        in_specs=[pl.BlockSpec((8, d), lambda l, j: (0, 0)),
                  pl.BlockSpec((1, d, tn), lambda l, j: (l, 0, j)),
                  pl.BlockSpec((1, 1, tn), lambda l, j: (l, 0, j))],
        out_specs=pl.BlockSpec((1, 8, tn), lambda l, j: (l, 0, j)),
        out_shape=jax.ShapeDtypeStruct((depth, 8, n6), F32),
        compiler_params=_params("parallel", "parallel"),
        name="adaln",
    )(s8, ada_w, ada_b.reshape(depth, 1, n6))


def _norm_mod_kernel(x_ref, g_ref, sh_ref, sc_ref, o_ref):
    o_ref[...] = _norm_mod(x_ref[...], g_ref[...], sh_ref[...], sc_ref[...]).astype(o_ref.dtype)


def norm_mod(x, g, sh, sc, dtype):
    n, d = x.shape
    tm = _tile(n, 256)
    vec = pl.BlockSpec((1, d), lambda i: (0, 0))
    return pl.pallas_call(
        _norm_mod_kernel,
        grid=(n // tm,),
        in_specs=[pl.BlockSpec((tm, d), lambda i: (i, 0)), vec, vec, vec],
        out_specs=pl.BlockSpec((tm, d), lambda i: (i, 0)),
        out_shape=jax.ShapeDtypeStruct((n, d), dtype),
        compiler_params=_params("parallel"),
        name="norm_mod",
    )(x, g, sh, sc)


def _pool_kernel(cur_ref, prev_ref, next_ref, x_ref, w_ref, ps_ref, g1_ref, o_ref, *, n, tm):
    g = pl.program_id(0)
    i = pl.program_id(1)
    last = pl.num_programs(1) - 1
    cur = cur_ref[...]
    prev = jnp.where(i == 0, 0.0, prev_ref[...])
    nxt = jnp.where(i == last, 0.0, next_ref[...])
    ext = jnp.concatenate([prev, cur, nxt], axis=0)
    t = i * tm + lax.broadcasted_iota(I32, (tm, 1), 0)

    for gi, win in enumerate(POOL_WINDOWS):
        @pl.when(g == gi)
        def _(win=win):
            half = win // 2
            s = ext
            step = 1
            while step < win:
                m = s.shape[0] - step
                s = s[:m] + s[step:step + m]
                step *= 2
            wsum = s[8 - half:8 - half + tm]
            cnt = jnp.minimum(t + half, n) - jnp.maximum(t - half, 0)
            dlt = (wsum / cnt.astype(F32) - cur).astype(BF16)
            y = jnp.dot(dlt, w_ref[0], preferred_element_type=F32) * ps_ref[...]
            o_ref[...] = x_ref[...] + g1_ref[...] * y


def pool_mixer(h, x, w, pscale, g1):
    n, d = x.shape
    ng, dg, _ = w.shape
    tm = _tile(n, 512)
    nb8 = n // 8
    kern = functools.partial(_pool_kernel, n=n, tm=tm)
    col = pl.BlockSpec((1, dg), lambda g, i: (0, g))
    return pl.pallas_call(
        kern,
        grid=(ng, n // tm),
        in_specs=[pl.BlockSpec((tm, dg), lambda g, i: (i, g)),
                  pl.BlockSpec((8, dg), lambda g, i: (jnp.maximum(i * (tm // 8) - 1, 0), g)),
                  pl.BlockSpec((8, dg), lambda g, i: (jnp.minimum((i + 1) * (tm // 8), nb8 - 1), g)),
                  pl.BlockSpec((tm, dg), lambda g, i: (i, g)),
                  pl.BlockSpec((1, dg, dg), lambda g, i: (g, 0, 0)),
                  col, col],
        out_specs=pl.BlockSpec((tm, dg), lambda g, i: (i, g)),
        out_shape=jax.ShapeDtypeStruct((n, d), F32),
        compiler_params=_params("parallel", "parallel"),
        name="pool_mixer",
    )(h, h, h, x, w, pscale, g1)


def _mm_call(kern, a, w, extra, extra_specs, out_shape, out_specs, tm, tn, name, scratch=()):
    m, k = a.shape
    n = w.shape[1]
    return pl.pallas_call(
        kern,
        grid=(pl.cdiv(m, tm), n // tn),
        in_specs=[pl.BlockSpec((tm, k), lambda i, j: (i, 0)),
                  pl.BlockSpec((k, tn), lambda i, j: (0, j))] + list(extra_specs),
        out_specs=out_specs,
        out_shape=out_shape,
        scratch_shapes=list(scratch),
        compiler_params=_params("parallel", "arbitrary"),
        name=name,
    )(a, w, *extra)


def _mm_plain_kernel(a_ref, w_ref, o_ref):
    o_ref[...] = jnp.dot(a_ref[...], w_ref[...], preferred_element_type=F32).astype(o_ref.dtype)


def mm_plain(a, w, dtype=BF16):
    m, _ = a.shape
    n = w.shape[1]
    tm, tn = min(m, 1024), _tile(n, 512)
    return _mm_call(_mm_plain_kernel, a, w, (), (), jax.ShapeDtypeStruct((m, n), dtype),
                    pl.BlockSpec((tm, tn), lambda i, j: (i, j)), tm, tn, "mm_plain")


def _mm_qk_kernel(a_ref, w_ref, g_ref, cos_ref, sin_ref, o_ref, *, scale):
    y = jnp.dot(a_ref[...], w_ref[...], preferred_element_type=F32)
    tn = y.shape[1]
    gain = g_ref[...]
    cos = cos_ref[...]
    sin = sin_ref[...]
    lane = lax.broadcasted_iota(I32, (1, HEAD_DIM), 1)
    first = (lane % 64) < 32
    for c in range(tn // HEAD_DIM):
        z = y[:, c * HEAD_DIM:(c + 1) * HEAD_DIM]
        z = z * lax.rsqrt(jnp.mean(z * z, axis=-1, keepdims=True) + NORM_EPS) * gain
        partner = jnp.where(first, pltpu.roll(z, HEAD_DIM - 32, 1), pltpu.roll(z, 32, 1))
        o_ref[:, c * HEAD_DIM:(c + 1) * HEAD_DIM] = ((z * cos + partner * sin) * scale).astype(o_ref.dtype)


def mm_qk(a, w, gain, cos, sin, scale):
    m, _ = a.shape
    n = w.shape[1]
    tm, tn = min(m, 1024), _tile(n, 512)
    kern = functools.partial(_mm_qk_kernel, scale=scale)
    tab = pl.BlockSpec((tm, HEAD_DIM), lambda i, j: (i, 0))
    return _mm_call(kern, a, w, (gain, cos, sin),
                    (pl.BlockSpec((1, HEAD_DIM), lambda i, j: (0, 0)), tab, tab),
                    jax.ShapeDtypeStruct((m, n), BF16),
                    pl.BlockSpec((tm, tn), lambda i, j: (i, j)), tm, tn, "mm_qk")


def _mm_gelu_kernel(a_ref, w_ref, o_ref, ssq_ref, *, nj_half):
    j = pl.program_id(1)
    y = jax.nn.gelu(jnp.dot(a_ref[...], w_ref[...], preferred_element_type=F32))
    o_ref[...] = y.astype(o_ref.dtype)

    @pl.when(j == nj_half)
    def _():
        ssq_ref[...] = jnp.zeros_like(ssq_ref)

    @pl.when(j >= nj_half)
    def _():
        ssq_ref[...] += jnp.sum(y * y, axis=-1, keepdims=True)


def mm_gelu(a, w):
    m, _ = a.shape
    n = w.shape[1]
    tm, tn = min(m, 1024), _tile(n // 2, 512)
    kern = functools.partial(_mm_gelu_kernel, nj_half=(n // 2) // tn)
    return _mm_call(kern, a, w, (), (),
                    (jax.ShapeDtypeStruct((m, n), BF16), jax.ShapeDtypeStruct((m, LANES), F32)),
                    (pl.BlockSpec((tm, tn), lambda i, j: (i, j)),
                     pl.BlockSpec((tm, LANES), lambda i, j: (i, 0))), tm, tn, "mm_gelu")


def _mm_resid_kernel(a_ref, w_ref, x_ref, g_ref, o_ref):
    y = jnp.dot(a_ref[...], w_ref[...], preferred_element_type=F32)
    o_ref[...] = x_ref[...] + g_ref[...] * y


def mm_resid(a, w, x, g1):
    m, _ = a.shape
    n = w.shape[1]
    tm, tn = min(m, 1024), _tile(n, 512)
    blk = pl.BlockSpec((tm, tn), lambda i, j: (i, j))
    return _mm_call(_mm_resid_kernel, a, w, (x, g1),
                    (blk, pl.BlockSpec((1, tn), lambda i, j: (0, j))),
                    jax.ShapeDtypeStruct((m, n), F32), blk, tm, tn, "mm_resid")


def _gate_mm_kernel(u_ref, v_ref, ssq_ref, vg_ref, ws_ref, bs_ref, w_ref, x_ref, g_ref, o_ref, z_scr, *, width):
    j = pl.program_id(1)
    tm = u_ref.shape[0]

    @pl.when(j == 0)
    def _():
        rstd = lax.rsqrt(ssq_ref[:, :1] * (1.0 / width) + NORM_EPS)

        def body(g, carry):
            col = pl.multiple_of(g * LANES, LANES)
            ws = ws_ref[g]
            for c in range(tm // CHUNK):
                rows = slice(c * CHUNK, (c + 1) * CHUNK)
                vn = (v_ref[rows, pl.ds(col, LANES)].astype(F32) * rstd[rows]
                      * vg_ref[:, pl.ds(col, LANES)]).astype(BF16)
                sv = jnp.dot(ws, vn, preferred_element_type=F32) + bs_ref[:, pl.ds(col, LANES)]
                z_scr[rows, pl.ds(col, LANES)] = (u_ref[rows, pl.ds(col, LANES)].astype(F32) * sv).astype(BF16)
            return carry

        lax.fori_loop(0, width // LANES, body, 0)

    y = jnp.dot(z_scr[...], w_ref[...], preferred_element_type=F32)
    o_ref[...] = x_ref[...] + g_ref[...] * y


def gate_mm_resid(uv, ssq, vg, ws, bs_full, w_out, x, g1):
    n, w2 = uv.shape
    width = w2 // 2
    d = w_out.shape[1]
    tm, tn = _tile(n, 512), _tile(d, 512)
    kern = functools.partial(_gate_mm_kernel, width=width)
    blk = pl.BlockSpec((tm, tn), lambda i, j: (i, j))
    return pl.pallas_call(
        kern,
        grid=(n // tm, d // tn),
        in_specs=[pl.BlockSpec((tm, width), lambda i, j: (i, 0)),
                  pl.BlockSpec((tm, width), lambda i, j: (i, 1)),
                  pl.BlockSpec((tm, LANES), lambda i, j: (i, 0)),
                  pl.BlockSpec((1, width), lambda i, j: (0, 0)),
                  pl.BlockSpec(ws.shape, lambda i, j: (0, 0, 0)),
                  pl.BlockSpec((CHUNK, width), lambda i, j: (0, 0)),
                  pl.BlockSpec((width, tn), lambda i, j: (0, j)),
                  blk,
                  pl.BlockSpec((1, tn), lambda i, j: (0, j))],
        out_specs=blk,
        out_shape=jax.ShapeDtypeStruct((n, d), F32),
        scratch_shapes=[pltpu.VMEM((tm, width), BF16)],
        compiler_params=_params("parallel", "arbitrary"),
        name="gate_mm_resid",
    )(uv, uv, ssq, vg, ws, bs_full, w_out, x, g1)


def _attn_kernel(lamv_ref, q_ref, k_ref, v_ref, sg_ref, o_ref, m_scr, l_scr, acc_scr, *, lam_init):
    ki = pl.program_id(2)

    @pl.when(ki == 0)
    def _():
        m_scr[...] = jnp.full_like(m_scr, -jnp.inf)
        l_scr[...] = jnp.zeros_like(l_scr)
        acc_scr[...] = jnp.zeros_like(acc_scr)

    v = v_ref[...]
    for c in range(2):
        cols = slice(c * HEAD_DIM, (c + 1) * HEAD_DIM)
        s = lax.dot_general(q_ref[:, cols], k_ref[:, cols], (((1,), (1,)), ((), ())),
                            preferred_element_type=F32)
        m_prev = m_scr[c]
        m_new = jnp.maximum(m_prev, jnp.max(s, axis=-1, keepdims=True))
        alpha = jnp.exp(m_prev - m_new)
        p = jnp.exp(s - m_new)
        l_scr[c] = alpha * l_scr[c] + jnp.sum(p, axis=-1, keepdims=True)
        acc_scr[c] = alpha * acc_scr[c] + jnp.dot(p.astype(BF16), v, preferred_element_type=F32)
        m_scr[c] = m_new

    @pl.when(ki == pl.num_programs(2) - 1)
    def _():
        lv = lamv_ref[...]
        lam = (jnp.exp(jnp.sum(lv[0:1] * lv[1:2], axis=-1, keepdims=True))
               - jnp.exp(jnp.sum(lv[2:3] * lv[3:4], axis=-1, keepdims=True)) + lam_init)
        o = acc_scr[0] / l_scr[0] - lam * (acc_scr[1] / l_scr[1])
        o = o * lax.rsqrt(jnp.mean(o * o, axis=-1, keepdims=True) + NORM_EPS) * sg_ref[...]
        o_ref[...] = (o * (1.0 - lam_init)).astype(o_ref.dtype)


def _key_tile(nk, cap):
    best = LANES
    for t in range(LANES, cap + 1, LANES):
        if nk % t == 0:
            best = t
    return best


def diff_attention(q, k, v, lamv, sub_g, lam_init):
    n, d = q.shape
    nk = k.shape[0]
    hw = 2 * HEAD_DIM
    heads = d // hw
    tq = _tile(n, 1024)
    tk = _key_tile(nk, 1536)
    kern = functools.partial(_attn_kernel, lam_init=lam_init)
    return pl.pallas_call(
        kern,
        grid=(heads, n // tq, nk // tk),
        in_specs=[pl.BlockSpec((4, HEAD_DIM), lambda h, i, j: (0, 0)),
                  pl.BlockSpec((tq, hw), lambda h, i, j: (i, h)),
                  pl.BlockSpec((tk, hw), lambda h, i, j: (j, h)),
                  pl.BlockSpec((tk, hw), lambda h, i, j: (j, h)),
                  pl.BlockSpec((1, hw), lambda h, i, j: (0, 0))],
        out_specs=pl.BlockSpec((tq, hw), lambda h, i, j: (i, h)),
        out_shape=jax.ShapeDtypeStruct((n, d), BF16),
        scratch_shapes=[pltpu.VMEM((2, tq, 1), F32), pltpu.VMEM((2, tq, 1), F32),
                        pltpu.VMEM((2, tq, hw), F32)],
        compiler_params=_params("parallel", "parallel", "arbitrary"),
        name="diff_attention",
    )(lamv, q, k, v, sub_g)


def _router_kernel(x_ref, g_ref, sh_ref, sc_ref, wrt_ref, h_ref, aff_ref):
    h = _norm_mod(x_ref[...], g_ref[...], sh_ref[...], sc_ref[...])
    h_ref[...] = h
    logits = lax.dot_general(wrt_ref[...], h, (((1,), (1,)), ((), ())),
                             preferred_element_type=F32, precision=lax.Precision.HIGHEST)
    ex = jnp.exp(logits - jnp.max(logits, axis=0, keepdims=True))
    aff_ref[...] = ex / jnp.sum(ex, axis=0, keepdims=True)


def router(x, g, sh, sc, w_r_t):
    n, d = x.shape
    e = w_r_t.shape[0]
    tm = _tile(n, 256)
    vec = pl.BlockSpec((1, d), lambda i: (0, 0))
    return pl.pallas_call(
        _router_kernel,
        grid=(n // tm,),
        in_specs=[pl.BlockSpec((tm, d), lambda i: (i, 0)), vec, vec, vec,
                  pl.BlockSpec((e, d), lambda i: (0, 0))],
        out_specs=(pl.BlockSpec((tm, d), lambda i: (i, 0)), pl.BlockSpec((e, tm), lambda i: (0, i))),
        out_shape=(jax.ShapeDtypeStruct((n, d), F32), jax.ShapeDtypeStruct((e, n), F32)),
        compiler_params=_params("parallel"),
        name="router",
    )(x, g, sh, sc, w_r_t)


def _split3(a):
    hi = a.astype(BF16)
    r1 = a - hi.astype(F32)
    mid = r1.astype(BF16)
    lo = (r1 - mid.astype(F32)).astype(BF16)
    return hi, mid, lo


def _topk_kernel(abt_ref, atb_ref, idx_ref, gate_ref, thr_scr, cut_scr, *, cap, cap_pad):
    ne, nb, _ = abt_ref.shape
    ntok = nb * LANES
    bits = pltpu.bitcast(abt_ref[...], I32)
    tok_bt = (lax.broadcasted_iota(I32, (1, nb, LANES), 1) * LANES
              + lax.broadcasted_iota(I32, (1, nb, LANES), 2))

    def count(ones):
        c = jnp.sum(ones, axis=2, keepdims=True)
        return jnp.sum(c, axis=1, keepdims=True)

    def thr_body(_, carry):
        lo, hi = carry
        mid = lo + ((hi - lo + 1) >> 1)
        ok = count(jnp.where(bits >= mid, 1, 0)) >= cap
        return jnp.where(ok, mid, lo), jnp.where(ok, hi, mid - 1)

    lo0 = jnp.zeros((ne, 1, 1), I32)
    hi0 = jnp.full((ne, 1, 1), 0x7F800000, I32)
    thr, _ = lax.fori_loop(0, 31, thr_body, (lo0, hi0))

    tok_eq = jnp.where(bits == thr, tok_bt, ntok)
    need = cap - count(jnp.where(bits > thr, 1, 0))

    def cut_body(_, carry):
        lo, hi = carry
        mid = (lo + hi) >> 1
        ok = count(jnp.where(tok_eq < mid, 1, 0)) >= need
        return jnp.where(ok, lo, mid), jnp.where(ok, mid, hi)

    _, cut = lax.fori_loop(0, int(math.log2(ntok)) + 1, cut_body,
                           (jnp.zeros((ne, 1, 1), I32), jnp.full((ne, 1, 1), ntok, I32)))
    thr_scr[...] = jnp.broadcast_to(thr, thr_scr.shape)
    cut_scr[...] = jnp.broadcast_to(cut, cut_scr.shape)

    r_i = lax.broadcasted_iota(I32, (LANES, LANES), 0)
    c_i = lax.broadcasted_iota(I32, (LANES, LANES), 1)
    tri_t = jnp.where(c_i <= r_i, 1.0, 0.0).astype(BF16)
    rb_i = lax.broadcasted_iota(I32, (nb, nb), 0)
    cb_i = lax.broadcasted_iota(I32, (nb, nb), 1)
    tri_b = jnp.where(cb_i <= rb_i, 1.0, 0.0).astype(BF16)
    tok_b = lax.broadcasted_iota(I32, (nb, LANES), 0) * LANES + lax.broadcasted_iota(I32, (nb, LANES), 1)
    tok_t = lax.broadcasted_iota(I32, (LANES, nb), 1) * LANES + lax.broadcasted_iota(I32, (LANES, nb), 0)
    slot = lax.broadcasted_iota(I32, (1, cap_pad), 1).astype(F32)
    blk_iota = lax.broadcasted_iota(I32, (nb, 1), 0).astype(F32)
    lane_iota = lax.broadcasted_iota(I32, (LANES, 1), 0).astype(F32)

    def select(b, tok, th, ct):
        return jnp.where(b > th, 1.0, jnp.where(b == th, jnp.where(tok < ct, 1.0, 0.0), 0.0))

    def per_expert(e, carry):
        th = thr_scr[e][:1, :1]
        ct = cut_scr[e][:1, :1]
        a_tb = atb_ref[e]
        sel_bt = select(pltpu.bitcast(abt_ref[e], I32), tok_b, th, ct)
        sel_tb = select(pltpu.bitcast(a_tb, I32), tok_t, th, ct)
        cw_t = jnp.dot(tri_t, sel_tb.astype(BF16), preferred_element_type=F32)
        tot = jnp.sum(sel_bt, axis=1, keepdims=True)
        incl = jnp.dot(tri_b, jnp.broadcast_to(tot, (nb, LANES)).astype(BF16),
                       preferred_element_type=F32)[:, :1]
        excl = incl - tot
        blk = jnp.sum(jnp.where(incl <= slot, 1.0, 0.0), axis=0, keepdims=True)
        onehot = jnp.where(blk_iota == blk, 1.0, 0.0)
        oh16 = onehot.astype(BF16)
        g_t = jnp.dot(cw_t.astype(BF16), oh16, preferred_element_type=F32)
        base = jnp.sum(onehot * excl, axis=0, keepdims=True)
        tl = jnp.sum(jnp.where(g_t + base <= slot, 1.0, 0.0), axis=0, keepdims=True)
        idx_ref[pl.ds(e, 1), :] = (blk * LANES + tl).astype(I32)
        hi, mid, lo = _split3(a_tb)
        rows = (jnp.dot(hi, oh16, preferred_element_type=F32)
                + jnp.dot(mid, oh16, preferred_element_type=F32)
                + jnp.dot(lo, oh16, preferred_element_type=F32))
        gate_ref[pl.ds(e, 1), :] = jnp.sum(jnp.where(lane_iota == tl, rows, 0.0), axis=0, keepdims=True)
        return carry

    lax.fori_loop(0, ne, per_expert, 0)


def topk_select(aff_t, cap):
    ne, n = aff_t.shape
    nb = max(LANES, -(-n // LANES))
    nb = -(-nb // LANES) * LANES
    cap_pad = max(LANES, cap)
    padded = jnp.pad(aff_t, ((0, 0), (0, nb * LANES - n)), constant_values=-1.0)
    a_bt = padded.reshape(ne, nb, LANES)
    a_tb = jnp.swapaxes(a_bt, 1, 2)
    kern = functools.partial(_topk_kernel, cap=cap, cap_pad=cap_pad)
    idx, gate = pl.pallas_call(
        kern,
        out_shape=(jax.ShapeDtypeStruct((ne, cap_pad), I32), jax.ShapeDtypeStruct((ne, cap_pad), F32)),
        scratch_shapes=[pltpu.VMEM((ne, 8, LANES), I32), pltpu.VMEM((ne, 8, LANES), I32)],
        compiler_params=pltpu.CompilerParams(vmem_limit_bytes=VMEM_LIMIT),
        name="topk_select",
    )(a_bt, a_tb)
    return idx[:, :cap], gate[:, :cap]


def _moe_kernel(idx_ref, gate_ref, wg_ref, wu_ref, wd_ref, g2_ref, h_hbm, xin_hbm, out_hbm,
                xs, ab, sems, *, cap, tc):
    del xin_hbm
    e = pl.program_id(0)
    base = e * cap + pl.program_id(1) * tc

    def in_copies(r):
        t = idx_ref[base + r]
        return (pltpu.make_async_copy(h_hbm.at[pl.ds(t, 1)], xs.at[pl.ds(r, 1)], sems.at[0]),
                pltpu.make_async_copy(out_hbm.at[pl.ds(t, 1)], ab.at[pl.ds(r, 1)], sems.at[1]))

    def out_copy(r):
        t = idx_ref[base + r]
        return pltpu.make_async_copy(ab.at[pl.ds(r, 1)], out_hbm.at[pl.ds(t, 1)], sems.at[2])

    def start_in(r, c):
        cx, ca = in_copies(r)
        cx.start()
        ca.start()
        return c

    def wait_in(r, c):
        cx, ca = in_copies(r)
        cx.wait()
        ca.wait()
        return c

    def start_out(r, c):
        out_copy(r).start()
        return c

    def wait_out(r, c):
        out_copy(r).wait()
        return c

    lax.fori_loop(0, tc, start_in, 0)
    lax.fori_loop(0, tc, wait_in, 0)
    x = xs[...].astype(BF16)
    a = jnp.dot(x, wg_ref[0], preferred_element_type=F32)
    b = jnp.dot(x, wu_ref[0], preferred_element_type=F32)
    f = a.shape[1]
    gate = jnp.concatenate([gate_ref[0]] * (f // LANES), axis=1)
    hm = (a * jax.nn.sigmoid(a) * b * gate).astype(BF16)
    y = jnp.dot(hm, wd_ref[0], preferred_element_type=F32)
    ab[...] = ab[...] + g2_ref[...] * y
    lax.fori_loop(0, tc, start_out, 0)
    lax.fori_loop(0, tc, wait_out, 0)


def moe_apply(h, x, idx, gate, wg, wu, wd, g2):
    n, d = x.shape
    ne, cap = idx.shape
    f = wg.shape[2]
    assert f % LANES == 0
    tc = _tile(cap, 256)
    gate_b = jnp.broadcast_to(gate[:, :, None], (ne, cap, LANES))
    kern = functools.partial(_moe_kernel, cap=cap, tc=tc)
    grid_spec = pltpu.PrefetchScalarGridSpec(
        num_scalar_prefetch=1,
        grid=(ne, cap // tc),
        in_specs=[pl.BlockSpec((1, tc, LANES), lambda e, c, idx: (e, c, 0)),
                  pl.BlockSpec((1, d, f), lambda e, c, idx: (e, 0, 0)),
                  pl.BlockSpec((1, d, f), lambda e, c, idx: (e, 0, 0)),
                  pl.BlockSpec((1, f, d), lambda e, c, idx: (e, 0, 0)),
                  pl.BlockSpec((1, d), lambda e, c, idx: (0, 0)),
                  pl.BlockSpec(memory_space=pl.ANY),
                  pl.BlockSpec(memory_space=pl.ANY)],
        out_specs=pl.BlockSpec(memory_space=pl.ANY),
        scratch_shapes=[pltpu.VMEM((tc, d), F32), pltpu.VMEM((tc, d), F32),
                        pltpu.SemaphoreType.DMA((3,))],
    )
    return pl.pallas_call(
        kern,
        grid_spec=grid_spec,
        out_shape=jax.ShapeDtypeStruct((n, d), F32),
        input_output_aliases={7: 0},
        compiler_params=_params("arbitrary", "arbitrary"),
        name="moe_apply",
    )(idx.reshape(-1), gate_b, wg, wu, wd, g2, h, x)


def moe_layer(x, g, sh, sc, g2, w_r, wg, wu, wd):
    n = x.shape[0]
    ne = w_r.shape[1]
    h, aff_t = router(x, g, sh, sc, w_r.T)
    idx, gate = topk_select(aff_t, EC_CAPACITY_FACTOR * n // ne)
    return moe_apply(h, x, idx, gate, wg, wu, wd, g2)


def _rope_tables(n, n_ctx):
    quarter = HEAD_DIM // 4
    t = jnp.arange(n, dtype=I32)
    freqs = ROPE_THETA ** (-jnp.arange(quarter, dtype=F32) / quarter)
    ang_r = (t // GRID_W).astype(F32)[:, None] * freqs
    ang_c = (t % GRID_W).astype(F32)[:, None] * freqs
    cos = jnp.concatenate([jnp.cos(ang_r)] * 2 + [jnp.cos(ang_c)] * 2, axis=1)
    sin = jnp.concatenate([-jnp.sin(ang_r), jnp.sin(ang_r), -jnp.sin(ang_c), jnp.sin(ang_c)], axis=1)
    cos = jnp.concatenate([jnp.ones((n_ctx, HEAD_DIM), F32), cos], axis=0)
    sin = jnp.concatenate([jnp.zeros((n_ctx, HEAD_DIM), F32), sin], axis=0)
    return cos, sin


def kernel(x, c, ctx, c_ctx, ada_w, ada_b, norm1_g, norm2_g, router_w, exp_w_gate, exp_w_up, exp_w_down,
           pool_w, pool_scale, gm_w_in, gm_v_g, gm_w_s, gm_b_s, gm_w_out, da_w_q, da_w_k, da_w_v, da_w_o,
           da_q_g, da_k_g, da_lam_q1, da_lam_k1, da_lam_q2, da_lam_k2, da_sub_g):
    bsz, n, d = x.shape
    assert bsz == 1 and c.shape[0] == 1
    n_ctx = ctx.shape[1]
    depth = ada_w.shape[0]
    xs, cs = x[0], ctx[0]

    s8 = jnp.concatenate([c, c_ctx[None], jnp.zeros((6, d), F32)], axis=0)
    mod = adaln(s8, ada_w, ada_b)

    def mods(i, row):
        return [mod[i, row:row + 1, k * d:(k + 1) * d] for k in range(6)]

    def row(v):
        return v.reshape(1, -1)

    for i in range(depth):
        kind, slot = i % N_MIXERS, i // N_MIXERS
        keep_ctx = any(j % N_MIXERS == 2 for j in range(i + 1, depth))
        sh1, sc1, g1, sh2, sc2, g2 = mods(i, 0)
        csh1, csc1, cg1, csh2, csc2, cg2 = mods(i, 1)
        n1, n2 = row(norm1_g[i]), row(norm2_g[i])
        streams = [(xs, sh1, sc1, g1)] + ([(cs, csh1, csc1, cg1)] if keep_ctx else [])

        if kind == 0:
            w = pool_w[slot].astype(BF16)
            outs = [pool_mixer(norm_mod(s, n1, sh, sc, F32), s, w, row(pool_scale[slot]), g)
                    for s, sh, sc, g in streams]
        elif kind == 1:
            w_in = gm_w_in[slot].astype(BF16)
            w_out = gm_w_out[slot].astype(BF16)
            w_s = gm_w_s[slot].astype(BF16)
            width = w_out.shape[0]
            bs_full = jnp.repeat(gm_b_s[slot].T, width // gm_w_s.shape[1], axis=1)
            outs = []
            for s, sh, sc, g in streams:
                uv, ssq = mm_gelu(norm_mod(s, n1, sh, sc, BF16), w_in)
                outs.append(gate_mm_resid(uv, ssq, row(gm_v_g[slot]), w_s, bs_full, w_out, s, g))
        else:
            assert not keep_ctx
            lam_init = 0.8 - 0.6 * math.exp(-0.3 * i)
            h_all = jnp.concatenate([norm_mod(cs, n1, csh1, csc1, BF16), norm_mod(xs, n1, sh1, sc1, BF16)], axis=0)
            cos, sin = _rope_tables(n, n_ctx)
            q = mm_qk(h_all[n_ctx:], da_w_q[slot].astype(BF16), row(da_q_g[slot]), cos[n_ctx:], sin[n_ctx:],
                      HEAD_DIM ** -0.5)
            k = mm_qk(h_all, da_w_k[slot].astype(BF16), row(da_k_g[slot]), cos, sin, 1.0)
            v = mm_plain(h_all, da_w_v[slot].astype(BF16))
            lamv = jnp.stack([da_lam_q1[slot], da_lam_k1[slot], da_lam_q2[slot], da_lam_k2[slot]])
            o = diff_attention(q, k, v, lamv, row(da_sub_g[slot]), lam_init)
            outs = [mm_resid(o, da_w_o[slot].astype(BF16), xs, g1)]

        wg, wu, wd = (exp_w_gate[i].astype(BF16), exp_w_up[i].astype(BF16), exp_w_down[i].astype(BF16))
        xs = moe_layer(outs[0], n2, sh2, sc2, g2, router_w[i], wg, wu, wd)
        if keep_ctx:
            cs = moe_layer(outs[1], n2, csh2, csc2, cg2, router_w[i], wg, wu, wd)
    return xs[None]
```

```python
import functools
import math

import jax
import jax.numpy as jnp
from jax import lax
from jax.experimental import pallas as pl
from jax.experimental.pallas import tpu as pltpu

F32 = jnp.float32
BF16 = jnp.bfloat16
I32 = jnp.int32
U32 = jnp.uint32

NORM_EPS = 1e-6
LANES = 128
GRID_W = 64
CHUNK = 128
POOL_WINDOWS = (2, 4, 8, 16)
HEAD_DIM = 128
ROPE_THETA = 10000.0
EC_CAPACITY_FACTOR = 2
N_MIXERS = 3
VMEM_LIMIT = 56 * 1024 * 1024


def _params(*sem):
    return pltpu.CompilerParams(dimension_semantics=sem, vmem_limit_bytes=VMEM_LIMIT)


def _tile(n, t):
    t = min(n, t)
    assert n % t == 0, (n, t)
    return t


def _norm_mod(x, g, sh, sc):
    ms = jnp.mean(x * x, axis=-1, keepdims=True)
    return (x * lax.rsqrt(ms + NORM_EPS) * g) * (1.0 + sc) + sh


def _adaln_kernel(s_ref, w_ref, b_ref, o_ref):
    s = s_ref[...]
    s = s * jax.nn.sigmoid(s)
    o_ref[0] = jnp.dot(s, w_ref[0], preferred_element_type=F32) + b_ref[0]


def adaln(s8, ada_w, ada_b):
    depth, d, n6 = ada_w.shape
    tn = _tile(n6, 1024)
    return pl.pallas_call(
        _adaln_kernel,
        grid=(depth, n6 // tn),
        in_specs=[pl.BlockSpec((8, d), lambda l, j: (0, 0)),
                  pl.BlockSpec((1, d, tn), lambda l, j: (l, 0, j)),
                  pl.BlockSpec((1, 1, tn), lambda l, j: (l, 0, j))],
        out_specs=pl.BlockSpec((1, 8, tn), lambda l, j: (l, 0, j)),
        out_shape=jax.ShapeDtypeStruct((depth, 8, n6), F32),
        compiler_params=_params("parallel", "parallel"),
        name="adaln",
    )(s8, ada_w, ada_b.reshape(depth, 1, n6))


def _norm_mod_kernel(x_ref, g_ref, sh_ref, sc_ref, o_ref):
    o_ref[...] = _norm_mod(x_ref[...], g_ref[...], sh_ref[...], sc_ref[...]).astype(o_ref.dtype)


def norm_mod(x, g, sh, sc, dtype):
    n, d = x.shape
    tm = _tile(n, 256)
    vec = pl.BlockSpec((1, d), lambda i: (0, 0))
    return pl.pallas_call(
        _norm_mod_kernel,
        grid=(n // tm,),
        in_specs=[pl.BlockSpec((tm, d), lambda i: (i, 0)), vec, vec, vec],
        out_specs=pl.BlockSpec((tm, d), lambda i: (i, 0)),
        out_shape=jax.ShapeDtypeStruct((n, d), dtype),
        compiler_params=_params("parallel"),
        name="norm_mod",
    )(x, g, sh, sc)


def _pool_kernel(cur_ref, prev_ref, next_ref, x_ref, w_ref, ps_ref, g1_ref, o_ref, *, n, tm):
    g = pl.program_id(0)
    i = pl.program_id(1)
    last = pl.num_programs(1) - 1
    cur = cur_ref[...]
    prev = jnp.where(i == 0, 0.0, prev_ref[...])
    nxt = jnp.where(i == last, 0.0, next_ref[...])
    ext = jnp.concatenate([prev, cur, nxt], axis=0)
    t = i * tm + lax.broadcasted_iota(I32, (tm, 1), 0)

    for gi, win in enumerate(POOL_WINDOWS):
        @pl.when(g == gi)
        def _(win=win):
            half = win // 2
            s = ext
            step = 1
            while step < win:
                m = s.shape[0] - step
                s = s[:m] + s[step:step + m]
                step *= 2
            wsum = s[8 - half:8 - half + tm]
            cnt = jnp.minimum(t + half, n) - jnp.maximum(t - half, 0)
            dlt = (wsum / cnt.astype(F32) - cur).astype(BF16)
            y = jnp.dot(dlt, w_ref[0], preferred_element_type=F32) * ps_ref[...]
            o_ref[...] = x_ref[...] + g1_ref[...] * y


def pool_mixer(h, x, w, pscale, g1):
    n, d = x.shape
    ng, dg, _ = w.shape
    tm = _tile(n, 512)
    nb8 = n // 8
    kern = functools.partial(_pool_kernel, n=n, tm=tm)
    col = pl.BlockSpec((1, dg), lambda g, i: (0, g))
    return pl.pallas_call(
        kern,
        grid=(ng, n // tm),
        in_specs=[pl.BlockSpec((tm, dg), lambda g, i: (i, g)),
                  pl.BlockSpec((8, dg), lambda g, i: (jnp.maximum(i * (tm // 8) - 1, 0), g)),
                  pl.BlockSpec((8, dg), lambda g, i: (jnp.minimum((i + 1) * (tm // 8), nb8 - 1), g)),
                  pl.BlockSpec((tm, dg), lambda g, i: (i, g)),
                  pl.BlockSpec((1, dg, dg), lambda g, i: (g, 0, 0)),
                  col, col],
        out_specs=pl.BlockSpec((tm, dg), lambda g, i: (i, g)),
        out_shape=jax.ShapeDtypeStruct((n, d), F32),
        compiler_params=_params("parallel", "parallel"),
        name="pool_mixer",
    )(h, h, h, x, w, pscale, g1)


def _mm_call(kern, a, w, extra, extra_specs, out_shape, out_specs, tm, tn, name, scratch=()):
    m, k = a.shape
    n = w.shape[1]
    return pl.pallas_call(
        kern,
        grid=(pl.cdiv(m, tm), n // tn),
        in_specs=[pl.BlockSpec((tm, k), lambda i, j: (i, 0)),
                  pl.BlockSpec((k, tn), lambda i, j: (0, j))] + list(extra_specs),
        out_specs=out_specs,
        out_shape=out_shape,
        scratch_shapes=list(scratch),
        compiler_params=_params("parallel", "arbitrary"),
        name=name,
    )(a, w, *extra)


def _mm_plain_kernel(a_ref, w_ref, o_ref):
    o_ref[...] = jnp.dot(a_ref[...], w_ref[...], preferred_element_type=F32).astype(o_ref.dtype)


def mm_plain(a, w, dtype=BF16):
    m, _ = a.shape
    n = w.shape[1]
    tm, tn = min(m, 1024), _tile(n, 512)
    return _mm_call(_mm_plain_kernel, a, w, (), (), jax.ShapeDtypeStruct((m, n), dtype),
                    pl.BlockSpec((tm, tn), lambda i, j: (i, j)), tm, tn, "mm_plain")


def _mm_qk_kernel(a_ref, w_ref, g_ref, cos_ref, sin_ref, o_ref, *, scale):
    y = jnp.dot(a_ref[...], w_ref[...], preferred_element_type=F32)
    tn = y.shape[1]
    cos = cos_ref[...]
    sin = sin_ref[...]
    ga_cos, ga_sin = cos * g_ref[0:1], sin * g_ref[0:1]
    gb_cos, gb_sin = cos * g_ref[1:2], sin * g_ref[1:2]
    comp0 = lax.broadcasted_iota(I32, (1, HEAD_DIM), 1) < HEAD_DIM // 2
    for hd in range(tn // (2 * HEAD_DIM)):
        a = y[:, 2 * hd * HEAD_DIM:(2 * hd + 1) * HEAD_DIM]
        b = y[:, (2 * hd + 1) * HEAD_DIM:(2 * hd + 2) * HEAD_DIM]
        sq = a * a + b * b
        ssq0 = jnp.sum(jnp.where(comp0, sq, 0.0), axis=-1, keepdims=True)
        ssq1 = jnp.sum(jnp.where(comp0, 0.0, sq), axis=-1, keepdims=True)
        rstd = jnp.where(comp0, lax.rsqrt(ssq0 * (1.0 / HEAD_DIM) + NORM_EPS),
                         lax.rsqrt(ssq1 * (1.0 / HEAD_DIM) + NORM_EPS)) * scale
        o_ref[:, 2 * hd * HEAD_DIM:(2 * hd + 1) * HEAD_DIM] = ((a * ga_cos - b * gb_sin) * rstd).astype(o_ref.dtype)
        o_ref[:, (2 * hd + 1) * HEAD_DIM:(2 * hd + 2) * HEAD_DIM] = ((b * gb_cos + a * ga_sin) * rstd).astype(o_ref.dtype)


def mm_qk(a, w, gain, cos, sin, scale):
    m, _ = a.shape
    n = w.shape[1]
    tm, tn = min(m, 1024), _tile(n, 512)
    kern = functools.partial(_mm_qk_kernel, scale=scale)
    tab = pl.BlockSpec((tm, HEAD_DIM), lambda i, j: (i, 0))
    return _mm_call(kern, a, w, (gain, cos, sin),
                    (pl.BlockSpec((2, HEAD_DIM), lambda i, j: (0, 0)), tab, tab),
                    jax.ShapeDtypeStruct((m, n), BF16),
                    pl.BlockSpec((tm, tn), lambda i, j: (i, j)), tm, tn, "mm_qk")


def _mm_gelu_kernel(a_ref, w_ref, o_ref, ssq_ref, *, nj_half):
    j = pl.program_id(1)
    y = jax.nn.gelu(jnp.dot(a_ref[...], w_ref[...], preferred_element_type=F32))
    o_ref[...] = y.astype(o_ref.dtype)

    @pl.when(j == nj_half)
    def _():
        ssq_ref[...] = jnp.zeros_like(ssq_ref)

    @pl.when(j >= nj_half)
    def _():
        ssq_ref[...] += jnp.sum(y * y, axis=-1, keepdims=True)


def mm_gelu(a, w):
    m, _ = a.shape
    n = w.shape[1]
    tm, tn = min(m, 1024), _tile(n // 2, 512)
    kern = functools.partial(_mm_gelu_kernel, nj_half=(n // 2) // tn)
    return _mm_call(kern, a, w, (), (),
                    (jax.ShapeDtypeStruct((m, n), BF16), jax.ShapeDtypeStruct((m, LANES), F32)),
                    (pl.BlockSpec((tm, tn), lambda i, j: (i, j)),
                     pl.BlockSpec((tm, LANES), lambda i, j: (i, 0))), tm, tn, "mm_gelu")


def _mm_resid_kernel(a_ref, w_ref, x_ref, g_ref, o_ref):
    y = jnp.dot(a_ref[...], w_ref[...], preferred_element_type=F32)
    o_ref[...] = x_ref[...] + g_ref[...] * y


def mm_resid(a, w, x, g1):
    m, _ = a.shape
    n = w.shape[1]
    tm, tn = min(m, 1024), _tile(n, 512)
    blk = pl.BlockSpec((tm, tn), lambda i, j: (i, j))
    return _mm_call(_mm_resid_kernel, a, w, (x, g1),
                    (blk, pl.BlockSpec((1, tn), lambda i, j: (0, j))),
                    jax.ShapeDtypeStruct((m, n), F32), blk, tm, tn, "mm_resid")


def _gate_mm_kernel(u_ref, v_ref, ssq_ref, vg_ref, ws_ref, bs_ref, w_ref, x_ref, g_ref, o_ref, z_scr, *, width):
    j = pl.program_id(1)
    tm = u_ref.shape[0]

    @pl.when(j == 0)
    def _():
        rstd = lax.rsqrt(ssq_ref[:, :1] * (1.0 / width) + NORM_EPS)

        def body(g, carry):
            col = pl.multiple_of(g * LANES, LANES)
            vg = vg_ref[:, pl.ds(col, LANES)]
            chunks = [slice(c * CHUNK, (c + 1) * CHUNK) for c in range(tm // CHUNK)]
            vn = jnp.concatenate([(v_ref[rows, pl.ds(col, LANES)].astype(F32) * rstd[rows] * vg).astype(BF16)
                                  for rows in chunks], axis=1)
            sv = jnp.dot(ws_ref[g], vn, preferred_element_type=F32)
            bias = bs_ref[:, pl.ds(col, LANES)]
            for c, rows in enumerate(chunks):
                gate = sv[:, c * LANES:(c + 1) * LANES] + bias
                z_scr[rows, pl.ds(col, LANES)] = (u_ref[rows, pl.ds(col, LANES)].astype(F32) * gate).astype(BF16)
            return carry

        lax.fori_loop(0, width // LANES, body, 0, unroll=2)

    y = jnp.dot(z_scr[...], w_ref[...], preferred_element_type=F32)
    o_ref[...] = x_ref[...] + g_ref[...] * y


def gate_mm_resid(uv, ssq, vg, ws, bs_full, w_out, x, g1):
    n, w2 = uv.shape
    width = w2 // 2
    d = w_out.shape[1]
    tm, tn = _tile(n, 512), _tile(d, 512)
    kern = functools.partial(_gate_mm_kernel, width=width)
    blk = pl.BlockSpec((tm, tn), lambda i, j: (i, j))
    return pl.pallas_call(
        kern,
        grid=(n // tm, d // tn),
        in_specs=[pl.BlockSpec((tm, width), lambda i, j: (i, 0)),
                  pl.BlockSpec((tm, width), lambda i, j: (i, 1)),
                  pl.BlockSpec((tm, LANES), lambda i, j: (i, 0)),
                  pl.BlockSpec((1, width), lambda i, j: (0, 0)),
                  pl.BlockSpec(ws.shape, lambda i, j: (0, 0, 0)),
                  pl.BlockSpec((CHUNK, width), lambda i, j: (0, 0)),
                  pl.BlockSpec((width, tn), lambda i, j: (0, j)),
                  blk,
                  pl.BlockSpec((1, tn), lambda i, j: (0, j))],
        out_specs=blk,
        out_shape=jax.ShapeDtypeStruct((n, d), F32),
        scratch_shapes=[pltpu.VMEM((tm, width), BF16)],
        compiler_params=_params("parallel", "arbitrary"),
        name="gate_mm_resid",
    )(uv, uv, ssq, vg, ws, bs_full, w_out, x, g1)


def _attn_kernel(lamv_ref, q_ref, k_ref, vt_ref, sg_ref, o_ref, m_scr, l_scr, acc_scr, p_scr, *, lam_init, tc):
    ki = pl.program_id(2)
    tk = k_ref.shape[0]
    nbuf = p_scr.shape[0]

    @pl.when(ki == 0)
    def _():
        m_scr[...] = jnp.full_like(m_scr, -jnp.inf)
        l_scr[...] = jnp.zeros_like(l_scr)
        acc_scr[...] = jnp.zeros_like(acc_scr)

    vt = vt_ref[...]
    nchunk = q_ref.shape[0] // tc
    lane = lax.broadcasted_iota(I32, (1, 2 * HEAD_DIM), 1) % HEAD_DIM
    kt = k_ref[...]
    scores = {}
    for c in range(2):
        keep = jnp.where((lane < HEAD_DIM // 2) == (c == 0), 1.0, 0.0).astype(BF16)
        qc = q_ref[...] * keep
        for r in range(nchunk):
            scores[r, c] = lax.dot_general(kt, qc[r * tc:(r + 1) * tc], (((1,), (1,)), ((), ())),
                                           preferred_element_type=F32)
    for r in range(nchunk):
        qs = slice(r * tc, (r + 1) * tc)
        for c in range(2):
            buf = (2 * r + c) % nbuf
            s = scores[r, c]
            m_prev = m_scr[c, :, qs]
            m_new = jnp.maximum(m_prev, jnp.max(s, axis=0, keepdims=True))
            alpha = jnp.exp2(m_prev - m_new)
            psum = jnp.zeros((16, tc), F32)
            for g in range(tk // 16):
                pg = jnp.exp2(s[16 * g:16 * (g + 1)] - m_new)
                psum = psum + pg
                p_scr[buf, 16 * g:16 * (g + 1), :] = pg.astype(BF16)
            l_scr[c, :, qs] = alpha * l_scr[c, :, qs] + jnp.sum(psum, axis=0, keepdims=True)
            acc_scr[c, :, qs] = alpha * acc_scr[c, :, qs] + jnp.dot(vt, p_scr[buf], preferred_element_type=F32)
            m_scr[c, :, qs] = m_new

    @pl.when(ki == pl.num_programs(2) - 1)
    def _():
        lv = lamv_ref[...]
        lam = (jnp.exp(jnp.sum(lv[0:1] * lv[1:2], axis=-1, keepdims=True))
               - jnp.exp(jnp.sum(lv[2:3] * lv[3:4], axis=-1, keepdims=True)) + lam_init)
        ot = acc_scr[0] / l_scr[0] - lam * (acc_scr[1] / l_scr[1])
        ot = ot * lax.rsqrt(jnp.mean(ot * ot, axis=0, keepdims=True) + NORM_EPS)
        o_ref[...] = (ot.T * (sg_ref[...] * (1.0 - lam_init))).astype(o_ref.dtype)


def _key_tile(nk, cap):
    best = LANES
    for t in range(LANES, cap + 1, LANES):
        if nk % t == 0:
            best = t
    return best


def diff_attention(q, k, vt, lamv, sub_g, lam_init):
    n, d = q.shape
    nk = k.shape[0]
    hw = 2 * HEAD_DIM
    heads = d // hw
    tq = _tile(n, 1024)
    tk = _key_tile(nk, 1536)
    tc = min(tq, 256)
    kern = functools.partial(_attn_kernel, lam_init=lam_init, tc=tc)
    return pl.pallas_call(
        kern,
        grid=(heads, n // tq, nk // tk),
        in_specs=[pl.BlockSpec((4, HEAD_DIM), lambda h, i, j: (0, 0)),
                  pl.BlockSpec((tq, hw), lambda h, i, j: (i, h)),
                  pl.BlockSpec((tk, hw), lambda h, i, j: (j, h)),
                  pl.BlockSpec((hw, tk), lambda h, i, j: (h, j)),
                  pl.BlockSpec((1, hw), lambda h, i, j: (0, 0))],
        out_specs=pl.BlockSpec((tq, hw), lambda h, i, j: (i, h)),
        out_shape=jax.ShapeDtypeStruct((n, d), BF16),
        scratch_shapes=[pltpu.VMEM((2, 1, tq), F32), pltpu.VMEM((2, 1, tq), F32),
                        pltpu.VMEM((2, hw, tq), F32), pltpu.VMEM((2 * (tq // tc), tk, tc), BF16)],
        compiler_params=_params("parallel", "parallel", "arbitrary"),
        name="diff_attention",
    )(lamv, q, k, vt, sub_g)


def _router_kernel(x_ref, g_ref, sh_ref, sc_ref, wrt_ref, h_ref, aff_ref):
    h = _norm_mod(x_ref[...], g_ref[...], sh_ref[...], sc_ref[...])
    half = h.shape[1] // 2
    bits = pltpu.bitcast(h.astype(BF16).astype(F32), U32)
    h_ref[...] = (bits[:, :half] >> 16) | (bits[:, half:] & jnp.uint32(0xFFFF0000))
    logits = lax.dot_general(wrt_ref[...], h, (((1,), (1,)), ((), ())),
                             preferred_element_type=F32, precision=lax.Precision.HIGHEST)
    ex = jnp.exp(logits - jnp.max(logits, axis=0, keepdims=True))
    aff_ref[...] = ex / jnp.sum(ex, axis=0, keepdims=True)


def router(x, g, sh, sc, w_r_t):
    n, d = x.shape
    e = w_r_t.shape[0]
    tm = _tile(n, 256)
    vec = pl.BlockSpec((1, d), lambda i: (0, 0))
    return pl.pallas_call(
        _router_kernel,
        grid=(n // tm,),
        in_specs=[pl.BlockSpec((tm, d), lambda i: (i, 0)), vec, vec, vec,
                  pl.BlockSpec((e, d), lambda i: (0, 0))],
        out_specs=(pl.BlockSpec((tm, d // 2), lambda i: (i, 0)), pl.BlockSpec((e, tm), lambda i: (0, i))),
        out_shape=(jax.ShapeDtypeStruct((n, d // 2), U32), jax.ShapeDtypeStruct((e, n), F32)),
        compiler_params=_params("parallel"),
        name="router",
    )(x, g, sh, sc, w_r_t)


def _split3(a):
    hi = a.astype(BF16)
    r1 = a - hi.astype(F32)
    mid = r1.astype(BF16)
    lo = (r1 - mid.astype(F32)).astype(BF16)
    return hi, mid, lo


def _topk_kernel(abt_ref, atb_ref, idx_ref, gate_ref, thr_scr, cut_scr, *, cap, cap_pad):
    ne, nb, _ = abt_ref.shape
    ntok = nb * LANES
    bits = pltpu.bitcast(abt_ref[...], I32)
    tok_bt = (lax.broadcasted_iota(I32, (1, nb, LANES), 1) * LANES
              + lax.broadcasted_iota(I32, (1, nb, LANES), 2))

    def count(ones):
        c = jnp.sum(ones, axis=2, keepdims=True)
        return jnp.sum(c, axis=1, keepdims=True)

    def thr_body(_, carry):
        lo, hi = carry
        mid = lo + ((hi - lo + 1) >> 1)
        ok = count(jnp.where(bits >= mid, 1, 0)) >= cap
        return jnp.where(ok, mid, lo), jnp.where(ok, hi, mid - 1)

    lo0 = jnp.zeros((ne, 1, 1), I32)
    hi0 = jnp.full((ne, 1, 1), 0x7F800000, I32)
    thr, _ = lax.fori_loop(0, 31, thr_body, (lo0, hi0))

    tok_eq = jnp.where(bits == thr, tok_bt, ntok)
    need = cap - count(jnp.where(bits > thr, 1, 0))

    def cut_body(_, carry):
        lo, hi = carry
        mid = (lo + hi) >> 1
        ok = count(jnp.where(tok_eq < mid, 1, 0)) >= need
        return jnp.where(ok, lo, mid), jnp.where(ok, mid, hi)

    _, cut = lax.fori_loop(0, int(math.log2(ntok)) + 1, cut_body,
                           (jnp.zeros((ne, 1, 1), I32), jnp.full((ne, 1, 1), ntok, I32)))
    thr_scr[...] = jnp.broadcast_to(thr, thr_scr.shape)
    cut_scr[...] = jnp.broadcast_to(cut, cut_scr.shape)

    r_i = lax.broadcasted_iota(I32, (LANES, LANES), 0)
    c_i = lax.broadcasted_iota(I32, (LANES, LANES), 1)
    tri_t = jnp.where(c_i <= r_i, 1.0, 0.0).astype(BF16)
    rb_i = lax.broadcasted_iota(I32, (nb, nb), 0)
    cb_i = lax.broadcasted_iota(I32, (nb, nb), 1)
    tri_b = jnp.where(cb_i <= rb_i, 1.0, 0.0).astype(BF16)
    tok_b = lax.broadcasted_iota(I32, (nb, LANES), 0) * LANES + lax.broadcasted_iota(I32, (nb, LANES), 1)
    tok_t = lax.broadcasted_iota(I32, (LANES, nb), 1) * LANES + lax.broadcasted_iota(I32, (LANES, nb), 0)
    slot = lax.broadcasted_iota(I32, (1, cap_pad), 1).astype(F32)
    blk_iota = lax.broadcasted_iota(I32, (nb, 1), 0).astype(F32)
    lane_iota = lax.broadcasted_iota(I32, (LANES, 1), 0).astype(F32)

    def select(b, tok, th, ct):
        return jnp.where(b > th, 1.0, jnp.where(b == th, jnp.where(tok < ct, 1.0, 0.0), 0.0))

    def per_expert(e, carry):
        th = thr_scr[e][:1, :1]
        ct = cut_scr[e][:1, :1]
        a_tb = atb_ref[e]
        sel_bt = select(pltpu.bitcast(abt_ref[e], I32), tok_b, th, ct)
        sel_tb = select(pltpu.bitcast(a_tb, I32), tok_t, th, ct)
        cw_t = jnp.dot(tri_t, sel_tb.astype(BF16), preferred_element_type=F32)
        tot = jnp.sum(sel_bt, axis=1, keepdims=True)
        incl = jnp.dot(tri_b, jnp.broadcast_to(tot, (nb, LANES)).astype(BF16),
                       preferred_element_type=F32)[:, :1]
        excl = incl - tot
        blk = jnp.sum(jnp.where(incl <= slot, 1.0, 0.0), axis=0, keepdims=True)
        onehot = jnp.where(blk_iota == blk, 1.0, 0.0)
        oh16 = onehot.astype(BF16)
        g_t = jnp.dot(cw_t.astype(BF16), oh16, preferred_element_type=F32)
        base = jnp.sum(onehot * excl, axis=0, keepdims=True)
        tl = jnp.sum(jnp.where(g_t + base <= slot, 1.0, 0.0), axis=0, keepdims=True)
        idx_ref[pl.ds(e, 1), :] = (blk * LANES + tl).astype(I32)
        hi, mid, lo = _split3(a_tb)
        rows = (jnp.dot(hi, oh16, preferred_element_type=F32)
                + jnp.dot(mid, oh16, preferred_element_type=F32)
                + jnp.dot(lo, oh16, preferred_element_type=F32))
        gate_ref[pl.ds(e, 1), :] = jnp.sum(jnp.where(lane_iota == tl, rows, 0.0), axis=0, keepdims=True)
        return carry

    lax.fori_loop(0, ne, per_expert, 0)


def topk_select(aff_t, cap):
    ne, n = aff_t.shape
    nb = max(LANES, -(-n // LANES))
    nb = -(-nb // LANES) * LANES
    cap_pad = max(LANES, cap)
    padded = jnp.pad(aff_t, ((0, 0), (0, nb * LANES - n)), constant_values=-1.0)
    a_bt = padded.reshape(ne, nb, LANES)
    a_tb = jnp.swapaxes(a_bt, 1, 2)
    kern = functools.partial(_topk_kernel, cap=cap, cap_pad=cap_pad)
    idx, gate = pl.pallas_call(
        kern,
        out_shape=(jax.ShapeDtypeStruct((ne, cap_pad), I32), jax.ShapeDtypeStruct((ne, cap_pad), F32)),
        scratch_shapes=[pltpu.VMEM((ne, 8, LANES), I32), pltpu.VMEM((ne, 8, LANES), I32)],
        compiler_params=pltpu.CompilerParams(vmem_limit_bytes=VMEM_LIMIT),
        name="topk_select",
    )(a_bt, a_tb)
    return idx[:, :cap], gate[:, :cap]


def _moe_kernel(idx_ref, gate_ref, wgu_ref, wd_ref, g2_ref, h_hbm, xin_hbm, out_hbm,
                xs, ab, y_scr, sem_x, sem_a, sem_o, *, tc):
    del xin_hbm
    nct = pl.num_programs(1)
    ct = pl.program_id(1)
    s = pl.program_id(0) * nct + ct
    last = pl.num_programs(0) * nct - 1
    slot = s % 2
    base = s * tc
    nxt = jnp.minimum(s + 1, last) * tc

    def x_copy(b, r, sl):
        return pltpu.make_async_copy(h_hbm.at[pl.ds(idx_ref[b + r], 1)], xs.at[sl, pl.ds(r, 1)], sem_x.at[sl])

    def a_copy(r):
        return pltpu.make_async_copy(out_hbm.at[pl.ds(idx_ref[base + r], 1)], ab.at[slot, pl.ds(r, 1)],
                                     sem_a.at[slot])

    def o_copy(b, r, sl):
        return pltpu.make_async_copy(ab.at[sl, pl.ds(r, 1)], out_hbm.at[pl.ds(idx_ref[b + r], 1)], sem_o.at[sl])

    def rows(fn):
        def body(r, c):
            fn(r)
            return c
        lax.fori_loop(0, tc, body, 0, unroll=8)

    @pl.when(s == 0)
    def _():
        rows(lambda r: x_copy(0, r, 0).start())

    rows(lambda r: x_copy(base, r, slot).wait())

    @pl.when(jnp.logical_and(ct == 0, s > 0))
    def _():
        rows(lambda r: o_copy(base - tc, r, 1 - slot).wait())

    for r in range(tc):
        x_copy(nxt, r, 1 - slot).start()
        a_copy(r).start()
    xu = xs[slot]
    x_lo = pltpu.bitcast(xu << 16, F32).astype(BF16)
    x_hi = pltpu.bitcast(xu & jnp.uint32(0xFFFF0000), F32).astype(BF16)
    half = xu.shape[1]
    ab2 = (jnp.dot(x_lo, wgu_ref[0, :half], preferred_element_type=F32)
           + jnp.dot(x_hi, wgu_ref[0, half:], preferred_element_type=F32))
    f = ab2.shape[1] // 2
    a, b = ab2[:, :f], ab2[:, f:]
    gate = jnp.concatenate([gate_ref[0]] * (f // LANES), axis=1)
    hm = (a * jax.nn.sigmoid(a) * b * gate).astype(BF16)
    y_scr[...] = jnp.dot(hm, wd_ref[0], preferred_element_type=F32) * g2_ref[...]

    rows(lambda r: a_copy(r).wait())

    @pl.when(ct > 0)
    def _():
        rows(lambda r: o_copy(base - tc, r, 1 - slot).wait())

    ab[slot] = ab[slot] + y_scr[...]
    rows(lambda r: o_copy(base, r, slot).start())

    @pl.when(s == last)
    def _():
        rows(lambda r: o_copy(base, r, slot).wait())
        rows(lambda r: x_copy(nxt, r, 1 - slot).wait())


def moe_apply(h, x, idx, gate, wgu, wd, g2):
    n, d = x.shape
    ne, cap = idx.shape
    f = wd.shape[1]
    assert f % LANES == 0
    tc = _tile(cap, 256)
    gate_b = jnp.broadcast_to(gate[:, :, None], (ne, cap, LANES))
    kern = functools.partial(_moe_kernel, tc=tc)
    grid_spec = pltpu.PrefetchScalarGridSpec(
        num_scalar_prefetch=1,
        grid=(ne, cap // tc),
        in_specs=[pl.BlockSpec((1, tc, LANES), lambda e, c, idx: (e, c, 0)),
                  pl.BlockSpec((1, d, 2 * f), lambda e, c, idx: (e, 0, 0)),
                  pl.BlockSpec((1, f, d), lambda e, c, idx: (e, 0, 0)),
                  pl.BlockSpec((1, d), lambda e, c, idx: (0, 0)),
                  pl.BlockSpec(memory_space=pl.ANY),
                  pl.BlockSpec(memory_space=pl.ANY)],
        out_specs=pl.BlockSpec(memory_space=pl.ANY),
        scratch_shapes=[pltpu.VMEM((2, tc, d // 2), U32), pltpu.VMEM((2, tc, d), F32), pltpu.VMEM((tc, d), F32),
                        pltpu.SemaphoreType.DMA((2,)), pltpu.SemaphoreType.DMA((2,)),
                        pltpu.SemaphoreType.DMA((2,))],
    )
    return pl.pallas_call(
        kern,
        grid_spec=grid_spec,
        out_shape=jax.ShapeDtypeStruct((n, d), F32),
        input_output_aliases={6: 0},
        compiler_params=_params("arbitrary", "arbitrary"),
        name="moe_apply",
    )(idx.reshape(-1), gate_b, wgu, wd, g2, h, x)


def moe_layer(x, g, sh, sc, g2, w_r, wgu, wd):
    n = x.shape[0]
    ne = w_r.shape[1]
    h, aff_t = router(x, g, sh, sc, w_r.T)
    idx, gate = topk_select(aff_t, EC_CAPACITY_FACTOR * n // ne)
    return moe_apply(h, x, idx, gate, wgu, wd, g2)


def _rope_perm(width):
    quarter = HEAD_DIM // 4
    starts = jnp.array([0, 2, 4, 6, 1, 3, 5, 7]) * quarter
    blk = (starts[:, None] + jnp.arange(quarter)[None, :]).reshape(-1)
    return (jnp.arange(0, width, 2 * HEAD_DIM)[:, None] + blk[None, :]).reshape(-1)


def _rope_tables(n, n_ctx):
    quarter = HEAD_DIM // 4
    t = jnp.arange(n, dtype=I32)
    freqs = ROPE_THETA ** (-jnp.arange(quarter, dtype=F32) / quarter)
    ang_r = (t // GRID_W).astype(F32)[:, None] * freqs
    ang_c = (t % GRID_W).astype(F32)[:, None] * freqs
    cos = jnp.concatenate([jnp.cos(ang_r), jnp.cos(ang_c)] * 2, axis=1)
    sin = jnp.concatenate([jnp.sin(ang_r), jnp.sin(ang_c)] * 2, axis=1)
    cos = jnp.concatenate([jnp.ones((n_ctx, HEAD_DIM), F32), cos], axis=0)
    sin = jnp.concatenate([jnp.zeros((n_ctx, HEAD_DIM), F32), sin], axis=0)
    return cos, sin


def kernel(x, c, ctx, c_ctx, ada_w, ada_b, norm1_g, norm2_g, router_w, exp_w_gate, exp_w_up, exp_w_down,
           pool_w, pool_scale, gm_w_in, gm_v_g, gm_w_s, gm_b_s, gm_w_out, da_w_q, da_w_k, da_w_v, da_w_o,
           da_q_g, da_k_g, da_lam_q1, da_lam_k1, da_lam_q2, da_lam_k2, da_sub_g):
    bsz, n, d = x.shape
    assert bsz == 1 and c.shape[0] == 1
    n_ctx = ctx.shape[1]
    depth = ada_w.shape[0]
    xs, cs = x[0], ctx[0]

    s8 = jnp.concatenate([c, c_ctx[None], jnp.zeros((6, d), F32)], axis=0)
    mod = adaln(s8, ada_w, ada_b)

    def mods(i, row):
        return [mod[i, row:row + 1, k * d:(k + 1) * d] for k in range(6)]

    def row(v):
        return v.reshape(1, -1)

    for i in range(depth):
        kind, slot = i % N_MIXERS, i // N_MIXERS
        keep_ctx = any(j % N_MIXERS == 2 for j in range(i + 1, depth))
        sh1, sc1, g1, sh2, sc2, g2 = mods(i, 0)
        csh1, csc1, cg1, csh2, csc2, cg2 = mods(i, 1)
        n1, n2 = row(norm1_g[i]), row(norm2_g[i])
        streams = [(xs, sh1, sc1, g1)] + ([(cs, csh1, csc1, cg1)] if keep_ctx else [])

        if kind == 0:
            w = pool_w[slot].astype(BF16)
            outs = [pool_mixer(norm_mod(s, n1, sh, sc, F32), s, w, row(pool_scale[slot]), g)
                    for s, sh, sc, g in streams]
        elif kind == 1:
            w_in = gm_w_in[slot].astype(BF16)
            w_out = gm_w_out[slot].astype(BF16)
            w_s = gm_w_s[slot].astype(BF16)
            width = w_out.shape[0]
            bs_full = jnp.repeat(gm_b_s[slot].T, width // gm_w_s.shape[1], axis=1)
            outs = []
            for s, sh, sc, g in streams:
                uv, ssq = mm_gelu(norm_mod(s, n1, sh, sc, BF16), w_in)
                outs.append(gate_mm_resid(uv, ssq, row(gm_v_g[slot]), w_s, bs_full, w_out, s, g))
        else:
            assert not keep_ctx
            lam_init = 0.8 - 0.6 * math.exp(-0.3 * i)
            h_all = jnp.concatenate([norm_mod(cs, n1, csh1, csc1, BF16), norm_mod(xs, n1, sh1, sc1, BF16)], axis=0)
            cos, sin = _rope_tables(n, n_ctx)
            perm = _rope_perm(d)
            gperm = (perm[:2 * HEAD_DIM] % HEAD_DIM).reshape(2, HEAD_DIM)
            q = mm_qk(h_all[n_ctx:], da_w_q[slot][:, perm].astype(BF16), da_q_g[slot][gperm],
                      cos[n_ctx:], sin[n_ctx:], HEAD_DIM ** -0.5 * math.log2(math.e))
            k = mm_qk(h_all, da_w_k[slot][:, perm].astype(BF16), da_k_g[slot][gperm], cos, sin, 1.0)
            v = mm_plain(h_all, da_w_v[slot].astype(BF16))
            lamv = jnp.stack([da_lam_q1[slot], da_lam_k1[slot], da_lam_q2[slot], da_lam_k2[slot]])
            o = diff_attention(q, k, v.T, lamv, row(da_sub_g[slot]), lam_init)
            outs = [mm_resid(o, da_w_o[slot].astype(BF16), xs, g1)]

        wgu = jnp.concatenate([exp_w_gate[i], exp_w_up[i]], axis=-1).astype(BF16)
        wd = exp_w_down[i].astype(BF16)
        xs = moe_layer(outs[0], n2, sh2, sc2, g2, router_w[i], wgu, wd)
        if keep_ctx:
            cs = moe_layer(outs[1], n2, csh2, csc2, cg2, router_w[i], wgu, wd)
    return xs[None]
```

```python
import functools
import math

import jax
import jax.numpy as jnp
from jax import lax
from jax.experimental import pallas as pl
from jax.experimental.pallas import tpu as pltpu

F32 = jnp.float32
BF16 = jnp.bfloat16
I32 = jnp.int32
U32 = jnp.uint32

NORM_EPS = 1e-6
LANES = 128
GRID_W = 64
CHUNK = 128
POOL_WINDOWS = (2, 4, 8, 16)
HEAD_DIM = 128
ROPE_THETA = 10000.0
EC_CAPACITY_FACTOR = 2
N_MIXERS = 3
VMEM_LIMIT = 56 * 1024 * 1024
BOUND_SLACK = 1.01
MAX_FIXED_OFFSET = 60.0


def _params(*sem):
    return pltpu.CompilerParams(dimension_semantics=sem, vmem_limit_bytes=VMEM_LIMIT)


def _tile(n, t):
    t = min(n, t)
    assert n % t == 0, (n, t)
    return t


def _norm_mod(x, g, sh, sc):
    ms = jnp.mean(x * x, axis=-1, keepdims=True)
    return (x * lax.rsqrt(ms + NORM_EPS) * g) * (1.0 + sc) + sh


def _adaln_kernel(s_ref, w_ref, b_ref, o_ref):
    s = s_ref[...]
    s = s * jax.nn.sigmoid(s)
    o_ref[0] = jnp.dot(s, w_ref[0], preferred_element_type=F32) + b_ref[0]


def adaln(s8, ada_w, ada_b):
    depth, d, n6 = ada_w.shape
    tn = _tile(n6, 1024)
    return pl.pallas_call(
        _adaln_kernel,
        grid=(depth, n6 // tn),
        in_specs=[pl.BlockSpec((8, d), lambda l, j: (0, 0)),
                  pl.BlockSpec((1, d, tn), lambda l, j: (l, 0, j)),
                  pl.BlockSpec((1, 1, tn), lambda l, j: (l, 0, j))],
        out_specs=pl.BlockSpec((1, 8, tn), lambda l, j: (l, 0, j)),
        out_shape=jax.ShapeDtypeStruct((depth, 8, n6), F32),
        compiler_params=_params("parallel", "parallel"),
        name="adaln",
    )(s8, ada_w, ada_b.reshape(depth, 1, n6))


def _norm_mod_kernel(x_ref, g_ref, sh_ref, sc_ref, o_ref):
    o_ref[...] = _norm_mod(x_ref[...], g_ref[...], sh_ref[...], sc_ref[...]).astype(o_ref.dtype)


def norm_mod(x, g, sh, sc, dtype):
    n, d = x.shape
    tm = _tile(n, 256)
    vec = pl.BlockSpec((1, d), lambda i: (0, 0))
    return pl.pallas_call(
        _norm_mod_kernel,
        grid=(n // tm,),
        in_specs=[pl.BlockSpec((tm, d), lambda i: (i, 0)), vec, vec, vec],
        out_specs=pl.BlockSpec((tm, d), lambda i: (i, 0)),
        out_shape=jax.ShapeDtypeStruct((n, d), dtype),
        compiler_params=_params("parallel"),
        name="norm_mod",
    )(x, g, sh, sc)


def _norm_mod_pair_kernel(x_ref, c_ref, g_ref, sh_ref, sc_ref, o_ref, *, nx):
    is_ctx = pl.program_id(0) >= nx
    src = jnp.where(is_ctx, c_ref[...], x_ref[...])
    sh = jnp.where(is_ctx, sh_ref[1:2], sh_ref[0:1])
    sc = jnp.where(is_ctx, sc_ref[1:2], sc_ref[0:1])
    o_ref[...] = _norm_mod(src, g_ref[...], sh, sc).astype(o_ref.dtype)


def norm_mod_pair(x, ctx, g, sh2, sc2, dtype):
    n, d = x.shape
    nc = ctx.shape[0]
    tm = _tile(nc, 256)
    assert n % tm == 0
    nx = n // tm
    vec = pl.BlockSpec((1, d), lambda i: (0, 0))
    vec2 = pl.BlockSpec((2, d), lambda i: (0, 0))
    return pl.pallas_call(
        functools.partial(_norm_mod_pair_kernel, nx=nx),
        grid=(nx + nc // tm,),
        in_specs=[pl.BlockSpec((tm, d), lambda i: (jnp.minimum(i, nx - 1), 0)),
                  pl.BlockSpec((tm, d), lambda i: (jnp.maximum(i - nx, 0), 0)), vec, vec2, vec2],
        out_specs=pl.BlockSpec((tm, d), lambda i: (i, 0)),
        out_shape=jax.ShapeDtypeStruct((n + nc, d), dtype),
        compiler_params=_params("parallel"),
        name="norm_mod_pair",
    )(x, ctx, g, sh2, sc2)


def _pool_kernel(cur_ref, prev_ref, next_ref, x_ref, w_ref, ps_ref, g1_ref, o_ref, *, n, tm):
    g = pl.program_id(0)
    i = pl.program_id(1)
    last = pl.num_programs(1) - 1
    cur = cur_ref[...]
    prev = jnp.where(i == 0, 0.0, prev_ref[...])
    nxt = jnp.where(i == last, 0.0, next_ref[...])
    ext = jnp.concatenate([prev, cur, nxt], axis=0)
    t = i * tm + lax.broadcasted_iota(I32, (tm, 1), 0)

    for gi, win in enumerate(POOL_WINDOWS):
        @pl.when(g == gi)
        def _(win=win):
            half = win // 2
            s = ext
            step = 1
            while step < win:
                m = s.shape[0] - step
                s = s[:m] + s[step:step + m]
                step *= 2
            wsum = s[8 - half:8 - half + tm]
            cnt = jnp.minimum(t + half, n) - jnp.maximum(t - half, 0)
            dlt = (wsum / cnt.astype(F32) - cur).astype(BF16)
            y = jnp.dot(dlt, w_ref[0], preferred_element_type=F32) * ps_ref[...]
            o_ref[...] = x_ref[...] + g1_ref[...] * y


def pool_mixer(h, x, w, pscale, g1):
    n, d = x.shape
    ng, dg, _ = w.shape
    tm = _tile(n, 512)
    nb8 = n // 8
    kern = functools.partial(_pool_kernel, n=n, tm=tm)
    col = pl.BlockSpec((1, dg), lambda g, i: (0, g))
    return pl.pallas_call(
        kern,
        grid=(ng, n // tm),
        in_specs=[pl.BlockSpec((tm, dg), lambda g, i: (i, g)),
                  pl.BlockSpec((8, dg), lambda g, i: (jnp.maximum(i * (tm // 8) - 1, 0), g)),
                  pl.BlockSpec((8, dg), lambda g, i: (jnp.minimum((i + 1) * (tm // 8), nb8 - 1), g)),
                  pl.BlockSpec((tm, dg), lambda g, i: (i, g)),
                  pl.BlockSpec((1, dg, dg), lambda g, i: (g, 0, 0)),
                  col, col],
        out_specs=pl.BlockSpec((tm, dg), lambda g, i: (i, g)),
        out_shape=jax.ShapeDtypeStruct((n, d), F32),
        compiler_params=_params("parallel", "parallel"),
        name="pool_mixer",
    )(h, h, h, x, w, pscale, g1)


def _mm_call(kern, a, w, extra, extra_specs, out_shape, out_specs, tm, tn, name, scratch=(), m=None):
    k = a.shape[1]
    m = a.shape[0] if m is None else m
    n = w.shape[1]
    return pl.pallas_call(
        kern,
        grid=(pl.cdiv(m, tm), n // tn),
        in_specs=[pl.BlockSpec((tm, k), lambda i, j: (i, 0)),
                  pl.BlockSpec((k, tn), lambda i, j: (0, j))] + list(extra_specs),
        out_specs=out_specs,
        out_shape=out_shape,
        scratch_shapes=list(scratch),
        compiler_params=_params("parallel", "arbitrary"),
        name=name,
    )(a, w, *extra)


def _mm_t_kernel(a_ref, wt_ref, o_ref):
    o_ref[...] = lax.dot_general(wt_ref[...], a_ref[...], (((1,), (1,)), ((), ())),
                                 preferred_element_type=F32).astype(o_ref.dtype)


def mm_transposed(a, wt, dtype=BF16):
    m, k = a.shape
    n = wt.shape[0]
    tm, tn = min(m, 1024), _tile(n, 512)
    return pl.pallas_call(
        _mm_t_kernel,
        grid=(pl.cdiv(m, tm), n // tn),
        in_specs=[pl.BlockSpec((tm, k), lambda i, j: (i, 0)),
                  pl.BlockSpec((tn, k), lambda i, j: (j, 0))],
        out_specs=pl.BlockSpec((tn, tm), lambda i, j: (j, i)),
        out_shape=jax.ShapeDtypeStruct((n, m), dtype),
        compiler_params=_params("parallel", "arbitrary"),
        name="mm_transposed",
    )(a, wt)


def _mm_qk_kernel(a_ref, w_ref, g_ref, cos_ref, sin_ref, o_ref, *, scale):
    y = jnp.dot(a_ref[...], w_ref[...], preferred_element_type=F32)
    tn = y.shape[1]
    cos = cos_ref[...]
    sin = sin_ref[...]
    ga_cos, ga_sin = cos * g_ref[0:1], sin * g_ref[0:1]
    gb_cos, gb_sin = cos * g_ref[1:2], sin * g_ref[1:2]
    comp0 = lax.broadcasted_iota(I32, (1, HEAD_DIM), 1) < HEAD_DIM // 2
    for hd in range(tn // (2 * HEAD_DIM)):
        a = y[:, 2 * hd * HEAD_DIM:(2 * hd + 1) * HEAD_DIM]
        b = y[:, (2 * hd + 1) * HEAD_DIM:(2 * hd + 2) * HEAD_DIM]
        sq = a * a + b * b
        ssq0 = jnp.sum(jnp.where(comp0, sq, 0.0), axis=-1, keepdims=True)
        ssq1 = jnp.sum(jnp.where(comp0, 0.0, sq), axis=-1, keepdims=True)
        rstd = jnp.where(comp0, lax.rsqrt(ssq0 * (1.0 / HEAD_DIM) + NORM_EPS),
                         lax.rsqrt(ssq1 * (1.0 / HEAD_DIM) + NORM_EPS)) * scale
        o_ref[:, 2 * hd * HEAD_DIM:(2 * hd + 1) * HEAD_DIM] = ((a * ga_cos - b * gb_sin) * rstd).astype(o_ref.dtype)
        o_ref[:, (2 * hd + 1) * HEAD_DIM:(2 * hd + 2) * HEAD_DIM] = ((b * gb_cos + a * ga_sin) * rstd).astype(o_ref.dtype)


def mm_qk(a, w, gain, cos, sin, scale, m=None):
    m = a.shape[0] if m is None else m
    n = w.shape[1]
    tm, tn = min(m, 1024), _tile(n, 512)
    kern = functools.partial(_mm_qk_kernel, scale=scale)
    tab = pl.BlockSpec((tm, HEAD_DIM), lambda i, j: (i, 0))
    return _mm_call(kern, a, w, (gain, cos, sin),
                    (pl.BlockSpec((2, HEAD_DIM), lambda i, j: (0, 0)), tab, tab),
                    jax.ShapeDtypeStruct((m, n), BF16),
                    pl.BlockSpec((tm, tn), lambda i, j: (i, j)), tm, tn, "mm_qk", m=m)


def _mm_gelu_kernel(a_ref, w_ref, o_ref, ssq_ref, *, nj_half):
    j = pl.program_id(1)
    y = jax.nn.gelu(jnp.dot(a_ref[...], w_ref[...], preferred_element_type=F32))
    o_ref[...] = y.astype(o_ref.dtype)

    @pl.when(j == nj_half)
    def _():
        ssq_ref[...] = jnp.zeros_like(ssq_ref)

    @pl.when(j >= nj_half)
    def _():
        ssq_ref[...] += jnp.sum(y * y, axis=-1, keepdims=True)


def mm_gelu(a, w):
    m, _ = a.shape
    n = w.shape[1]
    tm, tn = min(m, 1024), _tile(n // 2, 512)
    kern = functools.partial(_mm_gelu_kernel, nj_half=(n // 2) // tn)
    return _mm_call(kern, a, w, (), (),
                    (jax.ShapeDtypeStruct((m, n), BF16), jax.ShapeDtypeStruct((m, LANES), F32)),
                    (pl.BlockSpec((tm, tn), lambda i, j: (i, j)),
                     pl.BlockSpec((tm, LANES), lambda i, j: (i, 0))), tm, tn, "mm_gelu")


def _mm_resid_kernel(a_ref, w_ref, x_ref, g_ref, o_ref):
    y = jnp.dot(a_ref[...], w_ref[...], preferred_element_type=F32)
    o_ref[...] = x_ref[...] + g_ref[...] * y


def mm_resid(a, w, x, g1):
    m, _ = a.shape
    n = w.shape[1]
    tm, tn = min(m, 1024), _tile(n, 512)
    blk = pl.BlockSpec((tm, tn), lambda i, j: (i, j))
    return _mm_call(_mm_resid_kernel, a, w, (x, g1),
                    (blk, pl.BlockSpec((1, tn), lambda i, j: (0, j))),
                    jax.ShapeDtypeStruct((m, n), F32), blk, tm, tn, "mm_resid")


def _gate_mm_kernel(u_ref, v_ref, ssq_ref, vg_ref, ws_ref, bs_ref, w_ref, x_ref, g_ref, o_ref, z_scr, *, width):
    j = pl.program_id(1)
    tm = u_ref.shape[0]

    @pl.when(j == 0)
    def _():
        rstd = lax.rsqrt(ssq_ref[:, :1] * (1.0 / width) + NORM_EPS)

        def body(g, carry):
            col = pl.multiple_of(g * LANES, LANES)
            vg = vg_ref[:, pl.ds(col, LANES)]
            chunks = [slice(c * CHUNK, (c + 1) * CHUNK) for c in range(tm // CHUNK)]
            vn = jnp.concatenate([(v_ref[rows, pl.ds(col, LANES)].astype(F32) * rstd[rows] * vg).astype(BF16)
                                  for rows in chunks], axis=1)
            sv = jnp.dot(ws_ref[g], vn, preferred_element_type=F32)
            bias = bs_ref[:, pl.ds(col, LANES)]
            for c, rows in enumerate(chunks):
                gate = sv[:, c * LANES:(c + 1) * LANES] + bias
                z_scr[rows, pl.ds(col, LANES)] = (u_ref[rows, pl.ds(col, LANES)].astype(F32) * gate).astype(BF16)
            return carry

        lax.fori_loop(0, width // LANES, body, 0, unroll=2)

    y = jnp.dot(z_scr[...], w_ref[...], preferred_element_type=F32)
    o_ref[...] = x_ref[...] + g_ref[...] * y


def gate_mm_resid(uv, ssq, vg, ws, bs_full, w_out, x, g1):
    n, w2 = uv.shape
    width = w2 // 2
    d = w_out.shape[1]
    tm, tn = _tile(n, 512), _tile(d, 512)
    kern = functools.partial(_gate_mm_kernel, width=width)
    blk = pl.BlockSpec((tm, tn), lambda i, j: (i, j))
    return pl.pallas_call(
        kern,
        grid=(n // tm, d // tn),
        in_specs=[pl.BlockSpec((tm, width), lambda i, j: (i, 0)),
                  pl.BlockSpec((tm, width), lambda i, j: (i, 1)),
                  pl.BlockSpec((tm, LANES), lambda i, j: (i, 0)),
                  pl.BlockSpec((1, width), lambda i, j: (0, 0)),
                  pl.BlockSpec(ws.shape, lambda i, j: (0, 0, 0)),
                  pl.BlockSpec((CHUNK, width), lambda i, j: (0, 0)),
                  pl.BlockSpec((width, tn), lambda i, j: (0, j)),
                  blk,
                  pl.BlockSpec((1, tn), lambda i, j: (0, j))],
        out_specs=blk,
        out_shape=jax.ShapeDtypeStruct((n, d), F32),
        scratch_shapes=[pltpu.VMEM((tm, width), BF16)],
        compiler_params=_params("parallel", "arbitrary"),
        name="gate_mm_resid",
    )(uv, uv, ssq, vg, ws, bs_full, w_out, x, g1)


def _attn_kernel(lamv_ref, kmax_ref, q_ref, k_ref, vt_ref, sg_ref, o_ref, m_scr, l_scr, acc_scr, p_scr,
                 *, lam_init, tc, bounded):
    ki = pl.program_id(2)
    tk = k_ref.shape[0]
    nbuf = p_scr.shape[0]
    nchunk = q_ref.shape[0] // tc
    lane = lax.broadcasted_iota(I32, (1, 2 * HEAD_DIM), 1) % HEAD_DIM
    qcs = [q_ref[...] * jnp.where((lane < HEAD_DIM // 2) == (c == 0), 1.0, 0.0).astype(BF16) for c in range(2)]

    @pl.when(ki == 0)
    def _():
        l_scr[...] = jnp.zeros_like(l_scr)
        acc_scr[...] = jnp.zeros_like(acc_scr)
        if bounded:
            ones = jnp.ones((8, 2 * HEAD_DIM), BF16)
            for c in range(2):
                n2 = lax.dot_general(ones, qcs[c] * qcs[c], (((1,), (1,)), ((), ())),
                                     preferred_element_type=F32)
                m_scr[c] = jnp.sqrt(n2[:1]) * (BOUND_SLACK * kmax_ref[0, c:c + 1, :1])
        else:
            m_scr[...] = jnp.full_like(m_scr, -jnp.inf)

    vt = vt_ref[...]
    kt = k_ref[...]
    scores = {}
    for c in range(2):
        for r in range(nchunk):
            scores[r, c] = lax.dot_general(kt, qcs[c][r * tc:(r + 1) * tc], (((1,), (1,)), ((), ())),
                                           preferred_element_type=F32)
    for r in range(nchunk):
        qs = slice(r * tc, (r + 1) * tc)
        for c in range(2):
            buf = (2 * r + c) % nbuf
            s = scores[r, c]
            m_prev = m_scr[c, :, qs]
            m_new = m_prev if bounded else jnp.maximum(m_prev, jnp.max(s, axis=0, keepdims=True))
            psum = jnp.zeros((16, tc), F32)
            for g in range(tk // 16):
                pg = jnp.exp2(s[16 * g:16 * (g + 1)] - m_new)
                psum = psum + pg
                p_scr[buf, 16 * g:16 * (g + 1), :] = pg.astype(BF16)
            lsum = jnp.sum(psum, axis=0, keepdims=True)
            pv = jnp.dot(vt, p_scr[buf], preferred_element_type=F32)
            if bounded:
                l_scr[c, :, qs] += lsum
                acc_scr[c, :, qs] += pv
            else:
                alpha = jnp.exp2(m_prev - m_new)
                l_scr[c, :, qs] = alpha * l_scr[c, :, qs] + lsum
                acc_scr[c, :, qs] = alpha * acc_scr[c, :, qs] + pv
                m_scr[c, :, qs] = m_new

    @pl.when(ki == pl.num_programs(2) - 1)
    def _():
        lv = lamv_ref[...]
        lam = (jnp.exp(jnp.sum(lv[0:1] * lv[1:2], axis=-1, keepdims=True))
               - jnp.exp(jnp.sum(lv[2:3] * lv[3:4], axis=-1, keepdims=True)) + lam_init)
        ot = acc_scr[0] / l_scr[0] - lam * (acc_scr[1] / l_scr[1])
        ot = ot * lax.rsqrt(jnp.mean(ot * ot, axis=0, keepdims=True) + NORM_EPS)
        o_ref[...] = (ot.T * (sg_ref[...] * (1.0 - lam_init))).astype(o_ref.dtype)


def _key_tile(nk, cap):
    best = LANES
    for t in range(LANES, cap + 1, LANES):
        if nk % t == 0:
            best = t
    return best


def diff_attention(q, k, vt, lamv, sub_g, lam_init):
    n, d = q.shape
    nk = k.shape[0]
    hw = 2 * HEAD_DIM
    heads = d // hw
    tq = _tile(n, 2048)
    tk = _key_tile(nk, 1536)
    tc = min(tq, 256)

    def comp_norms(z):
        zf = z.astype(F32).reshape(z.shape[0], heads, 2, 2, HEAD_DIM // 2)
        return jnp.sqrt(jnp.sum(zf * zf, axis=(2, 4)))

    kmax = jnp.max(comp_norms(k), axis=0)
    qmax = jnp.max(comp_norms(q), axis=0)
    safe = jnp.all(qmax * kmax * BOUND_SLACK <= MAX_FIXED_OFFSET)
    kmax_b = jnp.broadcast_to(kmax[:, :, None], (heads, 2, LANES))

    def call(bounded):
        kern = functools.partial(_attn_kernel, lam_init=lam_init, tc=tc, bounded=bounded)
        return pl.pallas_call(
            kern,
            grid=(heads, n // tq, nk // tk),
            in_specs=[pl.BlockSpec((4, HEAD_DIM), lambda h, i, j: (0, 0)),
                      pl.BlockSpec((1, 2, LANES), lambda h, i, j: (h, 0, 0)),
                      pl.BlockSpec((tq, hw), lambda h, i, j: (i, h)),
                      pl.BlockSpec((tk, hw), lambda h, i, j: (j, h)),
                      pl.BlockSpec((hw, tk), lambda h, i, j: (h, j)),
                      pl.BlockSpec((1, hw), lambda h, i, j: (0, 0))],
            out_specs=pl.BlockSpec((tq, hw), lambda h, i, j: (i, h)),
            out_shape=jax.ShapeDtypeStruct((n, d), BF16),
            scratch_shapes=[pltpu.VMEM((2, 1, tq), F32), pltpu.VMEM((2, 1, tq), F32),
                            pltpu.VMEM((2, hw, tq), F32), pltpu.VMEM((2 * (tq // tc), tk, tc), BF16)],
            compiler_params=_params("parallel", "parallel", "arbitrary"),
            name="diff_attention_bounded" if bounded else "diff_attention_online",
        )(lamv, kmax_b, q, k, vt, sub_g)

    return lax.cond(safe, lambda: call(True), lambda: call(False))


def _router_kernel(x_ref, g_ref, sh_ref, sc_ref, wrt_ref, h_ref, aff_ref):
    h = _norm_mod(x_ref[...], g_ref[...], sh_ref[...], sc_ref[...])
    half = h.shape[1] // 2
    bits = pltpu.bitcast(h.astype(BF16).astype(F32), U32)
    h_ref[...] = (bits[:, :half] >> 16) | (bits[:, half:] & jnp.uint32(0xFFFF0000))
    logits = lax.dot_general(wrt_ref[...], h, (((1,), (1,)), ((), ())),
                             preferred_element_type=F32, precision=lax.Precision.HIGHEST)
    ex = jnp.exp(logits - jnp.max(logits, axis=0, keepdims=True))
    aff_ref[...] = ex / jnp.sum(ex, axis=0, keepdims=True)


def router(x, g, sh, sc, w_r_t):
    n, d = x.shape
    e = w_r_t.shape[0]
    tm = _tile(n, 256)
    vec = pl.BlockSpec((1, d), lambda i: (0, 0))
    return pl.pallas_call(
        _router_kernel,
        grid=(n // tm,),
        in_specs=[pl.BlockSpec((tm, d), lambda i: (i, 0)), vec, vec, vec,
                  pl.BlockSpec((e, d), lambda i: (0, 0))],
        out_specs=(pl.BlockSpec((tm, d // 2), lambda i: (i, 0)), pl.BlockSpec((e, tm), lambda i: (0, i))),
        out_shape=(jax.ShapeDtypeStruct((n, d // 2), U32), jax.ShapeDtypeStruct((e, n), F32)),
        compiler_params=_params("parallel"),
        name="router",
    )(x, g, sh, sc, w_r_t)


def _split3(a):
    hi = a.astype(BF16)
    r1 = a - hi.astype(F32)
    mid = r1.astype(BF16)
    lo = (r1 - mid.astype(F32)).astype(BF16)
    return hi, mid, lo


def _topk_kernel(abt_ref, atb_ref, idx_ref, gate_ref, thr_scr, cut_scr, *, cap, cap_pad):
    ne, nb, _ = abt_ref.shape
    ntok = nb * LANES
    bits = pltpu.bitcast(abt_ref[...], I32)
    tok_bt = (lax.broadcasted_iota(I32, (1, nb, LANES), 1) * LANES
              + lax.broadcasted_iota(I32, (1, nb, LANES), 2))

    def count(ones):
        c = jnp.sum(ones, axis=2, keepdims=True)
        return jnp.sum(c, axis=1, keepdims=True)

    def thr_body(_, carry):
        lo, hi = carry
        mid = lo + ((hi - lo + 1) >> 1)
        ok = count(jnp.where(bits >= mid, 1, 0)) >= cap
        return jnp.where(ok, mid, lo), jnp.where(ok, hi, mid - 1)

    lo0 = jnp.zeros((ne, 1, 1), I32)
    hi0 = jnp.full((ne, 1, 1), 0x7F800000, I32)
    thr, _ = lax.fori_loop(0, 31, thr_body, (lo0, hi0))

    tok_eq = jnp.where(bits == thr, tok_bt, ntok)
    need = cap - count(jnp.where(bits > thr, 1, 0))

    def cut_body(_, carry):
        lo, hi = carry
        mid = (lo + hi) >> 1
        ok = count(jnp.where(tok_eq < mid, 1, 0)) >= need
        return jnp.where(ok, lo, mid), jnp.where(ok, mid, hi)

    _, cut = lax.fori_loop(0, int(math.log2(ntok)) + 1, cut_body,
                           (jnp.zeros((ne, 1, 1), I32), jnp.full((ne, 1, 1), ntok, I32)))
    thr_scr[...] = jnp.broadcast_to(thr, thr_scr.shape)
    cut_scr[...] = jnp.broadcast_to(cut, cut_scr.shape)

    r_i = lax.broadcasted_iota(I32, (LANES, LANES), 0)
    c_i = lax.broadcasted_iota(I32, (LANES, LANES), 1)
    tri_t = jnp.where(c_i <= r_i, 1.0, 0.0).astype(BF16)
    rb_i = lax.broadcasted_iota(I32, (nb, nb), 0)
    cb_i = lax.broadcasted_iota(I32, (nb, nb), 1)
    tri_b = jnp.where(cb_i <= rb_i, 1.0, 0.0).astype(BF16)
    tok_b = lax.broadcasted_iota(I32, (nb, LANES), 0) * LANES + lax.broadcasted_iota(I32, (nb, LANES), 1)
    tok_t = lax.broadcasted_iota(I32, (LANES, nb), 1) * LANES + lax.broadcasted_iota(I32, (LANES, nb), 0)
    slot = lax.broadcasted_iota(I32, (1, cap_pad), 1).astype(F32)
    blk_iota = lax.broadcasted_iota(I32, (nb, 1), 0).astype(F32)
    lane_iota = lax.broadcasted_iota(I32, (LANES, 1), 0).astype(F32)

    def select(b, tok, th, ct):
        return jnp.where(b > th, 1.0, jnp.where(b == th, jnp.where(tok < ct, 1.0, 0.0), 0.0))

    def per_expert(e, carry):
        th = thr_scr[e][:1, :1]
        ct = cut_scr[e][:1, :1]
        a_tb = atb_ref[e]
        sel_bt = select(pltpu.bitcast(abt_ref[e], I32), tok_b, th, ct)
        sel_tb = select(pltpu.bitcast(a_tb, I32), tok_t, th, ct)
        cw_t = jnp.dot(tri_t, sel_tb.astype(BF16), preferred_element_type=F32)
        tot = jnp.sum(sel_bt, axis=1, keepdims=True)
        incl = jnp.dot(tri_b, jnp.broadcast_to(tot, (nb, LANES)).astype(BF16),
                       preferred_element_type=F32)[:, :1]
        excl = incl - tot
        blk = jnp.sum(jnp.where(incl <= slot, 1.0, 0.0), axis=0, keepdims=True)
        onehot = jnp.where(blk_iota == blk, 1.0, 0.0)
        oh16 = onehot.astype(BF16)
        g_t = jnp.dot(cw_t.astype(BF16), oh16, preferred_element_type=F32)
        base = jnp.sum(onehot * excl, axis=0, keepdims=True)
        tl = jnp.sum(jnp.where(g_t + base <= slot, 1.0, 0.0), axis=0, keepdims=True)
        idx_ref[pl.ds(e, 1), :] = (blk * LANES + tl).astype(I32)
        hi, mid, lo = _split3(a_tb)
        rows = (jnp.dot(hi, oh16, preferred_element_type=F32)
                + jnp.dot(mid, oh16, preferred_element_type=F32)
                + jnp.dot(lo, oh16, preferred_element_type=F32))
        gate_ref[pl.ds(e, 1), :] = jnp.sum(jnp.where(lane_iota == tl, rows, 0.0), axis=0, keepdims=True)
        return carry

    lax.fori_loop(0, ne, per_expert, 0)


def topk_select(aff_t, cap):
    ne, n = aff_t.shape
    nb = max(LANES, -(-n // LANES))
    nb = -(-nb // LANES) * LANES
    cap_pad = max(LANES, cap)
    padded = jnp.pad(aff_t, ((0, 0), (0, nb * LANES - n)), constant_values=-1.0)
    a_bt = padded.reshape(ne, nb, LANES)
    a_tb = jnp.swapaxes(a_bt, 1, 2)
    kern = functools.partial(_topk_kernel, cap=cap, cap_pad=cap_pad)
    idx, gate = pl.pallas_call(
        kern,
        out_shape=(jax.ShapeDtypeStruct((ne, cap_pad), I32), jax.ShapeDtypeStruct((ne, cap_pad), F32)),
        scratch_shapes=[pltpu.VMEM((ne, 8, LANES), I32), pltpu.VMEM((ne, 8, LANES), I32)],
        compiler_params=pltpu.CompilerParams(vmem_limit_bytes=VMEM_LIMIT),
        name="topk_select",
    )(a_bt, a_tb)
    return idx[:, :cap], gate[:, :cap]


def _moe_kernel(idx_ref, gate_ref, wgu_ref, wd_ref, g2_ref, h_hbm, xin_hbm, out_hbm,
                xs, ab, y_scr, sem_x, sem_a, sem_o, *, tc):
    del xin_hbm
    nct = pl.num_programs(1)
    ct = pl.program_id(1)
    s = pl.program_id(0) * nct + ct
    last = pl.num_programs(0) * nct - 1
    slot = s % 2
    base = s * tc
    nxt = jnp.minimum(s + 1, last) * tc

    def x_copy(b, r, sl):
        return pltpu.make_async_copy(h_hbm.at[pl.ds(idx_ref[b + r], 1)], xs.at[sl, pl.ds(r, 1)], sem_x.at[sl])

    def a_copy(r):
        return pltpu.make_async_copy(out_hbm.at[pl.ds(idx_ref[base + r], 1)], ab.at[slot, pl.ds(r, 1)],
                                     sem_a.at[slot])

    def o_copy(b, r, sl):
        return pltpu.make_async_copy(ab.at[sl, pl.ds(r, 1)], out_hbm.at[pl.ds(idx_ref[b + r], 1)], sem_o.at[sl])

    def rows(fn):
        def body(r, c):
            fn(r)
            return c
        lax.fori_loop(0, tc, body, 0, unroll=8)

    @pl.when(s == 0)
    def _():
        rows(lambda r: x_copy(0, r, 0).start())

    rows(lambda r: x_copy(base, r, slot).wait())

    @pl.when(jnp.logical_and(ct == 0, s > 0))
    def _():
        rows(lambda r: o_copy(base - tc, r, 1 - slot).wait())

    for r in range(tc):
        x_copy(nxt, r, 1 - slot).start()
        a_copy(r).start()
    xu = xs[slot]
    x_lo = pltpu.bitcast(xu << 16, F32).astype(BF16)
    x_hi = pltpu.bitcast(xu & jnp.uint32(0xFFFF0000), F32).astype(BF16)
    half = xu.shape[1]
    ab2 = (jnp.dot(x_lo, wgu_ref[0, :half], preferred_element_type=F32)
           + jnp.dot(x_hi, wgu_ref[0, half:], preferred_element_type=F32))
    f = ab2.shape[1] // 2
    a, b = ab2[:, :f], ab2[:, f:]
    gate = jnp.concatenate([gate_ref[0]] * (f // LANES), axis=1)
    hm = (a * jax.nn.sigmoid(a) * b * gate).astype(BF16)
    y_scr[...] = jnp.dot(hm, wd_ref[0], preferred_element_type=F32) * g2_ref[...]

    rows(lambda r: a_copy(r).wait())

    @pl.when(ct > 0)
    def _():
        rows(lambda r: o_copy(base - tc, r, 1 - slot).wait())

    ab[slot] = ab[slot] + y_scr[...]
    rows(lambda r: o_copy(base, r, slot).start())

    @pl.when(s == last)
    def _():
        rows(lambda r: o_copy(base, r, slot).wait())
        rows(lambda r: x_copy(nxt, r, 1 - slot).wait())


def moe_apply(h, x, idx, gate, wgu, wd, g2):
    n, d = x.shape
    ne, cap = idx.shape
    f = wd.shape[1]
    assert f % LANES == 0
    tc = _tile(cap, 256)
    gate_b = jnp.broadcast_to(gate[:, :, None], (ne, cap, LANES))
    kern = functools.partial(_moe_kernel, tc=tc)
    grid_spec = pltpu.PrefetchScalarGridSpec(
        num_scalar_prefetch=1,
        grid=(ne, cap // tc),
        in_specs=[pl.BlockSpec((1, tc, LANES), lambda e, c, idx: (e, c, 0)),
                  pl.BlockSpec((1, d, 2 * f), lambda e, c, idx: (e, 0, 0)),
                  pl.BlockSpec((1, f, d), lambda e, c, idx: (e, 0, 0)),
                  pl.BlockSpec((1, d), lambda e, c, idx: (0, 0)),
                  pl.BlockSpec(memory_space=pl.ANY),
                  pl.BlockSpec(memory_space=pl.ANY)],
        out_specs=pl.BlockSpec(memory_space=pl.ANY),
        scratch_shapes=[pltpu.VMEM((2, tc, d // 2), U32), pltpu.VMEM((2, tc, d), F32), pltpu.VMEM((tc, d), F32),
                        pltpu.SemaphoreType.DMA((2,)), pltpu.SemaphoreType.DMA((2,)),
                        pltpu.SemaphoreType.DMA((2,))],
    )
    return pl.pallas_call(
        kern,
        grid_spec=grid_spec,
        out_shape=jax.ShapeDtypeStruct((n, d), F32),
        input_output_aliases={6: 0},
        compiler_params=_params("arbitrary", "arbitrary"),
        name="moe_apply",
    )(idx.reshape(-1), gate_b, wgu, wd, g2, h, x)


def moe_layer(x, g, sh, sc, g2, w_r, wgu, wd):
    n = x.shape[0]
    ne = w_r.shape[1]
    h, aff_t = router(x, g, sh, sc, w_r.T)
    idx, gate = topk_select(aff_t, EC_CAPACITY_FACTOR * n // ne)
    return moe_apply(h, x, idx, gate, wgu, wd, g2)


def _rope_perm(width):
    quarter = HEAD_DIM // 4
    starts = jnp.array([0, 2, 4, 6, 1, 3, 5, 7]) * quarter
    blk = (starts[:, None] + jnp.arange(quarter)[None, :]).reshape(-1)
    return (jnp.arange(0, width, 2 * HEAD_DIM)[:, None] + blk[None, :]).reshape(-1)


def _permute_heads(w):
    rows, width = w.shape
    quarter = HEAD_DIM // 4
    w6 = w.reshape(rows, width // (2 * HEAD_DIM), 2, 2, 2, quarter)
    return jnp.transpose(w6, (0, 1, 4, 2, 3, 5)).reshape(rows, width)


def _rope_tables(n, n_ctx):
    quarter = HEAD_DIM // 4
    t = jnp.arange(n, dtype=I32)
    freqs = ROPE_THETA ** (-jnp.arange(quarter, dtype=F32) / quarter)
    ang_r = (t // GRID_W).astype(F32)[:, None] * freqs
    ang_c = (t % GRID_W).astype(F32)[:, None] * freqs
    cos = jnp.concatenate([jnp.cos(ang_r), jnp.cos(ang_c)] * 2, axis=1)
    sin = jnp.concatenate([jnp.sin(ang_r), jnp.sin(ang_c)] * 2, axis=1)
    cos = jnp.concatenate([cos, jnp.ones((n_ctx, HEAD_DIM), F32)], axis=0)
    sin = jnp.concatenate([sin, jnp.zeros((n_ctx, HEAD_DIM), F32)], axis=0)
    return cos, sin


def kernel(x, c, ctx, c_ctx, ada_w, ada_b, norm1_g, norm2_g, router_w, exp_w_gate, exp_w_up, exp_w_down,
           pool_w, pool_scale, gm_w_in, gm_v_g, gm_w_s, gm_b_s, gm_w_out, da_w_q, da_w_k, da_w_v, da_w_o,
           da_q_g, da_k_g, da_lam_q1, da_lam_k1, da_lam_q2, da_lam_k2, da_sub_g):
    bsz, n, d = x.shape
    assert bsz == 1 and c.shape[0] == 1
    n_ctx = ctx.shape[1]
    depth = ada_w.shape[0]
    xs, cs = x[0], ctx[0]

    s8 = jnp.concatenate([c, c_ctx[None], jnp.zeros((6, d), F32)], axis=0)
    mod = adaln(s8, ada_w, ada_b)

    def mods(i, row):
        return [mod[i, row:row + 1, k * d:(k + 1) * d] for k in range(6)]

    def row(v):
        return v.reshape(1, -1)

    for i in range(depth):
        kind, slot = i % N_MIXERS, i // N_MIXERS
        keep_ctx = any(j % N_MIXERS == 2 for j in range(i + 1, depth))
        sh1, sc1, g1, sh2, sc2, g2 = mods(i, 0)
        csh1, csc1, cg1, csh2, csc2, cg2 = mods(i, 1)
        n1, n2 = row(norm1_g[i]), row(norm2_g[i])
        streams = [(xs, sh1, sc1, g1)] + ([(cs, csh1, csc1, cg1)] if keep_ctx else [])

        if kind == 0:
            w = pool_w[slot].astype(BF16)
            outs = [pool_mixer(norm_mod(s, n1, sh, sc, F32), s, w, row(pool_scale[slot]), g)
                    for s, sh, sc, g in streams]
        elif kind == 1:
            w_in = gm_w_in[slot].astype(BF16)
            w_out = gm_w_out[slot].astype(BF16)
            w_s = gm_w_s[slot].astype(BF16)
            width = w_out.shape[0]
            bs_full = jnp.repeat(gm_b_s[slot].T, width // gm_w_s.shape[1], axis=1)
            outs = []
            for s, sh, sc, g in streams:
                uv, ssq = mm_gelu(norm_mod(s, n1, sh, sc, BF16), w_in)
                outs.append(gate_mm_resid(uv, ssq, row(gm_v_g[slot]), w_s, bs_full, w_out, s, g))
        else:
            assert not keep_ctx
            lam_init = 0.8 - 0.6 * math.exp(-0.3 * i)
            h_all = norm_mod_pair(xs, cs, n1, jnp.concatenate([sh1, csh1]), jnp.concatenate([sc1, csc1]), BF16)
            cos, sin = _rope_tables(n, n_ctx)
            gperm = (_rope_perm(2 * HEAD_DIM) % HEAD_DIM).reshape(2, HEAD_DIM)
            q = mm_qk(h_all, _permute_heads(da_w_q[slot]).astype(BF16), da_q_g[slot][gperm], cos, sin,
                      HEAD_DIM ** -0.5 * math.log2(math.e), m=n)
            k = mm_qk(h_all, _permute_heads(da_w_k[slot]).astype(BF16), da_k_g[slot][gperm], cos, sin, 1.0)
            vt = mm_transposed(h_all, da_w_v[slot].T.astype(BF16))
            lamv = jnp.stack([da_lam_q1[slot], da_lam_k1[slot], da_lam_q2[slot], da_lam_k2[slot]])
            o = diff_attention(q, k, vt, lamv, row(da_sub_g[slot]), lam_init)
            outs = [mm_resid(o, da_w_o[slot].astype(BF16), xs, g1)]

        wgu = jnp.concatenate([exp_w_gate[i], exp_w_up[i]], axis=-1).astype(BF16)
        wd = exp_w_down[i].astype(BF16)
        xs = moe_layer(outs[0], n2, sh2, sc2, g2, router_w[i], wgu, wd)
        if keep_ctx:
            cs = moe_layer(outs[1], n2, csh2, csc2, cg2, router_w[i], wgu, wd)
    return xs[None]
```

```python
import functools
import math

import jax
import jax.numpy as jnp
from jax import lax
from jax.experimental import pallas as pl
from jax.experimental.pallas import tpu as pltpu

F32 = jnp.float32
BF16 = jnp.bfloat16
I32 = jnp.int32
U32 = jnp.uint32

NORM_EPS = 1e-6
LANES = 128
GRID_W = 64
CHUNK = 128
POOL_WINDOWS = (2, 4, 8, 16)
HEAD_DIM = 128
ROPE_THETA = 10000.0
EC_CAPACITY_FACTOR = 2
N_MIXERS = 3
VMEM_LIMIT = 56 * 1024 * 1024
BOUND_SLACK = 1.01
MAX_FIXED_OFFSET = 60.0


def _params(*sem):
    return pltpu.CompilerParams(dimension_semantics=sem, vmem_limit_bytes=VMEM_LIMIT)


def _tile(n, t):
    t = min(n, t)
    assert n % t == 0, (n, t)
    return t


def _norm_mod(x, g, sh, sc):
    ms = jnp.mean(x * x, axis=-1, keepdims=True)
    return (x * lax.rsqrt(ms + NORM_EPS) * g) * (1.0 + sc) + sh


def _adaln_kernel(s_ref, w_ref, b_ref, o_ref):
    s = s_ref[...]
    s = s * jax.nn.sigmoid(s)
    o_ref[0] = jnp.dot(s, w_ref[0], preferred_element_type=F32) + b_ref[0]


def adaln(s8, ada_w, ada_b):
    depth, d, n6 = ada_w.shape
    tn = _tile(n6, 1024)
    return pl.pallas_call(
        _adaln_kernel,
        grid=(depth, n6 // tn),
        in_specs=[pl.BlockSpec((8, d), lambda l, j: (0, 0)),
                  pl.BlockSpec((1, d, tn), lambda l, j: (l, 0, j)),
                  pl.BlockSpec((1, 1, tn), lambda l, j: (l, 0, j))],
        out_specs=pl.BlockSpec((1, 8, tn), lambda l, j: (l, 0, j)),
        out_shape=jax.ShapeDtypeStruct((depth, 8, n6), F32),
        compiler_params=_params("parallel", "parallel"),
        name="adaln",
    )(s8, ada_w, ada_b.reshape(depth, 1, n6))


def _norm_mod_kernel(x_ref, g_ref, sh_ref, sc_ref, o_ref):
    o_ref[...] = _norm_mod(x_ref[...], g_ref[...], sh_ref[...], sc_ref[...]).astype(o_ref.dtype)


def norm_mod(x, g, sh, sc, dtype):
    n, d = x.shape
    tm = _tile(n, 256)
    vec = pl.BlockSpec((1, d), lambda i: (0, 0))
    return pl.pallas_call(
        _norm_mod_kernel,
        grid=(n // tm,),
        in_specs=[pl.BlockSpec((tm, d), lambda i: (i, 0)), vec, vec, vec],
        out_specs=pl.BlockSpec((tm, d), lambda i: (i, 0)),
        out_shape=jax.ShapeDtypeStruct((n, d), dtype),
        compiler_params=_params("parallel"),
        name="norm_mod",
    )(x, g, sh, sc)


def _norm_mod_pair_kernel(x_ref, c_ref, g_ref, sh_ref, sc_ref, o_ref, *, nx):
    is_ctx = pl.program_id(0) >= nx
    src = jnp.where(is_ctx, c_ref[...], x_ref[...])
    sh = jnp.where(is_ctx, sh_ref[1:2], sh_ref[0:1])
    sc = jnp.where(is_ctx, sc_ref[1:2], sc_ref[0:1])
    o_ref[...] = _norm_mod(src, g_ref[...], sh, sc).astype(o_ref.dtype)


def norm_mod_pair(x, ctx, g, sh2, sc2, dtype):
    n, d = x.shape
    nc = ctx.shape[0]
    tm = _tile(nc, 256)
    assert n % tm == 0
    nx = n // tm
    vec = pl.BlockSpec((1, d), lambda i: (0, 0))
    vec2 = pl.BlockSpec((2, d), lambda i: (0, 0))
    return pl.pallas_call(
        functools.partial(_norm_mod_pair_kernel, nx=nx),
        grid=(nx + nc // tm,),
        in_specs=[pl.BlockSpec((tm, d), lambda i: (jnp.minimum(i, nx - 1), 0)),
                  pl.BlockSpec((tm, d), lambda i: (jnp.maximum(i - nx, 0), 0)), vec, vec2, vec2],
        out_specs=pl.BlockSpec((tm, d), lambda i: (i, 0)),
        out_shape=jax.ShapeDtypeStruct((n + nc, d), dtype),
        compiler_params=_params("parallel"),
        name="norm_mod_pair",
    )(x, ctx, g, sh2, sc2)


def _pool_kernel(cur_ref, prev_ref, next_ref, x_ref, w_ref, ps_ref, g1_ref, o_ref, *, n, tm):
    g = pl.program_id(0)
    i = pl.program_id(1)
    last = pl.num_programs(1) - 1
    cur = cur_ref[...]
    prev = jnp.where(i == 0, 0.0, prev_ref[...])
    nxt = jnp.where(i == last, 0.0, next_ref[...])
    ext = jnp.concatenate([prev, cur, nxt], axis=0)
    t = i * tm + lax.broadcasted_iota(I32, (tm, 1), 0)

    for gi, win in enumerate(POOL_WINDOWS):
        @pl.when(g == gi)
        def _(win=win):
            half = win // 2
            s = ext
            step = 1
            while step < win:
                m = s.shape[0] - step
                s = s[:m] + s[step:step + m]
                step *= 2
            wsum = s[8 - half:8 - half + tm]
            cnt = jnp.minimum(t + half, n) - jnp.maximum(t - half, 0)
            dlt = (wsum / cnt.astype(F32) - cur).astype(BF16)
            y = jnp.dot(dlt, w_ref[0], preferred_element_type=F32) * ps_ref[...]
            o_ref[...] = x_ref[...] + g1_ref[...] * y


def pool_mixer(h, x, w, pscale, g1):
    n, d = x.shape
    ng, dg, _ = w.shape
    tm = _tile(n, 512)
    nb8 = n // 8
    kern = functools.partial(_pool_kernel, n=n, tm=tm)
    col = pl.BlockSpec((1, dg), lambda g, i: (0, g))
    return pl.pallas_call(
        kern,
        grid=(ng, n // tm),
        in_specs=[pl.BlockSpec((tm, dg), lambda g, i: (i, g)),
                  pl.BlockSpec((8, dg), lambda g, i: (jnp.maximum(i * (tm // 8) - 1, 0), g)),
                  pl.BlockSpec((8, dg), lambda g, i: (jnp.minimum((i + 1) * (tm // 8), nb8 - 1), g)),
                  pl.BlockSpec((tm, dg), lambda g, i: (i, g)),
                  pl.BlockSpec((1, dg, dg), lambda g, i: (g, 0, 0)),
                  col, col],
        out_specs=pl.BlockSpec((tm, dg), lambda g, i: (i, g)),
        out_shape=jax.ShapeDtypeStruct((n, d), F32),
        compiler_params=_params("parallel", "parallel"),
        name="pool_mixer",
    )(h, h, h, x, w, pscale, g1)


def _mm_call(kern, a, w, extra, extra_specs, out_shape, out_specs, tm, tn, name, scratch=(), m=None):
    k = a.shape[1]
    m = a.shape[0] if m is None else m
    n = w.shape[1]
    return pl.pallas_call(
        kern,
        grid=(pl.cdiv(m, tm), n // tn),
        in_specs=[pl.BlockSpec((tm, k), lambda i, j: (i, 0)),
                  pl.BlockSpec((k, tn), lambda i, j: (0, j))] + list(extra_specs),
        out_specs=out_specs,
        out_shape=out_shape,
        scratch_shapes=list(scratch),
        compiler_params=_params("parallel", "arbitrary"),
        name=name,
    )(a, w, *extra)


def _mm_t_kernel(a_ref, wt_ref, o_ref):
    o_ref[...] = lax.dot_general(wt_ref[...], a_ref[...], (((1,), (1,)), ((), ())),
                                 preferred_element_type=F32).astype(o_ref.dtype)


def mm_transposed(a, wt, dtype=BF16):
    m, k = a.shape
    n = wt.shape[0]
    tm, tn = min(m, 1024), _tile(n, 512)
    return pl.pallas_call(
        _mm_t_kernel,
        grid=(pl.cdiv(m, tm), n // tn),
        in_specs=[pl.BlockSpec((tm, k), lambda i, j: (i, 0)),
                  pl.BlockSpec((tn, k), lambda i, j: (j, 0))],
        out_specs=pl.BlockSpec((tn, tm), lambda i, j: (j, i)),
        out_shape=jax.ShapeDtypeStruct((n, m), dtype),
        compiler_params=_params("parallel", "arbitrary"),
        name="mm_transposed",
    )(a, wt)


def _mm_qk_kernel(a_ref, w_ref, g_ref, cos_ref, sin_ref, o_ref, *, scale):
    y = jnp.dot(a_ref[...], w_ref[...], preferred_element_type=F32)
    tn = y.shape[1]
    cos = cos_ref[...]
    sin = sin_ref[...]
    ga_cos, ga_sin = cos * g_ref[0:1], sin * g_ref[0:1]
    gb_cos, gb_sin = cos * g_ref[1:2], sin * g_ref[1:2]
    comp0 = lax.broadcasted_iota(I32, (1, HEAD_DIM), 1) < HEAD_DIM // 2
    for hd in range(tn // (2 * HEAD_DIM)):
        a = y[:, 2 * hd * HEAD_DIM:(2 * hd + 1) * HEAD_DIM]
        b = y[:, (2 * hd + 1) * HEAD_DIM:(2 * hd + 2) * HEAD_DIM]
        sq = a * a + b * b
        ssq0 = jnp.sum(jnp.where(comp0, sq, 0.0), axis=-1, keepdims=True)
        ssq1 = jnp.sum(jnp.where(comp0, 0.0, sq), axis=-1, keepdims=True)
        rstd = jnp.where(comp0, lax.rsqrt(ssq0 * (1.0 / HEAD_DIM) + NORM_EPS),
                         lax.rsqrt(ssq1 * (1.0 / HEAD_DIM) + NORM_EPS)) * scale
        o_ref[:, 2 * hd * HEAD_DIM:(2 * hd + 1) * HEAD_DIM] = ((a * ga_cos - b * gb_sin) * rstd).astype(o_ref.dtype)
        o_ref[:, (2 * hd + 1) * HEAD_DIM:(2 * hd + 2) * HEAD_DIM] = ((b * gb_cos + a * ga_sin) * rstd).astype(o_ref.dtype)


def mm_qk(a, w, gain, cos, sin, scale, m=None):
    m = a.shape[0] if m is None else m
    n = w.shape[1]
    tm, tn = min(m, 1024), _tile(n, 512)
    kern = functools.partial(_mm_qk_kernel, scale=scale)
    tab = pl.BlockSpec((tm, HEAD_DIM), lambda i, j: (i, 0))
    return _mm_call(kern, a, w, (gain, cos, sin),
                    (pl.BlockSpec((2, HEAD_DIM), lambda i, j: (0, 0)), tab, tab),
                    jax.ShapeDtypeStruct((m, n), BF16),
                    pl.BlockSpec((tm, tn), lambda i, j: (i, j)), tm, tn, "mm_qk", m=m)


def _mm_gelu_kernel(a_ref, w_ref, o_ref, ssq_ref, *, nj_half):
    j = pl.program_id(1)
    y = jax.nn.gelu(jnp.dot(a_ref[...], w_ref[...], preferred_element_type=F32))
    o_ref[...] = y.astype(o_ref.dtype)

    @pl.when(j == nj_half)
    def _():
        ssq_ref[...] = jnp.zeros_like(ssq_ref)

    @pl.when(j >= nj_half)
    def _():
        ssq_ref[...] += jnp.sum(y * y, axis=-1, keepdims=True)


def mm_gelu(a, w):
    m, _ = a.shape
    n = w.shape[1]
    tm, tn = min(m, 1024), _tile(n // 2, 512)
    kern = functools.partial(_mm_gelu_kernel, nj_half=(n // 2) // tn)
    return _mm_call(kern, a, w, (), (),
                    (jax.ShapeDtypeStruct((m, n), BF16), jax.ShapeDtypeStruct((m, LANES), F32)),
                    (pl.BlockSpec((tm, tn), lambda i, j: (i, j)),
                     pl.BlockSpec((tm, LANES), lambda i, j: (i, 0))), tm, tn, "mm_gelu")


def _mm_resid_kernel(a_ref, w_ref, x_ref, g_ref, o_ref):
    y = jnp.dot(a_ref[...], w_ref[...], preferred_element_type=F32)
    o_ref[...] = x_ref[...] + g_ref[...] * y


def mm_resid(a, w, x, g1):
    m, _ = a.shape
    n = w.shape[1]
    tm, tn = min(m, 1024), _tile(n, 512)
    blk = pl.BlockSpec((tm, tn), lambda i, j: (i, j))
    return _mm_call(_mm_resid_kernel, a, w, (x, g1),
                    (blk, pl.BlockSpec((1, tn), lambda i, j: (0, j))),
                    jax.ShapeDtypeStruct((m, n), F32), blk, tm, tn, "mm_resid")


def _gate_mm_kernel(u_ref, v_ref, ssq_ref, vg_ref, ws_ref, bs_ref, w_ref, x_ref, g_ref, o_ref, z_scr, *, width):
    j = pl.program_id(1)
    tm = u_ref.shape[0]

    @pl.when(j == 0)
    def _():
        rstd = lax.rsqrt(ssq_ref[:, :1] * (1.0 / width) + NORM_EPS)

        def body(g, carry):
            col = pl.multiple_of(g * LANES, LANES)
            vg = vg_ref[:, pl.ds(col, LANES)]
            chunks = [slice(c * CHUNK, (c + 1) * CHUNK) for c in range(tm // CHUNK)]
            vn = jnp.concatenate([(v_ref[rows, pl.ds(col, LANES)].astype(F32) * rstd[rows] * vg).astype(BF16)
                                  for rows in chunks], axis=1)
            sv = jnp.dot(ws_ref[g], vn, preferred_element_type=F32)
            bias = bs_ref[:, pl.ds(col, LANES)]
            for c, rows in enumerate(chunks):
                gate = sv[:, c * LANES:(c + 1) * LANES] + bias
                z_scr[rows, pl.ds(col, LANES)] = (u_ref[rows, pl.ds(col, LANES)].astype(F32) * gate).astype(BF16)
            return carry

        lax.fori_loop(0, width // LANES, body, 0, unroll=2)

    y = jnp.dot(z_scr[...], w_ref[...], preferred_element_type=F32)
    o_ref[...] = x_ref[...] + g_ref[...] * y


def gate_mm_resid(uv, ssq, vg, ws, bs_full, w_out, x, g1):
    n, w2 = uv.shape
    width = w2 // 2
    d = w_out.shape[1]
    tm, tn = _tile(n, 512), _tile(d, 512)
    kern = functools.partial(_gate_mm_kernel, width=width)
    blk = pl.BlockSpec((tm, tn), lambda i, j: (i, j))
    return pl.pallas_call(
        kern,
        grid=(n // tm, d // tn),
        in_specs=[pl.BlockSpec((tm, width), lambda i, j: (i, 0)),
                  pl.BlockSpec((tm, width), lambda i, j: (i, 1)),
                  pl.BlockSpec((tm, LANES), lambda i, j: (i, 0)),
                  pl.BlockSpec((1, width), lambda i, j: (0, 0)),
                  pl.BlockSpec(ws.shape, lambda i, j: (0, 0, 0)),
                  pl.BlockSpec((CHUNK, width), lambda i, j: (0, 0)),
                  pl.BlockSpec((width, tn), lambda i, j: (0, j)),
                  blk,
                  pl.BlockSpec((1, tn), lambda i, j: (0, j))],
        out_specs=blk,
        out_shape=jax.ShapeDtypeStruct((n, d), F32),
        scratch_shapes=[pltpu.VMEM((tm, width), BF16)],
        compiler_params=_params("parallel", "arbitrary"),
        name="gate_mm_resid",
    )(uv, uv, ssq, vg, ws, bs_full, w_out, x, g1)


def _attn_kernel(lamv_ref, off_ref, q_ref, k_ref, vt_ref, sg_ref, o_ref, m_scr, l_scr, acc_scr, p_scr,
                 *, lam_init, tc, bounded):
    ki = pl.program_id(2)
    tk = k_ref.shape[0]
    nbuf = p_scr.shape[0]
    nchunk = q_ref.shape[0] // tc
    lane = lax.broadcasted_iota(I32, (1, 2 * HEAD_DIM), 1) % HEAD_DIM
    qcs = [q_ref[...] * jnp.where((lane < HEAD_DIM // 2) == (c == 0), 1.0, 0.0).astype(BF16) for c in range(2)]

    @pl.when(ki == 0)
    def _():
        l_scr[...] = jnp.zeros_like(l_scr)
        acc_scr[...] = jnp.zeros_like(acc_scr)
        if not bounded:
            m_scr[...] = jnp.full_like(m_scr, -jnp.inf)

    vt = vt_ref[...]
    kt = k_ref[...]
    scores = {}
    for c in range(2):
        for r in range(nchunk):
            scores[r, c] = lax.dot_general(kt, qcs[c][r * tc:(r + 1) * tc], (((1,), (1,)), ((), ())),
                                           preferred_element_type=F32)
    for r in range(nchunk):
        qs = slice(r * tc, (r + 1) * tc)
        for c in range(2):
            buf = (2 * r + c) % nbuf
            s = scores[r, c]
            if bounded:
                m_new = off_ref[:, :1]
            else:
                m_prev = m_scr[c, :, qs]
                m_new = jnp.maximum(m_prev, jnp.max(s, axis=0, keepdims=True))
            psum = jnp.zeros((16, tc), F32)
            for g in range(tk // 16):
                pg = jnp.exp2(s[16 * g:16 * (g + 1)] - m_new)
                psum = psum + pg
                p_scr[buf, 16 * g:16 * (g + 1), :] = pg.astype(BF16)
            lsum = jnp.sum(psum, axis=0, keepdims=True)
            pv = jnp.dot(vt, p_scr[buf], preferred_element_type=F32)
            if bounded:
                l_scr[c, :, qs] += lsum
                acc_scr[c, :, qs] += pv
            else:
                alpha = jnp.exp2(m_prev - m_new)
                l_scr[c, :, qs] = alpha * l_scr[c, :, qs] + lsum
                acc_scr[c, :, qs] = alpha * acc_scr[c, :, qs] + pv
                m_scr[c, :, qs] = m_new

    @pl.when(ki == pl.num_programs(2) - 1)
    def _():
        lv = lamv_ref[...]
        lam = (jnp.exp(jnp.sum(lv[0:1] * lv[1:2], axis=-1, keepdims=True))
               - jnp.exp(jnp.sum(lv[2:3] * lv[3:4], axis=-1, keepdims=True)) + lam_init)
        ot = acc_scr[0] / l_scr[0] - lam * (acc_scr[1] / l_scr[1])
        ot = ot * lax.rsqrt(jnp.mean(ot * ot, axis=0, keepdims=True) + NORM_EPS)
        o_ref[...] = (ot.T * (sg_ref[...] * (1.0 - lam_init))).astype(o_ref.dtype)


def _key_tile(nk, cap):
    best = LANES
    for t in range(LANES, cap + 1, LANES):
        if nk % t == 0:
            best = t
    return best


def diff_attention(q, k, vt, lamv, sub_g, lam_init, score_bound):
    n, d = q.shape
    nk = k.shape[0]
    hw = 2 * HEAD_DIM
    heads = d // hw
    tq = _tile(n, 2048)
    tk = _key_tile(nk, 1536)
    tc = min(tq, 256)

    safe = score_bound <= MAX_FIXED_OFFSET
    off = jnp.full((1, LANES), score_bound, F32)

    def call(bounded):
        kern = functools.partial(_attn_kernel, lam_init=lam_init, tc=tc, bounded=bounded)
        return pl.pallas_call(
            kern,
            grid=(heads, n // tq, nk // tk),
            in_specs=[pl.BlockSpec((4, HEAD_DIM), lambda h, i, j: (0, 0)),
                      pl.BlockSpec((1, LANES), lambda h, i, j: (0, 0)),
                      pl.BlockSpec((tq, hw), lambda h, i, j: (i, h)),
                      pl.BlockSpec((tk, hw), lambda h, i, j: (j, h)),
                      pl.BlockSpec((hw, tk), lambda h, i, j: (h, j)),
                      pl.BlockSpec((1, hw), lambda h, i, j: (0, 0))],
            out_specs=pl.BlockSpec((tq, hw), lambda h, i, j: (i, h)),
            out_shape=jax.ShapeDtypeStruct((n, d), BF16),
            scratch_shapes=[pltpu.VMEM((2, 1, tq), F32), pltpu.VMEM((2, 1, tq), F32),
                            pltpu.VMEM((2, hw, tq), F32), pltpu.VMEM((2 * (tq // tc), tk, tc), BF16)],
            compiler_params=_params("parallel", "parallel", "arbitrary"),
            name="diff_attention_bounded" if bounded else "diff_attention_online",
        )(lamv, off, q, k, vt, sub_g)

    return lax.cond(safe, lambda: call(True), lambda: call(False))


def _router_kernel(x_ref, g_ref, sh_ref, sc_ref, wrt_ref, h_ref, aff_ref):
    h = _norm_mod(x_ref[...], g_ref[...], sh_ref[...], sc_ref[...])
    half = h.shape[1] // 2
    bits = pltpu.bitcast(h.astype(BF16).astype(F32), U32)
    h_ref[...] = (bits[:, :half] >> 16) | (bits[:, half:] & jnp.uint32(0xFFFF0000))
    logits = lax.dot_general(wrt_ref[...], h, (((1,), (1,)), ((), ())),
                             preferred_element_type=F32, precision=lax.Precision.HIGHEST)
    ex = jnp.exp(logits - jnp.max(logits, axis=0, keepdims=True))
    aff_ref[...] = ex / jnp.sum(ex, axis=0, keepdims=True)


def router(x, g, sh, sc, w_r_t):
    n, d = x.shape
    e = w_r_t.shape[0]
    tm = _tile(n, 256)
    vec = pl.BlockSpec((1, d), lambda i: (0, 0))
    return pl.pallas_call(
        _router_kernel,
        grid=(n // tm,),
        in_specs=[pl.BlockSpec((tm, d), lambda i: (i, 0)), vec, vec, vec,
                  pl.BlockSpec((e, d), lambda i: (0, 0))],
        out_specs=(pl.BlockSpec((tm, d // 2), lambda i: (i, 0)), pl.BlockSpec((e, tm), lambda i: (0, i))),
        out_shape=(jax.ShapeDtypeStruct((n, d // 2), U32), jax.ShapeDtypeStruct((e, n), F32)),
        compiler_params=_params("parallel"),
        name="router",
    )(x, g, sh, sc, w_r_t)


def _split3(a):
    hi = a.astype(BF16)
    r1 = a - hi.astype(F32)
    mid = r1.astype(BF16)
    lo = (r1 - mid.astype(F32)).astype(BF16)
    return hi, mid, lo


def _topk_kernel(abt_ref, atb_ref, idx_ref, gate_ref, thr_scr, cut_scr, *, cap, cap_pad):
    ne, nb, _ = abt_ref.shape
    ntok = nb * LANES
    bits = pltpu.bitcast(abt_ref[...], I32)
    tok_bt = (lax.broadcasted_iota(I32, (1, nb, LANES), 1) * LANES
              + lax.broadcasted_iota(I32, (1, nb, LANES), 2))

    def count(ones):
        c = jnp.sum(ones, axis=2, keepdims=True)
        return jnp.sum(c, axis=1, keepdims=True)

    def thr_body(_, carry):
        lo, hi = carry
        mid = lo + ((hi - lo + 1) >> 1)
        ok = count(jnp.where(bits >= mid, 1, 0)) >= cap
        return jnp.where(ok, mid, lo), jnp.where(ok, hi, mid - 1)

    lo0 = jnp.zeros((ne, 1, 1), I32)
    hi0 = jnp.full((ne, 1, 1), 0x7F800000, I32)
    thr, _ = lax.fori_loop(0, 31, thr_body, (lo0, hi0))

    tok_eq = jnp.where(bits == thr, tok_bt, ntok)
    need = cap - count(jnp.where(bits > thr, 1, 0))

    def cut_body(_, carry):
        lo, hi = carry
        mid = (lo + hi) >> 1
        ok = count(jnp.where(tok_eq < mid, 1, 0)) >= need
        return jnp.where(ok, lo, mid), jnp.where(ok, mid, hi)

    _, cut = lax.fori_loop(0, int(math.log2(ntok)) + 1, cut_body,
                           (jnp.zeros((ne, 1, 1), I32), jnp.full((ne, 1, 1), ntok, I32)))
    thr_scr[...] = jnp.broadcast_to(thr, thr_scr.shape)
    cut_scr[...] = jnp.broadcast_to(cut, cut_scr.shape)

    r_i = lax.broadcasted_iota(I32, (LANES, LANES), 0)
    c_i = lax.broadcasted_iota(I32, (LANES, LANES), 1)
    tri_t = jnp.where(c_i <= r_i, 1.0, 0.0).astype(BF16)
    rb_i = lax.broadcasted_iota(I32, (nb, nb), 0)
    cb_i = lax.broadcasted_iota(I32, (nb, nb), 1)
    tri_b = jnp.where(cb_i <= rb_i, 1.0, 0.0).astype(BF16)
    tok_b = lax.broadcasted_iota(I32, (nb, LANES), 0) * LANES + lax.broadcasted_iota(I32, (nb, LANES), 1)
    tok_t = lax.broadcasted_iota(I32, (LANES, nb), 1) * LANES + lax.broadcasted_iota(I32, (LANES, nb), 0)
    slot = lax.broadcasted_iota(I32, (1, cap_pad), 1).astype(F32)
    blk_iota = lax.broadcasted_iota(I32, (nb, 1), 0).astype(F32)
    lane_iota = lax.broadcasted_iota(I32, (LANES, 1), 0).astype(F32)

    def select(b, tok, th, ct):
        return jnp.where(b > th, 1.0, jnp.where(b == th, jnp.where(tok < ct, 1.0, 0.0), 0.0))

    def per_expert(e, carry):
        th = thr_scr[e][:1, :1]
        ct = cut_scr[e][:1, :1]
        a_tb = atb_ref[e]
        sel_bt = select(pltpu.bitcast(abt_ref[e], I32), tok_b, th, ct)
        sel_tb = select(pltpu.bitcast(a_tb, I32), tok_t, th, ct)
        cw_t = jnp.dot(tri_t, sel_tb.astype(BF16), preferred_element_type=F32)
        tot = jnp.sum(sel_bt, axis=1, keepdims=True)
        incl = jnp.dot(tri_b, jnp.broadcast_to(tot, (nb, LANES)).astype(BF16),
                       preferred_element_type=F32)[:, :1]
        excl = incl - tot
        blk = jnp.sum(jnp.where(incl <= slot, 1.0, 0.0), axis=0, keepdims=True)
        onehot = jnp.where(blk_iota == blk, 1.0, 0.0)
        oh16 = onehot.astype(BF16)
        g_t = jnp.dot(cw_t.astype(BF16), oh16, preferred_element_type=F32)
        base = jnp.sum(onehot * excl, axis=0, keepdims=True)
        tl = jnp.sum(jnp.where(g_t + base <= slot, 1.0, 0.0), axis=0, keepdims=True)
        idx_ref[pl.ds(e, 1), :] = (blk * LANES + tl).astype(I32)
        hi, mid, lo = _split3(a_tb)
        rows = (jnp.dot(hi, oh16, preferred_element_type=F32)
                + jnp.dot(mid, oh16, preferred_element_type=F32)
                + jnp.dot(lo, oh16, preferred_element_type=F32))
        gate_ref[pl.ds(e, 1), :] = jnp.sum(jnp.where(lane_iota == tl, rows, 0.0), axis=0, keepdims=True)
        return carry

    lax.fori_loop(0, ne, per_expert, 0)


def topk_select(aff_t, cap):
    ne, n = aff_t.shape
    nb = max(LANES, -(-n // LANES))
    nb = -(-nb // LANES) * LANES
    cap_pad = max(LANES, cap)
    padded = jnp.pad(aff_t, ((0, 0), (0, nb * LANES - n)), constant_values=-1.0)
    a_bt = padded.reshape(ne, nb, LANES)
    a_tb = jnp.swapaxes(a_bt, 1, 2)
    kern = functools.partial(_topk_kernel, cap=cap, cap_pad=cap_pad)
    idx, gate = pl.pallas_call(
        kern,
        out_shape=(jax.ShapeDtypeStruct((ne, cap_pad), I32), jax.ShapeDtypeStruct((ne, cap_pad), F32)),
        scratch_shapes=[pltpu.VMEM((ne, 8, LANES), I32), pltpu.VMEM((ne, 8, LANES), I32)],
        compiler_params=pltpu.CompilerParams(vmem_limit_bytes=VMEM_LIMIT),
        name="topk_select",
    )(a_bt, a_tb)
    return idx[:, :cap], gate[:, :cap]


def _moe_kernel(idx_ref, gate_ref, g2_ref, h_hbm, wg_hbm, wu_hbm, wd_hbm, xin_hbm, out_hbm,
                xs, ab, y_scr, wg_f32, wu_f32, wd_f32, wgu, wd, sem_x, sem_a, sem_o, sem_w, *, tc, layer):
    del xin_hbm
    nct = pl.num_programs(1)
    ct = pl.program_id(1)
    s = pl.program_id(0) * nct + ct
    last = pl.num_programs(0) * nct - 1
    slot = s % 2
    base = s * tc
    nxt = jnp.minimum(s + 1, last) * tc

    def x_copy(b, r, sl):
        return pltpu.make_async_copy(h_hbm.at[pl.ds(idx_ref[b + r], 1)], xs.at[sl, pl.ds(r, 1)], sem_x.at[sl])

    def a_copy(r):
        return pltpu.make_async_copy(out_hbm.at[pl.ds(idx_ref[base + r], 1)], ab.at[slot, pl.ds(r, 1)],
                                     sem_a.at[slot])

    def o_copy(b, r, sl):
        return pltpu.make_async_copy(ab.at[sl, pl.ds(r, 1)], out_hbm.at[pl.ds(idx_ref[b + r], 1)], sem_o.at[sl])

    def rows(fn):
        def body(r, c):
            fn(r)
            return c
        lax.fori_loop(0, tc, body, 0, unroll=8)

    ne = pl.num_programs(0)
    expert = pl.program_id(0)

    def w_copies(e):
        return (pltpu.make_async_copy(wg_hbm.at[layer, e], wg_f32, sem_w.at[0]),
                pltpu.make_async_copy(wu_hbm.at[layer, e], wu_f32, sem_w.at[1]),
                pltpu.make_async_copy(wd_hbm.at[layer, e], wd_f32, sem_w.at[2]))

    @pl.when(s == 0)
    def _():
        for cp in w_copies(0):
            cp.start()
        rows(lambda r: x_copy(0, r, 0).start())

    @pl.when(ct == 0)
    def _():
        for cp in w_copies(expert):
            cp.wait()
        f = wd.shape[0]
        wgu[:, :f] = wg_f32[...].astype(BF16)
        wgu[:, f:] = wu_f32[...].astype(BF16)
        wd[...] = wd_f32[...].astype(BF16)

    @pl.when(jnp.logical_and(ct == 0, expert + 1 < ne))
    def _():
        for cp in w_copies(expert + 1):
            cp.start()

    rows(lambda r: x_copy(base, r, slot).wait())

    @pl.when(jnp.logical_and(ct == 0, s > 0))
    def _():
        rows(lambda r: o_copy(base - tc, r, 1 - slot).wait())

    for r in range(tc):
        x_copy(nxt, r, 1 - slot).start()
        a_copy(r).start()
    xu = xs[slot]
    x_lo = pltpu.bitcast(xu << 16, F32).astype(BF16)
    x_hi = pltpu.bitcast(xu & jnp.uint32(0xFFFF0000), F32).astype(BF16)
    half = xu.shape[1]
    ab2 = (jnp.dot(x_lo, wgu[:half], preferred_element_type=F32)
           + jnp.dot(x_hi, wgu[half:], preferred_element_type=F32))
    f = ab2.shape[1] // 2
    a, b = ab2[:, :f], ab2[:, f:]
    gate = jnp.concatenate([gate_ref[0]] * (f // LANES), axis=1)
    hm = (a * jax.nn.sigmoid(a) * b * gate).astype(BF16)
    y_scr[...] = jnp.dot(hm, wd[...], preferred_element_type=F32) * g2_ref[...]

    rows(lambda r: a_copy(r).wait())

    @pl.when(ct > 0)
    def _():
        rows(lambda r: o_copy(base - tc, r, 1 - slot).wait())

    ab[slot] = ab[slot] + y_scr[...]
    rows(lambda r: o_copy(base, r, slot).start())

    @pl.when(s == last)
    def _():
        rows(lambda r: o_copy(base, r, slot).wait())
        rows(lambda r: x_copy(nxt, r, 1 - slot).wait())


def moe_apply(h, x, idx, gate, w_gate, w_up, w_down, layer, g2):
    n, d = x.shape
    ne, cap = idx.shape
    f = w_down.shape[2]
    assert f % LANES == 0
    tc = _tile(cap, 256)
    gate_b = jnp.broadcast_to(gate[:, :, None], (ne, cap, LANES))
    kern = functools.partial(_moe_kernel, tc=tc, layer=layer)
    hbm = pl.BlockSpec(memory_space=pl.ANY)
    grid_spec = pltpu.PrefetchScalarGridSpec(
        num_scalar_prefetch=1,
        grid=(ne, cap // tc),
        in_specs=[pl.BlockSpec((1, tc, LANES), lambda e, c, idx: (e, c, 0)),
                  pl.BlockSpec((1, d), lambda e, c, idx: (0, 0)),
                  hbm, hbm, hbm, hbm, hbm],
        out_specs=hbm,
        scratch_shapes=[pltpu.VMEM((2, tc, d // 2), U32), pltpu.VMEM((2, tc, d), F32), pltpu.VMEM((tc, d), F32),
                        pltpu.VMEM((d, f), F32), pltpu.VMEM((d, f), F32), pltpu.VMEM((f, d), F32),
                        pltpu.VMEM((d, 2 * f), BF16), pltpu.VMEM((f, d), BF16),
                        pltpu.SemaphoreType.DMA((2,)), pltpu.SemaphoreType.DMA((2,)),
                        pltpu.SemaphoreType.DMA((2,)), pltpu.SemaphoreType.DMA((3,))],
    )
    return pl.pallas_call(
        kern,
        grid_spec=grid_spec,
        out_shape=jax.ShapeDtypeStruct((n, d), F32),
        input_output_aliases={7: 0},
        compiler_params=_params("arbitrary", "arbitrary"),
        name="moe_apply",
    )(idx.reshape(-1), gate_b, g2, h, w_gate, w_up, w_down, x)


def moe_layer(x, g, sh, sc, g2, w_r, w_gate, w_up, w_down, layer):
    n = x.shape[0]
    ne = w_r.shape[1]
    h, aff_t = router(x, g, sh, sc, w_r.T)
    idx, gate = topk_select(aff_t, EC_CAPACITY_FACTOR * n // ne)
    return moe_apply(h, x, idx, gate, w_gate, w_up, w_down, layer, g2)


def _rope_perm(width):
    quarter = HEAD_DIM // 4
    starts = jnp.array([0, 2, 4, 6, 1, 3, 5, 7]) * quarter
    blk = (starts[:, None] + jnp.arange(quarter)[None, :]).reshape(-1)
    return (jnp.arange(0, width, 2 * HEAD_DIM)[:, None] + blk[None, :]).reshape(-1)


def _permute_heads(w):
    rows, width = w.shape
    quarter = HEAD_DIM // 4
    w6 = w.reshape(rows, width // (2 * HEAD_DIM), 2, 2, 2, quarter)
    return jnp.transpose(w6, (0, 1, 4, 2, 3, 5)).reshape(rows, width)


def _rope_tables(n, n_ctx):
    quarter = HEAD_DIM // 4
    t = jnp.arange(n, dtype=I32)
    freqs = ROPE_THETA ** (-jnp.arange(quarter, dtype=F32) / quarter)
    ang_r = (t // GRID_W).astype(F32)[:, None] * freqs
    ang_c = (t % GRID_W).astype(F32)[:, None] * freqs
    cos = jnp.concatenate([jnp.cos(ang_r), jnp.cos(ang_c)] * 2, axis=1)
    sin = jnp.concatenate([jnp.sin(ang_r), jnp.sin(ang_c)] * 2, axis=1)
    cos = jnp.concatenate([cos, jnp.ones((n_ctx, HEAD_DIM), F32)], axis=0)
    sin = jnp.concatenate([sin, jnp.zeros((n_ctx, HEAD_DIM), F32)], axis=0)
    return cos, sin


def kernel(x, c, ctx, c_ctx, ada_w, ada_b, norm1_g, norm2_g, router_w, exp_w_gate, exp_w_up, exp_w_down,
           pool_w, pool_scale, gm_w_in, gm_v_g, gm_w_s, gm_b_s, gm_w_out, da_w_q, da_w_k, da_w_v, da_w_o,
           da_q_g, da_k_g, da_lam_q1, da_lam_k1, da_lam_q2, da_lam_k2, da_sub_g):
    bsz, n, d = x.shape
    assert bsz == 1 and c.shape[0] == 1
    n_ctx = ctx.shape[1]
    depth = ada_w.shape[0]
    xs, cs = x[0], ctx[0]

    s8 = jnp.concatenate([c, c_ctx[None], jnp.zeros((6, d), F32)], axis=0)
    mod = adaln(s8, ada_w, ada_b)

    def mods(i, row):
        return [mod[i, row:row + 1, k * d:(k + 1) * d] for k in range(6)]

    def row(v):
        return v.reshape(1, -1)

    for i in range(depth):
        kind, slot = i % N_MIXERS, i // N_MIXERS
        keep_ctx = any(j % N_MIXERS == 2 for j in range(i + 1, depth))
        sh1, sc1, g1, sh2, sc2, g2 = mods(i, 0)
        csh1, csc1, cg1, csh2, csc2, cg2 = mods(i, 1)
        n1, n2 = row(norm1_g[i]), row(norm2_g[i])
        streams = [(xs, sh1, sc1, g1)] + ([(cs, csh1, csc1, cg1)] if keep_ctx else [])

        if kind == 0:
            w = pool_w[slot].astype(BF16)
            outs = [pool_mixer(norm_mod(s, n1, sh, sc, F32), s, w, row(pool_scale[slot]), g)
                    for s, sh, sc, g in streams]
        elif kind == 1:
            w_in = gm_w_in[slot].astype(BF16)
            w_out = gm_w_out[slot].astype(BF16)
            w_s = gm_w_s[slot].astype(BF16)
            width = w_out.shape[0]
            bs_full = jnp.repeat(gm_b_s[slot].T, width // gm_w_s.shape[1], axis=1)
            outs = []
            for s, sh, sc, g in streams:
                uv, ssq = mm_gelu(norm_mod(s, n1, sh, sc, BF16), w_in)
                outs.append(gate_mm_resid(uv, ssq, row(gm_v_g[slot]), w_s, bs_full, w_out, s, g))
        else:
            assert not keep_ctx
            lam_init = 0.8 - 0.6 * math.exp(-0.3 * i)
            h_all = norm_mod_pair(xs, cs, n1, jnp.concatenate([sh1, csh1]), jnp.concatenate([sc1, csc1]), BF16)
            cos, sin = _rope_tables(n, n_ctx)
            gperm = (_rope_perm(2 * HEAD_DIM) % HEAD_DIM).reshape(2, HEAD_DIM)
            q_scale = HEAD_DIM ** -0.5 * math.log2(math.e)
            q = mm_qk(h_all, _permute_heads(da_w_q[slot]).astype(BF16), da_q_g[slot][gperm], cos, sin, q_scale, m=n)
            k = mm_qk(h_all, _permute_heads(da_w_k[slot]).astype(BF16), da_k_g[slot][gperm], cos, sin, 1.0)
            vt = mm_transposed(h_all, da_w_v[slot].T.astype(BF16))
            lamv = jnp.stack([da_lam_q1[slot], da_lam_k1[slot], da_lam_q2[slot], da_lam_k2[slot]])
            score_bound = (HEAD_DIM * q_scale * BOUND_SLACK) * jnp.max(jnp.abs(da_q_g[slot])) * jnp.max(jnp.abs(da_k_g[slot]))
            o = diff_attention(q, k, vt, lamv, row(da_sub_g[slot]), lam_init, score_bound)
            outs = [mm_resid(o, da_w_o[slot].astype(BF16), xs, g1)]

        xs = moe_layer(outs[0], n2, sh2, sc2, g2, router_w[i], exp_w_gate, exp_w_up, exp_w_down, i)
        if keep_ctx:
            cs = moe_layer(outs[1], n2, csh2, csc2, cg2, router_w[i], exp_w_gate, exp_w_up, exp_w_down, i)
    return xs[None]
```

```python
import functools
import math

import jax
import jax.numpy as jnp
from jax import lax
from jax.experimental import pallas as pl
from jax.experimental.pallas import tpu as pltpu

F32 = jnp.float32
BF16 = jnp.bfloat16
I32 = jnp.int32
U32 = jnp.uint32

NORM_EPS = 1e-6
LANES = 128
GRID_W = 64
CHUNK = 128
POOL_WINDOWS = (2, 4, 8, 16)
HEAD_DIM = 128
ROPE_THETA = 10000.0
EC_CAPACITY_FACTOR = 2
N_MIXERS = 3
VMEM_LIMIT = 56 * 1024 * 1024
BOUND_SLACK = 1.01
MAX_FIXED_OFFSET = 60.0


def _params(*sem):
    return pltpu.CompilerParams(dimension_semantics=sem, vmem_limit_bytes=VMEM_LIMIT)


def _tile(n, t):
    t = min(n, t)
    assert n % t == 0, (n, t)
    return t


def _norm_mod(x, g, sh, sc):
    ms = jnp.mean(x * x, axis=-1, keepdims=True)
    return (x * lax.rsqrt(ms + NORM_EPS) * g) * (1.0 + sc) + sh


def _adaln_kernel(s_ref, w_ref, b_ref, o_ref):
    s = s_ref[...]
    s = s * jax.nn.sigmoid(s)
    o_ref[0] = jnp.dot(s, w_ref[0], preferred_element_type=F32) + b_ref[0]


def adaln(s8, ada_w, ada_b):
    depth, d, n6 = ada_w.shape
    tn = _tile(n6, 1024)
    return pl.pallas_call(
        _adaln_kernel,
        grid=(depth, n6 // tn),
        in_specs=[pl.BlockSpec((8, d), lambda l, j: (0, 0)),
                  pl.BlockSpec((1, d, tn), lambda l, j: (l, 0, j)),
                  pl.BlockSpec((1, 1, tn), lambda l, j: (l, 0, j))],
        out_specs=pl.BlockSpec((1, 8, tn), lambda l, j: (l, 0, j)),
        out_shape=jax.ShapeDtypeStruct((depth, 8, n6), F32),
        compiler_params=_params("parallel", "parallel"),
        name="adaln",
    )(s8, ada_w, ada_b.reshape(depth, 1, n6))


def _norm_mod_kernel(x_ref, g_ref, sh_ref, sc_ref, o_ref):
    o_ref[...] = _norm_mod(x_ref[...], g_ref[...], sh_ref[...], sc_ref[...]).astype(o_ref.dtype)


def norm_mod(x, g, sh, sc, dtype):
    n, d = x.shape
    tm = _tile(n, 256)
    vec = pl.BlockSpec((1, d), lambda i: (0, 0))
    return pl.pallas_call(
        _norm_mod_kernel,
        grid=(n // tm,),
        in_specs=[pl.BlockSpec((tm, d), lambda i: (i, 0)), vec, vec, vec],
        out_specs=pl.BlockSpec((tm, d), lambda i: (i, 0)),
        out_shape=jax.ShapeDtypeStruct((n, d), dtype),
        compiler_params=_params("parallel"),
        name="norm_mod",
    )(x, g, sh, sc)


def _norm_mod_pair_kernel(x_ref, c_ref, g_ref, sh_ref, sc_ref, o_ref, *, nx):
    is_ctx = pl.program_id(0) >= nx
    src = jnp.where(is_ctx, c_ref[...], x_ref[...])
    sh = jnp.where(is_ctx, sh_ref[1:2], sh_ref[0:1])
    sc = jnp.where(is_ctx, sc_ref[1:2], sc_ref[0:1])
    o_ref[...] = _norm_mod(src, g_ref[...], sh, sc).astype(o_ref.dtype)


def norm_mod_pair(x, ctx, g, sh2, sc2, dtype):
    n, d = x.shape
    nc = ctx.shape[0]
    tm = _tile(nc, 256)
    assert n % tm == 0
    nx = n // tm
    vec = pl.BlockSpec((1, d), lambda i: (0, 0))
    vec2 = pl.BlockSpec((2, d), lambda i: (0, 0))
    return pl.pallas_call(
        functools.partial(_norm_mod_pair_kernel, nx=nx),
        grid=(nx + nc // tm,),
        in_specs=[pl.BlockSpec((tm, d), lambda i: (jnp.minimum(i, nx - 1), 0)),
                  pl.BlockSpec((tm, d), lambda i: (jnp.maximum(i - nx, 0), 0)), vec, vec2, vec2],
        out_specs=pl.BlockSpec((tm, d), lambda i: (i, 0)),
        out_shape=jax.ShapeDtypeStruct((n + nc, d), dtype),
        compiler_params=_params("parallel"),
        name="norm_mod_pair",
    )(x, ctx, g, sh2, sc2)


def _rstd_kernel(x_ref, o_ref):
    x = x_ref[...]
    o_ref[...] = jnp.broadcast_to(lax.rsqrt(jnp.mean(x * x, axis=-1, keepdims=True) + NORM_EPS), o_ref.shape)


def row_rstd(x):
    n, d = x.shape
    tm = _tile(n, 256)
    return pl.pallas_call(
        _rstd_kernel,
        grid=(n // tm,),
        in_specs=[pl.BlockSpec((tm, d), lambda i: (i, 0))],
        out_specs=pl.BlockSpec((tm, LANES), lambda i: (i, 0)),
        out_shape=jax.ShapeDtypeStruct((n, LANES), F32),
        compiler_params=_params("parallel"),
        name="row_rstd",
    )(x)


def _pool_kernel(cur_ref, prev_ref, next_ref, rc_ref, rp_ref, rn_ref, gs_ref, sh_ref, w_ref, ps_ref, g1_ref, o_ref,
                 *, n, tm):
    g = pl.program_id(0)
    i = pl.program_id(1)
    last = pl.num_programs(1) - 1
    reps = cur_ref.shape[1] // LANES

    def modulated(x, r):
        return x * jnp.concatenate([r] * reps, axis=1) * gs_ref[...] + sh_ref[...]

    t = i * tm + lax.broadcasted_iota(I32, (tm, 1), 0)

    for gi, win in enumerate(POOL_WINDOWS):
        @pl.when(g == gi)
        def _(win=win):
            x_cur = cur_ref[...]
            cur = modulated(x_cur, rc_ref[...])
            prev = jnp.where(i == 0, 0.0, modulated(prev_ref[...], rp_ref[...]))
            nxt = jnp.where(i == last, 0.0, modulated(next_ref[...], rn_ref[...]))
            ext = jnp.concatenate([prev, cur, nxt], axis=0)
            half = win // 2
            s = ext
            step = 1
            while step < win:
                m = s.shape[0] - step
                s = s[:m] + s[step:step + m]
                step *= 2
            wsum = s[8 - half:8 - half + tm]
            cnt = jnp.minimum(t + half, n) - jnp.maximum(t - half, 0)
            dlt = (wsum / cnt.astype(F32) - cur).astype(BF16)
            y = jnp.dot(dlt, w_ref[0], preferred_element_type=F32) * ps_ref[...]
            o_ref[...] = x_cur + g1_ref[...] * y


def pool_mixer(x, rstd, gs, sh, w, pscale, g1):
    n, d = x.shape
    ng, dg, _ = w.shape
    tm = _tile(n, 512)
    nb8 = n // 8
    kern = functools.partial(_pool_kernel, n=n, tm=tm)
    col = pl.BlockSpec((1, dg), lambda g, i: (0, g))

    def halo(width, sel):
        before = pl.BlockSpec((8, width), lambda g, i: (jnp.maximum(i * (tm // 8) - 1, 0), sel(g)))
        after = pl.BlockSpec((8, width), lambda g, i: (jnp.minimum((i + 1) * (tm // 8), nb8 - 1), sel(g)))
        return pl.BlockSpec((tm, width), lambda g, i: (i, sel(g))), before, after

    return pl.pallas_call(
        kern,
        grid=(ng, n // tm),
        in_specs=[*halo(dg, lambda g: g), *halo(LANES, lambda g: 0), col, col,
                  pl.BlockSpec((1, dg, dg), lambda g, i: (g, 0, 0)), col, col],
        out_specs=pl.BlockSpec((tm, dg), lambda g, i: (i, g)),
        out_shape=jax.ShapeDtypeStruct((n, d), F32),
        compiler_params=_params("parallel", "parallel"),
        name="pool_mixer",
    )(x, x, x, rstd, rstd, rstd, gs, sh, w, pscale, g1)


def _mm_call(kern, a, w, extra, extra_specs, out_shape, out_specs, tm, tn, name, scratch=(), m=None):
    k = a.shape[1]
    m = a.shape[0] if m is None else m
    n = w.shape[1]
    return pl.pallas_call(
        kern,
        grid=(pl.cdiv(m, tm), n // tn),
        in_specs=[pl.BlockSpec((tm, k), lambda i, j: (i, 0)),
                  pl.BlockSpec((k, tn), lambda i, j: (0, j))] + list(extra_specs),
        out_specs=out_specs,
        out_shape=out_shape,
        scratch_shapes=list(scratch),
        compiler_params=_params("parallel", "arbitrary"),
        name=name,
    )(a, w, *extra)


def _mm_t_kernel(a_ref, wt_ref, o_ref):
    o_ref[...] = lax.dot_general(wt_ref[...], a_ref[...], (((1,), (1,)), ((), ())),
                                 preferred_element_type=F32).astype(o_ref.dtype)


def mm_transposed(a, wt, dtype=BF16):
    m, k = a.shape
    n = wt.shape[0]
    tm, tn = min(m, 1024), _tile(n, 1024)
    return pl.pallas_call(
        _mm_t_kernel,
        grid=(pl.cdiv(m, tm), n // tn),
        in_specs=[pl.BlockSpec((tm, k), lambda i, j: (i, 0)),
                  pl.BlockSpec((tn, k), lambda i, j: (j, 0))],
        out_specs=pl.BlockSpec((tn, tm), lambda i, j: (j, i)),
        out_shape=jax.ShapeDtypeStruct((n, m), dtype),
        compiler_params=_params("parallel", "arbitrary"),
        name="mm_transposed",
    )(a, wt)


def _mm_qk_kernel(a_ref, w_ref, g_ref, cos_ref, sin_ref, o_ref, *, scale):
    y = jnp.dot(a_ref[...], w_ref[...], preferred_element_type=F32)
    tn = y.shape[1]
    cos = cos_ref[...]
    sin = sin_ref[...]
    ga_cos, ga_sin = cos * g_ref[0:1], sin * g_ref[0:1]
    gb_cos, gb_sin = cos * g_ref[1:2], sin * g_ref[1:2]
    comp0 = lax.broadcasted_iota(I32, (1, HEAD_DIM), 1) < HEAD_DIM // 2
    for hd in range(tn // (2 * HEAD_DIM)):
        a = y[:, 2 * hd * HEAD_DIM:(2 * hd + 1) * HEAD_DIM]
        b = y[:, (2 * hd + 1) * HEAD_DIM:(2 * hd + 2) * HEAD_DIM]
        sq = a * a + b * b
        ssq0 = jnp.sum(jnp.where(comp0, sq, 0.0), axis=-1, keepdims=True)
        ssq1 = jnp.sum(jnp.where(comp0, 0.0, sq), axis=-1, keepdims=True)
        rstd = jnp.where(comp0, lax.rsqrt(ssq0 * (1.0 / HEAD_DIM) + NORM_EPS),
                         lax.rsqrt(ssq1 * (1.0 / HEAD_DIM) + NORM_EPS)) * scale
        o_ref[:, 2 * hd * HEAD_DIM:(2 * hd + 1) * HEAD_DIM] = ((a * ga_cos - b * gb_sin) * rstd).astype(o_ref.dtype)
        o_ref[:, (2 * hd + 1) * HEAD_DIM:(2 * hd + 2) * HEAD_DIM] = ((b * gb_cos + a * ga_sin) * rstd).astype(o_ref.dtype)


def mm_qk(a, w, gain, cos, sin, scale, m=None):
    m = a.shape[0] if m is None else m
    n = w.shape[1]
    tm, tn = min(m, 1024), _tile(n, 1024)
    kern = functools.partial(_mm_qk_kernel, scale=scale)
    tab = pl.BlockSpec((tm, HEAD_DIM), lambda i, j: (i, 0))
    return _mm_call(kern, a, w, (gain, cos, sin),
                    (pl.BlockSpec((2, HEAD_DIM), lambda i, j: (0, 0)), tab, tab),
                    jax.ShapeDtypeStruct((m, n), BF16),
                    pl.BlockSpec((tm, tn), lambda i, j: (i, j)), tm, tn, "mm_qk", m=m)


def _mm_gelu_kernel(a_ref, w_ref, o_ref, ssq_ref, *, nj_half):
    j = pl.program_id(1)
    y = jax.nn.gelu(jnp.dot(a_ref[...], w_ref[...], preferred_element_type=F32))
    o_ref[...] = y.astype(o_ref.dtype)

    @pl.when(j == nj_half)
    def _():
        ssq_ref[...] = jnp.zeros_like(ssq_ref)

    @pl.when(j >= nj_half)
    def _():
        ssq_ref[...] += jnp.sum(y * y, axis=-1, keepdims=True)


def mm_gelu(a, w):
    m, _ = a.shape
    n = w.shape[1]
    tm, tn = min(m, 1024), _tile(n // 2, 1024)
    kern = functools.partial(_mm_gelu_kernel, nj_half=(n // 2) // tn)
    return _mm_call(kern, a, w, (), (),
                    (jax.ShapeDtypeStruct((m, n), BF16), jax.ShapeDtypeStruct((m, LANES), F32)),
                    (pl.BlockSpec((tm, tn), lambda i, j: (i, j)),
                     pl.BlockSpec((tm, LANES), lambda i, j: (i, 0))), tm, tn, "mm_gelu")


def _mm_resid_kernel(a_ref, w_ref, x_ref, g_ref, o_ref):
    y = jnp.dot(a_ref[...], w_ref[...], preferred_element_type=F32)
    o_ref[...] = x_ref[...] + g_ref[...] * y


def mm_resid(a, w, x, g1):
    m, _ = a.shape
    n = w.shape[1]
    tm, tn = min(m, 1024), _tile(n, 1024)
    blk = pl.BlockSpec((tm, tn), lambda i, j: (i, j))
    return _mm_call(_mm_resid_kernel, a, w, (x, g1),
                    (blk, pl.BlockSpec((1, tn), lambda i, j: (0, j))),
                    jax.ShapeDtypeStruct((m, n), F32), blk, tm, tn, "mm_resid")


def _chunk_gate_kernel(u_ref, v_ref, ssq_ref, vg_ref, ws_ref, bs_ref, z_ref, *, width):
    tm = u_ref.shape[0]
    rstd = lax.rsqrt(ssq_ref[:, :1] * (1.0 / width) + NORM_EPS)

    def body(g, carry):
        col = pl.multiple_of(g * LANES, LANES)
        vg = vg_ref[:, pl.ds(col, LANES)]
        chunks = [slice(c * CHUNK, (c + 1) * CHUNK) for c in range(tm // CHUNK)]
        vn = jnp.concatenate([(v_ref[rows, pl.ds(col, LANES)].astype(F32) * rstd[rows] * vg).astype(BF16)
                              for rows in chunks], axis=1)
        sv = jnp.dot(ws_ref[g], vn, preferred_element_type=F32)
        bias = bs_ref[:, pl.ds(col, LANES)]
        for c, rows in enumerate(chunks):
            gate = sv[:, c * LANES:(c + 1) * LANES] + bias
            z_ref[rows, pl.ds(col, LANES)] = (u_ref[rows, pl.ds(col, LANES)].astype(F32) * gate).astype(BF16)
        return carry

    lax.fori_loop(0, width // LANES, body, 0, unroll=2)


def chunk_gate(uv, ssq, vg, ws, bs_full):
    n, w2 = uv.shape
    width = w2 // 2
    tm = _tile(n, 512)
    return pl.pallas_call(
        functools.partial(_chunk_gate_kernel, width=width),
        grid=(n // tm,),
        in_specs=[pl.BlockSpec((tm, width), lambda i: (i, 0)),
                  pl.BlockSpec((tm, width), lambda i: (i, 1)),
                  pl.BlockSpec((tm, LANES), lambda i: (i, 0)),
                  pl.BlockSpec((1, width), lambda i: (0, 0)),
                  pl.BlockSpec(ws.shape, lambda i: (0, 0, 0)),
                  pl.BlockSpec((CHUNK, width), lambda i: (0, 0))],
        out_specs=pl.BlockSpec((tm, width), lambda i: (i, 0)),
        out_shape=jax.ShapeDtypeStruct((n, width), BF16),
        compiler_params=_params("parallel"),
        name="chunk_gate",
    )(uv, uv, ssq, vg, ws, bs_full)


def _attn_kernel(lamv_ref, off_ref, q_ref, k_ref, vt_ref, sg_ref, o_ref, m_scr, l_scr, acc_scr, p_scr,
                 *, lam_init, tc, bounded):
    ki = pl.program_id(2)
    tk = k_ref.shape[0]
    nbuf = p_scr.shape[0]
    nchunk = q_ref.shape[0] // tc
    lane = lax.broadcasted_iota(I32, (1, 2 * HEAD_DIM), 1) % HEAD_DIM
    qcs = [q_ref[...] * jnp.where((lane < HEAD_DIM // 2) == (c == 0), 1.0, 0.0).astype(BF16) for c in range(2)]

    @pl.when(ki == 0)
    def _():
        l_scr[...] = jnp.zeros_like(l_scr)
        acc_scr[...] = jnp.zeros_like(acc_scr)
        if not bounded:
            m_scr[...] = jnp.full_like(m_scr, -jnp.inf)

    vt = vt_ref[...]
    kt = k_ref[...]
    scores = {}
    for c in range(2):
        for r in range(nchunk):
            scores[r, c] = lax.dot_general(kt, qcs[c][r * tc:(r + 1) * tc], (((1,), (1,)), ((), ())),
                                           preferred_element_type=F32)
    for r in range(nchunk):
        qs = slice(r * tc, (r + 1) * tc)
        for c in range(2):
            buf = (2 * r + c) % nbuf
            s = scores[r, c]
            if bounded:
                m_new = off_ref[:, :1]
            else:
                m_prev = m_scr[c, :, qs]
                m_new = jnp.maximum(m_prev, jnp.max(s, axis=0, keepdims=True))
            psum = jnp.zeros((16, tc), F32)
            for g in range(tk // 16):
                pg = jnp.exp2(s[16 * g:16 * (g + 1)] - m_new)
                psum = psum + pg
                p_scr[buf, 16 * g:16 * (g + 1), :] = pg.astype(BF16)
            lsum = jnp.sum(psum, axis=0, keepdims=True)
            pv = jnp.dot(vt, p_scr[buf], preferred_element_type=F32)
            if bounded:
                l_scr[c, :, qs] += lsum
                acc_scr[c, :, qs] += pv
            else:
                alpha = jnp.exp2(m_prev - m_new)
                l_scr[c, :, qs] = alpha * l_scr[c, :, qs] + lsum
                acc_scr[c, :, qs] = alpha * acc_scr[c, :, qs] + pv
                m_scr[c, :, qs] = m_new

    @pl.when(ki == pl.num_programs(2) - 1)
    def _():
        lv = lamv_ref[...]
        lam = (jnp.exp(jnp.sum(lv[0:1] * lv[1:2], axis=-1, keepdims=True))
               - jnp.exp(jnp.sum(lv[2:3] * lv[3:4], axis=-1, keepdims=True)) + lam_init)
        ot = acc_scr[0] / l_scr[0] - lam * (acc_scr[1] / l_scr[1])
        ot = ot * lax.rsqrt(jnp.mean(ot * ot, axis=0, keepdims=True) + NORM_EPS)
        o_ref[...] = (ot.T * (sg_ref[...] * (1.0 - lam_init))).astype(o_ref.dtype)


def _key_tile(nk, cap):
    best = LANES
    for t in range(LANES, cap + 1, LANES):
        if nk % t == 0:
            best = t
    return best


def diff_attention(q, k, vt, lamv, sub_g, lam_init, score_bound):
    n, d = q.shape
    nk = k.shape[0]
    hw = 2 * HEAD_DIM
    heads = d // hw
    tq = _tile(n, 2048)
    tk = _key_tile(nk, 1536)
    tc = min(tq, 256)

    safe = score_bound <= MAX_FIXED_OFFSET
    off = jnp.full((1, LANES), score_bound, F32)

    def call(bounded):
        kern = functools.partial(_attn_kernel, lam_init=lam_init, tc=tc, bounded=bounded)
        return pl.pallas_call(
            kern,
            grid=(heads, n // tq, nk // tk),
            in_specs=[pl.BlockSpec((4, HEAD_DIM), lambda h, i, j: (0, 0)),
                      pl.BlockSpec((1, LANES), lambda h, i, j: (0, 0)),
                      pl.BlockSpec((tq, hw), lambda h, i, j: (i, h)),
                      pl.BlockSpec((tk, hw), lambda h, i, j: (j, h)),
                      pl.BlockSpec((hw, tk), lambda h, i, j: (h, j)),
                      pl.BlockSpec((1, hw), lambda h, i, j: (0, 0))],
            out_specs=pl.BlockSpec((tq, hw), lambda h, i, j: (i, h)),
            out_shape=jax.ShapeDtypeStruct((n, d), BF16),
            scratch_shapes=[pltpu.VMEM((2, 1, tq), F32), pltpu.VMEM((2, 1, tq), F32),
                            pltpu.VMEM((2, hw, tq), F32), pltpu.VMEM((2 * (tq // tc), tk, tc), BF16)],
            compiler_params=_params("parallel", "parallel", "arbitrary"),
            name="diff_attention_bounded" if bounded else "diff_attention_online",
        )(lamv, off, q, k, vt, sub_g)

    return lax.cond(safe, lambda: call(True), lambda: call(False))


def _router_kernel(x_ref, g_ref, sh_ref, sc_ref, wrt_ref, h_ref, aff_ref):
    h = _norm_mod(x_ref[...], g_ref[...], sh_ref[...], sc_ref[...])
    half = h.shape[1] // 2
    bits = pltpu.bitcast(h.astype(BF16).astype(F32), U32)
    h_ref[...] = (bits[:, :half] >> 16) | (bits[:, half:] & jnp.uint32(0xFFFF0000))
    logits = lax.dot_general(wrt_ref[...], h, (((1,), (1,)), ((), ())),
                             preferred_element_type=F32, precision=lax.Precision.HIGHEST)
    ex = jnp.exp(logits - jnp.max(logits, axis=0, keepdims=True))
    aff_ref[...] = ex / jnp.sum(ex, axis=0, keepdims=True)


def router(x, g, sh, sc, w_r_t):
    n, d = x.shape
    e = w_r_t.shape[0]
    tm = _tile(n, 256)
    vec = pl.BlockSpec((1, d), lambda i: (0, 0))
    return pl.pallas_call(
        _router_kernel,
        grid=(n // tm,),
        in_specs=[pl.BlockSpec((tm, d), lambda i: (i, 0)), vec, vec, vec,
                  pl.BlockSpec((e, d), lambda i: (0, 0))],
        out_specs=(pl.BlockSpec((tm, d // 2), lambda i: (i, 0)), pl.BlockSpec((e, tm), lambda i: (0, i))),
        out_shape=(jax.ShapeDtypeStruct((n, d // 2), U32), jax.ShapeDtypeStruct((e, n), F32)),
        compiler_params=_params("parallel"),
        name="router",
    )(x, g, sh, sc, w_r_t)


def _split3(a):
    hi = a.astype(BF16)
    r1 = a - hi.astype(F32)
    mid = r1.astype(BF16)
    lo = (r1 - mid.astype(F32)).astype(BF16)
    return hi, mid, lo


def _topk_kernel(abt_ref, atb_ref, idx_ref, gate_ref, thr_scr, cut_scr, *, cap, cap_pad):
    ne, nb, _ = abt_ref.shape
    ntok = nb * LANES
    bits = pltpu.bitcast(abt_ref[...], I32)
    tok_bt = (lax.broadcasted_iota(I32, (1, nb, LANES), 1) * LANES
              + lax.broadcasted_iota(I32, (1, nb, LANES), 2))

    def count(ones):
        c = jnp.sum(ones, axis=2, keepdims=True)
        return jnp.sum(c, axis=1, keepdims=True)

    def thr_body(_, carry):
        lo, hi = carry
        mid = lo + ((hi - lo + 1) >> 1)
        ok = count(jnp.where(bits >= mid, 1, 0)) >= cap
        return jnp.where(ok, mid, lo), jnp.where(ok, hi, mid - 1)

    lo0 = jnp.zeros((ne, 1, 1), I32)
    hi0 = jnp.full((ne, 1, 1), 0x7F800000, I32)
    thr, _ = lax.fori_loop(0, 31, thr_body, (lo0, hi0))

    tok_eq = jnp.where(bits == thr, tok_bt, ntok)
    need = cap - count(jnp.where(bits > thr, 1, 0))

    def cut_body(_, carry):
        lo, hi = carry
        mid = (lo + hi) >> 1
        ok = count(jnp.where(tok_eq < mid, 1, 0)) >= need
        return jnp.where(ok, lo, mid), jnp.where(ok, mid, hi)

    _, cut = lax.fori_loop(0, int(math.log2(ntok)) + 1, cut_body,
                           (jnp.zeros((ne, 1, 1), I32), jnp.full((ne, 1, 1), ntok, I32)))
    thr_scr[...] = jnp.broadcast_to(thr, thr_scr.shape)
    cut_scr[...] = jnp.broadcast_to(cut, cut_scr.shape)

    r_i = lax.broadcasted_iota(I32, (LANES, LANES), 0)
    c_i = lax.broadcasted_iota(I32, (LANES, LANES), 1)
    tri_t = jnp.where(c_i <= r_i, 1.0, 0.0).astype(BF16)
    rb_i = lax.broadcasted_iota(I32, (nb, nb), 0)
    cb_i = lax.broadcasted_iota(I32, (nb, nb), 1)
    tri_b = jnp.where(cb_i <= rb_i, 1.0, 0.0).astype(BF16)
    tok_b = lax.broadcasted_iota(I32, (nb, LANES), 0) * LANES + lax.broadcasted_iota(I32, (nb, LANES), 1)
    tok_t = lax.broadcasted_iota(I32, (LANES, nb), 1) * LANES + lax.broadcasted_iota(I32, (LANES, nb), 0)
    slot = lax.broadcasted_iota(I32, (1, cap_pad), 1).astype(F32)
    blk_iota = lax.broadcasted_iota(I32, (nb, 1), 0).astype(F32)
    lane_iota = lax.broadcasted_iota(I32, (LANES, 1), 0).astype(F32)

    def select(b, tok, th, ct):
        return jnp.where(b > th, 1.0, jnp.where(b == th, jnp.where(tok < ct, 1.0, 0.0), 0.0))

    def per_expert(e, carry):
        th = thr_scr[e][:1, :1]
        ct = cut_scr[e][:1, :1]
        a_tb = atb_ref[e]
        sel_bt = select(pltpu.bitcast(abt_ref[e], I32), tok_b, th, ct)
        sel_tb = select(pltpu.bitcast(a_tb, I32), tok_t, th, ct)
        cw_t = jnp.dot(tri_t, sel_tb.astype(BF16), preferred_element_type=F32)
        tot = jnp.sum(sel_bt, axis=1, keepdims=True)
        incl = jnp.dot(tri_b, jnp.broadcast_to(tot, (nb, LANES)).astype(BF16),
                       preferred_element_type=F32)[:, :1]
        excl = incl - tot
        blk = jnp.sum(jnp.where(incl <= slot, 1.0, 0.0), axis=0, keepdims=True)
        onehot = jnp.where(blk_iota == blk, 1.0, 0.0)
        oh16 = onehot.astype(BF16)
        g_t = jnp.dot(cw_t.astype(BF16), oh16, preferred_element_type=F32)
        base = jnp.sum(onehot * excl, axis=0, keepdims=True)
        tl = jnp.sum(jnp.where(g_t + base <= slot, 1.0, 0.0), axis=0, keepdims=True)
        idx_ref[pl.ds(e, 1), :] = (blk * LANES + tl).astype(I32)
        hi, mid, lo = _split3(a_tb)
        rows = (jnp.dot(hi, oh16, preferred_element_type=F32)
                + jnp.dot(mid, oh16, preferred_element_type=F32)
                + jnp.dot(lo, oh16, preferred_element_type=F32))
        gate_ref[pl.ds(e, 1), :] = jnp.sum(jnp.where(lane_iota == tl, rows, 0.0), axis=0, keepdims=True)
        return carry

    lax.fori_loop(0, ne, per_expert, 0)


def topk_select(aff_t, cap):
    ne, n = aff_t.shape
    nb = max(LANES, -(-n // LANES))
    nb = -(-nb // LANES) * LANES
    cap_pad = max(LANES, cap)
    padded = jnp.pad(aff_t, ((0, 0), (0, nb * LANES - n)), constant_values=-1.0)
    a_bt = padded.reshape(ne, nb, LANES)
    a_tb = jnp.swapaxes(a_bt, 1, 2)
    kern = functools.partial(_topk_kernel, cap=cap, cap_pad=cap_pad)
    idx, gate = pl.pallas_call(
        kern,
        out_shape=(jax.ShapeDtypeStruct((ne, cap_pad), I32), jax.ShapeDtypeStruct((ne, cap_pad), F32)),
        scratch_shapes=[pltpu.VMEM((ne, 8, LANES), I32), pltpu.VMEM((ne, 8, LANES), I32)],
        compiler_params=pltpu.CompilerParams(vmem_limit_bytes=VMEM_LIMIT),
        name="topk_select",
    )(a_bt, a_tb)
    return idx[:, :cap], gate[:, :cap]


def _moe_kernel(idx_ref, gate_ref, g2_ref, h_hbm, wg_hbm, wu_hbm, wd_hbm, xin_hbm, out_hbm,
                xs, ab, y_scr, wg_f32, wu_f32, wd_f32, wgu, wd, sem_x, sem_a, sem_o, sem_w, *, tc, layer):
    del xin_hbm
    nct = pl.num_programs(1)
    ct = pl.program_id(1)
    s = pl.program_id(0) * nct + ct
    last = pl.num_programs(0) * nct - 1
    slot = s % 2
    base = s * tc
    nxt = jnp.minimum(s + 1, last) * tc

    def x_copy(b, r, sl):
        return pltpu.make_async_copy(h_hbm.at[pl.ds(idx_ref[b + r], 1)], xs.at[sl, pl.ds(r, 1)], sem_x.at[sl])

    def a_copy(r):
        return pltpu.make_async_copy(out_hbm.at[pl.ds(idx_ref[base + r], 1)], ab.at[slot, pl.ds(r, 1)],
                                     sem_a.at[slot])

    def o_copy(b, r, sl):
        return pltpu.make_async_copy(ab.at[sl, pl.ds(r, 1)], out_hbm.at[pl.ds(idx_ref[b + r], 1)], sem_o.at[sl])

    def rows(fn):
        def body(r, c):
            fn(r)
            return c
        lax.fori_loop(0, tc, body, 0, unroll=8)

    ne = pl.num_programs(0)
    expert = pl.program_id(0)

    def w_copies(e):
        return (pltpu.make_async_copy(wg_hbm.at[layer, e], wg_f32, sem_w.at[0]),
                pltpu.make_async_copy(wu_hbm.at[layer, e], wu_f32, sem_w.at[1]),
                pltpu.make_async_copy(wd_hbm.at[layer, e], wd_f32, sem_w.at[2]))

    @pl.when(s == 0)
    def _():
        for cp in w_copies(0):
            cp.start()
        rows(lambda r: x_copy(0, r, 0).start())

    @pl.when(ct == 0)
    def _():
        for cp in w_copies(expert):
            cp.wait()
        f = wd.shape[0]
        wgu[:, :f] = wg_f32[...].astype(BF16)
        wgu[:, f:] = wu_f32[...].astype(BF16)
        wd[...] = wd_f32[...].astype(BF16)

    @pl.when(jnp.logical_and(ct == 0, expert + 1 < ne))
    def _():
        for cp in w_copies(expert + 1):
            cp.start()

    rows(lambda r: x_copy(base, r, slot).wait())

    @pl.when(jnp.logical_and(ct == 0, s > 0))
    def _():
        rows(lambda r: o_copy(base - tc, r, 1 - slot).wait())

    for r in range(tc):
        x_copy(nxt, r, 1 - slot).start()
        a_copy(r).start()
    xu = xs[slot]
    x_lo = pltpu.bitcast(xu << 16, F32).astype(BF16)
    x_hi = pltpu.bitcast(xu & jnp.uint32(0xFFFF0000), F32).astype(BF16)
    half = xu.shape[1]
    ab2 = (jnp.dot(x_lo, wgu[:half], preferred_element_type=F32)
           + jnp.dot(x_hi, wgu[half:], preferred_element_type=F32))
    f = ab2.shape[1] // 2
    a, b = ab2[:, :f], ab2[:, f:]
    gate = jnp.concatenate([gate_ref[0]] * (f // LANES), axis=1)
    hm = (a * jax.nn.sigmoid(a) * b * gate).astype(BF16)
    y_scr[...] = jnp.dot(hm, wd[...], preferred_element_type=F32) * g2_ref[...]

    rows(lambda r: a_copy(r).wait())

    @pl.when(ct > 0)
    def _():
        rows(lambda r: o_copy(base - tc, r, 1 - slot).wait())

    ab[slot] = ab[slot] + y_scr[...]
    rows(lambda r: o_copy(base, r, slot).start())

    @pl.when(s == last)
    def _():
        rows(lambda r: o_copy(base, r, slot).wait())
        rows(lambda r: x_copy(nxt, r, 1 - slot).wait())


def moe_apply(h, x, idx, gate, w_gate, w_up, w_down, layer, g2):
    n, d = x.shape
    ne, cap = idx.shape
    f = w_down.shape[2]
    assert f % LANES == 0
    tc = _tile(cap, 256)
    gate_b = jnp.broadcast_to(gate[:, :, None], (ne, cap, LANES))
    kern = functools.partial(_moe_kernel, tc=tc, layer=layer)
    hbm = pl.BlockSpec(memory_space=pl.ANY)
    grid_spec = pltpu.PrefetchScalarGridSpec(
        num_scalar_prefetch=1,
        grid=(ne, cap // tc),
        in_specs=[pl.BlockSpec((1, tc, LANES), lambda e, c, idx: (e, c, 0)),
                  pl.BlockSpec((1, d), lambda e, c, idx: (0, 0)),
                  hbm, hbm, hbm, hbm, hbm],
        out_specs=hbm,
        scratch_shapes=[pltpu.VMEM((2, tc, d // 2), U32), pltpu.VMEM((2, tc, d), F32), pltpu.VMEM((tc, d), F32),
                        pltpu.VMEM((d, f), F32), pltpu.VMEM((d, f), F32), pltpu.VMEM((f, d), F32),
                        pltpu.VMEM((d, 2 * f), BF16), pltpu.VMEM((f, d), BF16),
                        pltpu.SemaphoreType.DMA((2,)), pltpu.SemaphoreType.DMA((2,)),
                        pltpu.SemaphoreType.DMA((2,)), pltpu.SemaphoreType.DMA((3,))],
    )
    return pl.pallas_call(
        kern,
        grid_spec=grid_spec,
        out_shape=jax.ShapeDtypeStruct((n, d), F32),
        input_output_aliases={7: 0},
        compiler_params=_params("arbitrary", "arbitrary"),
        name="moe_apply",
    )(idx.reshape(-1), gate_b, g2, h, w_gate, w_up, w_down, x)


def moe_layer(x, g, sh, sc, g2, w_r, w_gate, w_up, w_down, layer):
    n = x.shape[0]
    ne = w_r.shape[1]
    h, aff_t = router(x, g, sh, sc, w_r.T)
    idx, gate = topk_select(aff_t, EC_CAPACITY_FACTOR * n // ne)
    return moe_apply(h, x, idx, gate, w_gate, w_up, w_down, layer, g2)


def _rope_perm(width):
    quarter = HEAD_DIM // 4
    starts = jnp.array([0, 2, 4, 6, 1, 3, 5, 7]) * quarter
    blk = (starts[:, None] + jnp.arange(quarter)[None, :]).reshape(-1)
    return (jnp.arange(0, width, 2 * HEAD_DIM)[:, None] + blk[None, :]).reshape(-1)


def _permute_heads(w):
    rows, width = w.shape
    quarter = HEAD_DIM // 4
    w6 = w.reshape(rows, width // (2 * HEAD_DIM), 2, 2, 2, quarter)
    return jnp.transpose(w6, (0, 1, 4, 2, 3, 5)).reshape(rows, width)


def _rope_tables(n, n_ctx):
    quarter = HEAD_DIM // 4
    t = jnp.arange(n, dtype=I32)
    freqs = ROPE_THETA ** (-jnp.arange(quarter, dtype=F32) / quarter)
    ang_r = (t // GRID_W).astype(F32)[:, None] * freqs
    ang_c = (t % GRID_W).astype(F32)[:, None] * freqs
    cos = jnp.concatenate([jnp.cos(ang_r), jnp.cos(ang_c)] * 2, axis=1)
    sin = jnp.concatenate([jnp.sin(ang_r), jnp.sin(ang_c)] * 2, axis=1)
    cos = jnp.concatenate([cos, jnp.ones((n_ctx, HEAD_DIM), F32)], axis=0)
    sin = jnp.concatenate([sin, jnp.zeros((n_ctx, HEAD_DIM), F32)], axis=0)
    return cos, sin


def kernel(x, c, ctx, c_ctx, ada_w, ada_b, norm1_g, norm2_g, router_w, exp_w_gate, exp_w_up, exp_w_down,
           pool_w, pool_scale, gm_w_in, gm_v_g, gm_w_s, gm_b_s, gm_w_out, da_w_q, da_w_k, da_w_v, da_w_o,
           da_q_g, da_k_g, da_lam_q1, da_lam_k1, da_lam_q2, da_lam_k2, da_sub_g):
    bsz, n, d = x.shape
    assert bsz == 1 and c.shape[0] == 1
    n_ctx = ctx.shape[1]
    depth = ada_w.shape[0]
    xs, cs = x[0], ctx[0]

    s8 = jnp.concatenate([c, c_ctx[None], jnp.zeros((6, d), F32)], axis=0)
    mod = adaln(s8, ada_w, ada_b)

    def mods(i, row):
        return [mod[i, row:row + 1, k * d:(k + 1) * d] for k in range(6)]

    def row(v):
        return v.reshape(1, -1)

    for i in range(depth):
        kind, slot = i % N_MIXERS, i // N_MIXERS
        keep_ctx = any(j % N_MIXERS == 2 for j in range(i + 1, depth))
        sh1, sc1, g1, sh2, sc2, g2 = mods(i, 0)
        csh1, csc1, cg1, csh2, csc2, cg2 = mods(i, 1)
        n1, n2 = row(norm1_g[i]), row(norm2_g[i])
        streams = [(xs, sh1, sc1, g1)] + ([(cs, csh1, csc1, cg1)] if keep_ctx else [])

        if kind == 0:
            w = pool_w[slot].astype(BF16)
            outs = [pool_mixer(s, row_rstd(s), n1 * (1.0 + sc), sh, w, row(pool_scale[slot]), g)
                    for s, sh, sc, g in streams]
        elif kind == 1:
            w_in = gm_w_in[slot].astype(BF16)
            w_out = gm_w_out[slot].astype(BF16)
            w_s = gm_w_s[slot].astype(BF16)
            width = w_out.shape[0]
            bs_full = jnp.repeat(gm_b_s[slot].T, width // gm_w_s.shape[1], axis=1)
            outs = []
            for s, sh, sc, g in streams:
                uv, ssq = mm_gelu(norm_mod(s, n1, sh, sc, BF16), w_in)
                z = chunk_gate(uv, ssq, row(gm_v_g[slot]), w_s, bs_full)
                outs.append(mm_resid(z, w_out, s, g))
        else:
            assert not keep_ctx
            lam_init = 0.8 - 0.6 * math.exp(-0.3 * i)
            h_all = norm_mod_pair(xs, cs, n1, jnp.concatenate([sh1, csh1]), jnp.concatenate([sc1, csc1]), BF16)
            cos, sin = _rope_tables(n, n_ctx)
            gperm = (_rope_perm(2 * HEAD_DIM) % HEAD_DIM).reshape(2, HEAD_DIM)
            q_scale = HEAD_DIM ** -0.5 * math.log2(math.e)
            q = mm_qk(h_all, _permute_heads(da_w_q[slot]).astype(BF16), da_q_g[slot][gperm], cos, sin, q_scale, m=n)
            k = mm_qk(h_all, _permute_heads(da_w_k[slot]).astype(BF16), da_k_g[slot][gperm], cos, sin, 1.0)
            vt = mm_transposed(h_all, da_w_v[slot].T.astype(BF16))
            lamv = jnp.stack([da_lam_q1[slot], da_lam_k1[slot], da_lam_q2[slot], da_lam_k2[slot]])
            score_bound = (HEAD_DIM * q_scale * BOUND_SLACK) * jnp.max(jnp.abs(da_q_g[slot])) * jnp.max(jnp.abs(da_k_g[slot]))
            o = diff_attention(q, k, vt, lamv, row(da_sub_g[slot]), lam_init, score_bound)
            outs = [mm_resid(o, da_w_o[slot].astype(BF16), xs, g1)]

        xs = moe_layer(outs[0], n2, sh2, sc2, g2, router_w[i], exp_w_gate, exp_w_up, exp_w_down, i)
        if keep_ctx:
            cs = moe_layer(outs[1], n2, csh2, csc2, cg2, router_w[i], exp_w_gate, exp_w_up, exp_w_down, i)
    return xs[None]
```

```python
import functools
import math

import jax
import jax.numpy as jnp
from jax import lax
from jax.experimental import pallas as pl
from jax.experimental.pallas import tpu as pltpu

F32 = jnp.float32
BF16 = jnp.bfloat16
I32 = jnp.int32
U32 = jnp.uint32

NORM_EPS = 1e-6
LANES = 128
GRID_W = 64
CHUNK = 128
POOL_WINDOWS = (2, 4, 8, 16)
HEAD_DIM = 128
ROPE_THETA = 10000.0
EC_CAPACITY_FACTOR = 2
N_MIXERS = 3
VMEM_LIMIT = 56 * 1024 * 1024
BOUND_SLACK = 1.01
MAX_FIXED_OFFSET = 60.0


def _params(*sem):
    return pltpu.CompilerParams(dimension_semantics=sem, vmem_limit_bytes=VMEM_LIMIT)


def _tile(n, t):
    t = min(n, t)
    assert n % t == 0, (n, t)
    return t


def _norm_mod(x, g, sh, sc):
    ms = jnp.mean(x * x, axis=-1, keepdims=True)
    return (x * lax.rsqrt(ms + NORM_EPS) * g) * (1.0 + sc) + sh


def _adaln_kernel(s_ref, w_ref, b_ref, o_ref):
    s = s_ref[...]
    s = s * jax.nn.sigmoid(s)
    o_ref[0] = jnp.dot(s, w_ref[0], preferred_element_type=F32) + b_ref[0]


def adaln(s8, ada_w, ada_b):
    depth, d, n6 = ada_w.shape
    tn = _tile(n6, 1024)
    return pl.pallas_call(
        _adaln_kernel,
        grid=(depth, n6 // tn),
        in_specs=[pl.BlockSpec((8, d), lambda l, j: (0, 0)),
                  pl.BlockSpec((1, d, tn), lambda l, j: (l, 0, j)),
                  pl.BlockSpec((1, 1, tn), lambda l, j: (l, 0, j))],
        out_specs=pl.BlockSpec((1, 8, tn), lambda l, j: (l, 0, j)),
        out_shape=jax.ShapeDtypeStruct((depth, 8, n6), F32),
        compiler_params=_params("parallel", "parallel"),
        name="adaln",
    )(s8, ada_w, ada_b.reshape(depth, 1, n6))


def _norm_mod_kernel(x_ref, g_ref, sh_ref, sc_ref, o_ref):
    o_ref[...] = _norm_mod(x_ref[...], g_ref[...], sh_ref[...], sc_ref[...]).astype(o_ref.dtype)


def norm_mod(x, g, sh, sc, dtype):
    n, d = x.shape
    tm = _tile(n, 256)
    vec = pl.BlockSpec((1, d), lambda i: (0, 0))
    return pl.pallas_call(
        _norm_mod_kernel,
        grid=(n // tm,),
        in_specs=[pl.BlockSpec((tm, d), lambda i: (i, 0)), vec, vec, vec],
        out_specs=pl.BlockSpec((tm, d), lambda i: (i, 0)),
        out_shape=jax.ShapeDtypeStruct((n, d), dtype),
        compiler_params=_params("parallel"),
        name="norm_mod",
    )(x, g, sh, sc)


def _norm_mod_pair_kernel(x_ref, c_ref, g_ref, sh_ref, sc_ref, o_ref, *, nx):
    is_ctx = pl.program_id(0) >= nx
    src = jnp.where(is_ctx, c_ref[...], x_ref[...])
    sh = jnp.where(is_ctx, sh_ref[1:2], sh_ref[0:1])
    sc = jnp.where(is_ctx, sc_ref[1:2], sc_ref[0:1])
    o_ref[...] = _norm_mod(src, g_ref[...], sh, sc).astype(o_ref.dtype)


def norm_mod_pair(x, ctx, g, sh2, sc2, dtype):
    n, d = x.shape
    nc = ctx.shape[0]
    tm = _tile(nc, 256)
    assert n % tm == 0
    nx = n // tm
    vec = pl.BlockSpec((1, d), lambda i: (0, 0))
    vec2 = pl.BlockSpec((2, d), lambda i: (0, 0))
    return pl.pallas_call(
        functools.partial(_norm_mod_pair_kernel, nx=nx),
        grid=(nx + nc // tm,),
        in_specs=[pl.BlockSpec((tm, d), lambda i: (jnp.minimum(i, nx - 1), 0)),
                  pl.BlockSpec((tm, d), lambda i: (jnp.maximum(i - nx, 0), 0)), vec, vec2, vec2],
        out_specs=pl.BlockSpec((tm, d), lambda i: (i, 0)),
        out_shape=jax.ShapeDtypeStruct((n + nc, d), dtype),
        compiler_params=_params("parallel"),
        name="norm_mod_pair",
    )(x, ctx, g, sh2, sc2)


def _rstd_kernel(x_ref, o_ref):
    x = x_ref[...]
    o_ref[...] = jnp.broadcast_to(lax.rsqrt(jnp.mean(x * x, axis=-1, keepdims=True) + NORM_EPS), o_ref.shape)


def row_rstd(x):
    n, d = x.shape
    tm = _tile(n, 256)
    return pl.pallas_call(
        _rstd_kernel,
        grid=(n // tm,),
        in_specs=[pl.BlockSpec((tm, d), lambda i: (i, 0))],
        out_specs=pl.BlockSpec((tm, LANES), lambda i: (i, 0)),
        out_shape=jax.ShapeDtypeStruct((n, LANES), F32),
        compiler_params=_params("parallel"),
        name="row_rstd",
    )(x)


def _pool_kernel(cur_ref, prev_ref, next_ref, rc_ref, rp_ref, rn_ref, gs_ref, sh_ref, w_ref, ps_ref, g1_ref, o_ref,
                 *, n, tm):
    g = pl.program_id(0)
    i = pl.program_id(1)
    last = pl.num_programs(1) - 1
    reps = cur_ref.shape[1] // LANES

    def modulated(x, r):
        return x * jnp.concatenate([r] * reps, axis=1) * gs_ref[...] + sh_ref[...]

    t = i * tm + lax.broadcasted_iota(I32, (tm, 1), 0)

    for gi, win in enumerate(POOL_WINDOWS):
        @pl.when(g == gi)
        def _(win=win):
            x_cur = cur_ref[...]
            cur = modulated(x_cur, rc_ref[...])
            prev = jnp.where(i == 0, 0.0, modulated(prev_ref[...], rp_ref[...]))
            nxt = jnp.where(i == last, 0.0, modulated(next_ref[...], rn_ref[...]))
            ext = jnp.concatenate([prev, cur, nxt], axis=0)
            half = win // 2
            s = ext
            step = 1
            while step < win:
                m = s.shape[0] - step
                s = s[:m] + s[step:step + m]
                step *= 2
            wsum = s[8 - half:8 - half + tm]
            cnt = jnp.minimum(t + half, n) - jnp.maximum(t - half, 0)
            dlt = (wsum / cnt.astype(F32) - cur).astype(BF16)
            y = jnp.dot(dlt, w_ref[0], preferred_element_type=F32) * ps_ref[...]
            o_ref[...] = x_cur + g1_ref[...] * y


def pool_mixer(x, rstd, gs, sh, w, pscale, g1):
    n, d = x.shape
    ng, dg, _ = w.shape
    tm = _tile(n, 512)
    nb8 = n // 8
    kern = functools.partial(_pool_kernel, n=n, tm=tm)
    col = pl.BlockSpec((1, dg), lambda g, i: (0, g))

    def halo(width, sel):
        before = pl.BlockSpec((8, width), lambda g, i: (jnp.maximum(i * (tm // 8) - 1, 0), sel(g)))
        after = pl.BlockSpec((8, width), lambda g, i: (jnp.minimum((i + 1) * (tm // 8), nb8 - 1), sel(g)))
        return pl.BlockSpec((tm, width), lambda g, i: (i, sel(g))), before, after

    return pl.pallas_call(
        kern,
        grid=(ng, n // tm),
        in_specs=[*halo(dg, lambda g: g), *halo(LANES, lambda g: 0), col, col,
                  pl.BlockSpec((1, dg, dg), lambda g, i: (g, 0, 0)), col, col],
        out_specs=pl.BlockSpec((tm, dg), lambda g, i: (i, g)),
        out_shape=jax.ShapeDtypeStruct((n, d), F32),
        compiler_params=_params("parallel", "parallel"),
        name="pool_mixer",
    )(x, x, x, rstd, rstd, rstd, gs, sh, w, pscale, g1)


def _mm_call(kern, a, w, extra, extra_specs, out_shape, out_specs, tm, tn, name, scratch=(), m=None):
    k = a.shape[1]
    m = a.shape[0] if m is None else m
    n = w.shape[1]
    return pl.pallas_call(
        kern,
        grid=(pl.cdiv(m, tm), n // tn),
        in_specs=[pl.BlockSpec((tm, k), lambda i, j: (i, 0)),
                  pl.BlockSpec((k, tn), lambda i, j: (0, j))] + list(extra_specs),
        out_specs=out_specs,
        out_shape=out_shape,
        scratch_shapes=list(scratch),
        compiler_params=_params("parallel", "arbitrary"),
        name=name,
    )(a, w, *extra)


def _mm_t_kernel(a_ref, wt_ref, o_ref):
    o_ref[...] = lax.dot_general(wt_ref[...], a_ref[...], (((1,), (1,)), ((), ())),
                                 preferred_element_type=F32).astype(o_ref.dtype)


def mm_transposed(a, wt, dtype=BF16):
    m, k = a.shape
    n = wt.shape[0]
    tm, tn = min(m, 1024), _tile(n, 1024)
    return pl.pallas_call(
        _mm_t_kernel,
        grid=(pl.cdiv(m, tm), n // tn),
        in_specs=[pl.BlockSpec((tm, k), lambda i, j: (i, 0)),
                  pl.BlockSpec((tn, k), lambda i, j: (j, 0))],
        out_specs=pl.BlockSpec((tn, tm), lambda i, j: (j, i)),
        out_shape=jax.ShapeDtypeStruct((n, m), dtype),
        compiler_params=_params("parallel", "arbitrary"),
        name="mm_transposed",
    )(a, wt)


def _mm_qk_kernel(a_ref, w_ref, g_ref, cos_ref, sin_ref, o_ref, *, scale):
    y = jnp.dot(a_ref[...], w_ref[...], preferred_element_type=F32)
    tn = y.shape[1]
    cos = cos_ref[...]
    sin = sin_ref[...]
    ga_cos, ga_sin = cos * g_ref[0:1], sin * g_ref[0:1]
    gb_cos, gb_sin = cos * g_ref[1:2], sin * g_ref[1:2]
    comp0 = lax.broadcasted_iota(I32, (1, HEAD_DIM), 1) < HEAD_DIM // 2
    for hd in range(tn // (2 * HEAD_DIM)):
        a = y[:, 2 * hd * HEAD_DIM:(2 * hd + 1) * HEAD_DIM]
        b = y[:, (2 * hd + 1) * HEAD_DIM:(2 * hd + 2) * HEAD_DIM]
        sq = a * a + b * b
        ssq0 = jnp.sum(jnp.where(comp0, sq, 0.0), axis=-1, keepdims=True)
        ssq1 = jnp.sum(jnp.where(comp0, 0.0, sq), axis=-1, keepdims=True)
        rstd = jnp.where(comp0, lax.rsqrt(ssq0 * (1.0 / HEAD_DIM) + NORM_EPS),
                         lax.rsqrt(ssq1 * (1.0 / HEAD_DIM) + NORM_EPS)) * scale
        o_ref[:, 2 * hd * HEAD_DIM:(2 * hd + 1) * HEAD_DIM] = ((a * ga_cos - b * gb_sin) * rstd).astype(o_ref.dtype)
        o_ref[:, (2 * hd + 1) * HEAD_DIM:(2 * hd + 2) * HEAD_DIM] = ((b * gb_cos + a * ga_sin) * rstd).astype(o_ref.dtype)


def mm_qk(a, w, gain, cos, sin, scale, m=None):
    m = a.shape[0] if m is None else m
    n = w.shape[1]
    tm, tn = min(m, 1024), _tile(n, 1024)
    kern = functools.partial(_mm_qk_kernel, scale=scale)
    tab = pl.BlockSpec((tm, HEAD_DIM), lambda i, j: (i, 0))
    return _mm_call(kern, a, w, (gain, cos, sin),
                    (pl.BlockSpec((2, HEAD_DIM), lambda i, j: (0, 0)), tab, tab),
                    jax.ShapeDtypeStruct((m, n), BF16),
                    pl.BlockSpec((tm, tn), lambda i, j: (i, j)), tm, tn, "mm_qk", m=m)


def _mm_gelu_kernel(a_ref, w_ref, o_ref, ssq_ref, *, nj_half):
    j = pl.program_id(1)
    y = jax.nn.gelu(jnp.dot(a_ref[...], w_ref[...], preferred_element_type=F32))
    o_ref[...] = y.astype(o_ref.dtype)

    @pl.when(j == nj_half)
    def _():
        ssq_ref[...] = jnp.zeros_like(ssq_ref)

    @pl.when(j >= nj_half)
    def _():
        ssq_ref[...] += jnp.sum(y * y, axis=-1, keepdims=True)


def mm_gelu(a, w):
    m, _ = a.shape
    n = w.shape[1]
    tm, tn = min(m, 1024), _tile(n // 2, 1024)
    kern = functools.partial(_mm_gelu_kernel, nj_half=(n // 2) // tn)
    return _mm_call(kern, a, w, (), (),
                    (jax.ShapeDtypeStruct((m, n), BF16), jax.ShapeDtypeStruct((m, LANES), F32)),
                    (pl.BlockSpec((tm, tn), lambda i, j: (i, j)),
                     pl.BlockSpec((tm, LANES), lambda i, j: (i, 0))), tm, tn, "mm_gelu")


def _mm_resid_kernel(a_ref, w_ref, x_ref, g_ref, o_ref):
    y = jnp.dot(a_ref[...], w_ref[...], preferred_element_type=F32)
    o_ref[...] = x_ref[...] + g_ref[...] * y


def mm_resid(a, w, x, g1):
    m, _ = a.shape
    n = w.shape[1]
    tm, tn = min(m, 1024), _tile(n, 1024)
    blk = pl.BlockSpec((tm, tn), lambda i, j: (i, j))
    return _mm_call(_mm_resid_kernel, a, w, (x, g1),
                    (blk, pl.BlockSpec((1, tn), lambda i, j: (0, j))),
                    jax.ShapeDtypeStruct((m, n), F32), blk, tm, tn, "mm_resid")


def _chunk_gate_kernel(u_ref, v_ref, ssq_ref, vg_ref, ws_ref, bs_ref, z_ref, *, width):
    tm = u_ref.shape[0]
    rstd = lax.rsqrt(ssq_ref[:, :1] * (1.0 / width) + NORM_EPS)

    def body(g, carry):
        col = pl.multiple_of(g * LANES, LANES)
        vg = vg_ref[:, pl.ds(col, LANES)]
        chunks = [slice(c * CHUNK, (c + 1) * CHUNK) for c in range(tm // CHUNK)]
        vn = jnp.concatenate([(v_ref[rows, pl.ds(col, LANES)].astype(F32) * rstd[rows] * vg).astype(BF16)
                              for rows in chunks], axis=1)
        sv = jnp.dot(ws_ref[g], vn, preferred_element_type=F32)
        bias = bs_ref[:, pl.ds(col, LANES)]
        for c, rows in enumerate(chunks):
            gate = sv[:, c * LANES:(c + 1) * LANES] + bias
            z_ref[rows, pl.ds(col, LANES)] = (u_ref[rows, pl.ds(col, LANES)].astype(F32) * gate).astype(BF16)
        return carry

    lax.fori_loop(0, width // LANES, body, 0, unroll=2)


def chunk_gate(uv, ssq, vg, ws, bs_full):
    n, w2 = uv.shape
    width = w2 // 2
    tm = _tile(n, 512)
    return pl.pallas_call(
        functools.partial(_chunk_gate_kernel, width=width),
        grid=(n // tm,),
        in_specs=[pl.BlockSpec((tm, width), lambda i: (i, 0)),
                  pl.BlockSpec((tm, width), lambda i: (i, 1)),
                  pl.BlockSpec((tm, LANES), lambda i: (i, 0)),
                  pl.BlockSpec((1, width), lambda i: (0, 0)),
                  pl.BlockSpec(ws.shape, lambda i: (0, 0, 0)),
                  pl.BlockSpec((CHUNK, width), lambda i: (0, 0))],
        out_specs=pl.BlockSpec((tm, width), lambda i: (i, 0)),
        out_shape=jax.ShapeDtypeStruct((n, width), BF16),
        compiler_params=_params("parallel"),
        name="chunk_gate",
    )(uv, uv, ssq, vg, ws, bs_full)


def _attn_kernel(lamv_ref, off_ref, q_ref, k_ref, vt_ref, sg_ref, o_ref, m_scr, l_scr, acc_scr, p_scr,
                 *, lam_init, tc, bounded):
    ki = pl.program_id(2)
    tk = k_ref.shape[0]
    nbuf = p_scr.shape[0]
    nchunk = q_ref.shape[0] // tc
    lane = lax.broadcasted_iota(I32, (1, 2 * HEAD_DIM), 1) % HEAD_DIM
    qcs = [q_ref[...] * jnp.where((lane < HEAD_DIM // 2) == (c == 0), 1.0, 0.0).astype(BF16) for c in range(2)]

    @pl.when(ki == 0)
    def _():
        l_scr[...] = jnp.zeros_like(l_scr)
        acc_scr[...] = jnp.zeros_like(acc_scr)
        if not bounded:
            m_scr[...] = jnp.full_like(m_scr, -jnp.inf)

    vt = vt_ref[...]
    kt = k_ref[...]
    scores = {}
    for c in range(2):
        for r in range(nchunk):
            scores[r, c] = lax.dot_general(kt, qcs[c][r * tc:(r + 1) * tc], (((1,), (1,)), ((), ())),
                                           preferred_element_type=F32)
    for r in range(nchunk):
        qs = slice(r * tc, (r + 1) * tc)
        for c in range(2):
            buf = (2 * r + c) % nbuf
            s = scores[r, c]
            if bounded:
                m_new = off_ref[:, :1]
            else:
                m_prev = m_scr[c, :, qs]
                m_new = jnp.maximum(m_prev, jnp.max(s, axis=0, keepdims=True))
            psum = jnp.zeros((16, tc), F32)
            for g in range(tk // 16):
                pg = jnp.exp2(s[16 * g:16 * (g + 1)] - m_new)
                psum = psum + pg
                p_scr[buf, 16 * g:16 * (g + 1), :] = pg.astype(BF16)
            lsum = jnp.sum(psum, axis=0, keepdims=True)
            pv = jnp.dot(vt, p_scr[buf], preferred_element_type=F32)
            if bounded:
                l_scr[c, :, qs] += lsum
                acc_scr[c, :, qs] += pv
            else:
                alpha = jnp.exp2(m_prev - m_new)
                l_scr[c, :, qs] = alpha * l_scr[c, :, qs] + lsum
                acc_scr[c, :, qs] = alpha * acc_scr[c, :, qs] + pv
                m_scr[c, :, qs] = m_new

    @pl.when(ki == pl.num_programs(2) - 1)
    def _():
        lv = lamv_ref[...]
        lam = (jnp.exp(jnp.sum(lv[0:1] * lv[1:2], axis=-1, keepdims=True))
               - jnp.exp(jnp.sum(lv[2:3] * lv[3:4], axis=-1, keepdims=True)) + lam_init)
        ot = acc_scr[0] / l_scr[0] - lam * (acc_scr[1] / l_scr[1])
        ot = ot * lax.rsqrt(jnp.mean(ot * ot, axis=0, keepdims=True) + NORM_EPS)
        o_ref[...] = (ot.T * (sg_ref[...] * (1.0 - lam_init))).astype(o_ref.dtype)


def _key_tile(nk, cap):
    best = LANES
    for t in range(LANES, cap + 1, LANES):
        if nk % t == 0:
            best = t
    return best


def diff_attention(q, k, vt, lamv, sub_g, lam_init, score_bound):
    n, d = q.shape
    nk = k.shape[0]
    hw = 2 * HEAD_DIM
    heads = d // hw
    tq = _tile(n, 2048)
    tk = _key_tile(nk, 1536)
    tc = min(tq, 256)

    safe = score_bound <= MAX_FIXED_OFFSET
    off = jnp.full((1, LANES), score_bound, F32)

    def call(bounded):
        kern = functools.partial(_attn_kernel, lam_init=lam_init, tc=tc, bounded=bounded)
        return pl.pallas_call(
            kern,
            grid=(heads, n // tq, nk // tk),
            in_specs=[pl.BlockSpec((4, HEAD_DIM), lambda h, i, j: (0, 0)),
                      pl.BlockSpec((1, LANES), lambda h, i, j: (0, 0)),
                      pl.BlockSpec((tq, hw), lambda h, i, j: (i, h)),
                      pl.BlockSpec((tk, hw), lambda h, i, j: (j, h)),
                      pl.BlockSpec((hw, tk), lambda h, i, j: (h, j)),
                      pl.BlockSpec((1, hw), lambda h, i, j: (0, 0))],
            out_specs=pl.BlockSpec((tq, hw), lambda h, i, j: (i, h)),
            out_shape=jax.ShapeDtypeStruct((n, d), BF16),
            scratch_shapes=[pltpu.VMEM((2, 1, tq), F32), pltpu.VMEM((2, 1, tq), F32),
                            pltpu.VMEM((2, hw, tq), F32), pltpu.VMEM((2 * (tq // tc), tk, tc), BF16)],
            compiler_params=_params("parallel", "parallel", "arbitrary"),
            name="diff_attention_bounded" if bounded else "diff_attention_online",
        )(lamv, off, q, k, vt, sub_g)

    return lax.cond(safe, lambda: call(True), lambda: call(False))


def _router_kernel(x_ref, g_ref, sh_ref, sc_ref, wrt_ref, h_ref, aff_ref):
    h = _norm_mod(x_ref[...], g_ref[...], sh_ref[...], sc_ref[...])
    half = h.shape[1] // 2
    h_hi = h.astype(BF16)
    h_hi32 = h_hi.astype(F32)
    bits = pltpu.bitcast(h_hi32, U32)
    h_ref[...] = (bits[:, :half] >> 16) | (bits[:, half:] & jnp.uint32(0xFFFF0000))
    h_lo = (h - h_hi32).astype(BF16)
    w = wrt_ref[...]
    w_hi = w.astype(BF16)
    w_lo = (w - w_hi.astype(F32)).astype(BF16)
    dims = (((1,), (1,)), ((), ()))
    logits = (lax.dot_general(w_hi, h_hi, dims, preferred_element_type=F32)
              + lax.dot_general(w_lo, h_hi, dims, preferred_element_type=F32)
              + lax.dot_general(w_hi, h_lo, dims, preferred_element_type=F32))
    ex = jnp.exp(logits - jnp.max(logits, axis=0, keepdims=True))
    aff_ref[...] = ex / jnp.sum(ex, axis=0, keepdims=True)


def router(x, g, sh, sc, w_r_t):
    n, d = x.shape
    e = w_r_t.shape[0]
    tm = _tile(n, 256)
    vec = pl.BlockSpec((1, d), lambda i: (0, 0))
    return pl.pallas_call(
        _router_kernel,
        grid=(n // tm,),
        in_specs=[pl.BlockSpec((tm, d), lambda i: (i, 0)), vec, vec, vec,
                  pl.BlockSpec((e, d), lambda i: (0, 0))],
        out_specs=(pl.BlockSpec((tm, d // 2), lambda i: (i, 0)), pl.BlockSpec((e, tm), lambda i: (0, i))),
        out_shape=(jax.ShapeDtypeStruct((n, d // 2), U32), jax.ShapeDtypeStruct((e, n), F32)),
        compiler_params=_params("parallel"),
        name="router",
    )(x, g, sh, sc, w_r_t)


def _split3(a):
    hi = a.astype(BF16)
    r1 = a - hi.astype(F32)
    mid = r1.astype(BF16)
    lo = (r1 - mid.astype(F32)).astype(BF16)
    return hi, mid, lo


def _topk_kernel(abt_ref, atb_ref, idx_ref, gate_ref, thr_scr, cut_scr, *, cap, cap_pad):
    ne, nb, _ = abt_ref.shape
    ntok = nb * LANES
    bits = pltpu.bitcast(abt_ref[...], I32)
    tok_bt = (lax.broadcasted_iota(I32, (1, nb, LANES), 1) * LANES
              + lax.broadcasted_iota(I32, (1, nb, LANES), 2))

    def count(ones):
        c = jnp.sum(ones, axis=2, keepdims=True)
        return jnp.sum(c, axis=1, keepdims=True)

    def thr_body(_, carry):
        lo, hi = carry
        mid = lo + ((hi - lo + 1) >> 1)
        ok = count(jnp.where(bits >= mid, 1, 0)) >= cap
        return jnp.where(ok, mid, lo), jnp.where(ok, hi, mid - 1)

    lo0 = jnp.zeros((ne, 1, 1), I32)
    hi0 = jnp.full((ne, 1, 1), 0x7F800000, I32)
    thr, _ = lax.fori_loop(0, 31, thr_body, (lo0, hi0))

    tok_eq = jnp.where(bits == thr, tok_bt, ntok)
    need = cap - count(jnp.where(bits > thr, 1, 0))

    def cut_body(_, carry):
        lo, hi = carry
        mid = (lo + hi) >> 1
        ok = count(jnp.where(tok_eq < mid, 1, 0)) >= need
        return jnp.where(ok, lo, mid), jnp.where(ok, mid, hi)

    _, cut = lax.fori_loop(0, int(math.log2(ntok)) + 1, cut_body,
                           (jnp.zeros((ne, 1, 1), I32), jnp.full((ne, 1, 1), ntok, I32)))
    thr_scr[...] = jnp.broadcast_to(thr, thr_scr.shape)
    cut_scr[...] = jnp.broadcast_to(cut, cut_scr.shape)

    r_i = lax.broadcasted_iota(I32, (LANES, LANES), 0)
    c_i = lax.broadcasted_iota(I32, (LANES, LANES), 1)
    tri_t = jnp.where(c_i <= r_i, 1.0, 0.0).astype(BF16)
    rb_i = lax.broadcasted_iota(I32, (nb, nb), 0)
    cb_i = lax.broadcasted_iota(I32, (nb, nb), 1)
    tri_b = jnp.where(cb_i <= rb_i, 1.0, 0.0).astype(BF16)
    tok_b = lax.broadcasted_iota(I32, (nb, LANES), 0) * LANES + lax.broadcasted_iota(I32, (nb, LANES), 1)
    tok_t = lax.broadcasted_iota(I32, (LANES, nb), 1) * LANES + lax.broadcasted_iota(I32, (LANES, nb), 0)
    slot = lax.broadcasted_iota(I32, (1, cap_pad), 1).astype(F32)
    blk_iota = lax.broadcasted_iota(I32, (nb, 1), 0).astype(F32)
    lane_iota = lax.broadcasted_iota(I32, (LANES, 1), 0).astype(F32)

    def select(b, tok, th, ct):
        return jnp.where(b > th, 1.0, jnp.where(b == th, jnp.where(tok < ct, 1.0, 0.0), 0.0))

    def per_expert(e, carry):
        th = thr_scr[e][:1, :1]
        ct = cut_scr[e][:1, :1]
        a_tb = atb_ref[e]
        sel_bt = select(pltpu.bitcast(abt_ref[e], I32), tok_b, th, ct)
        sel_tb = select(pltpu.bitcast(a_tb, I32), tok_t, th, ct)
        cw_t = jnp.dot(tri_t, sel_tb.astype(BF16), preferred_element_type=F32)
        tot = jnp.sum(sel_bt, axis=1, keepdims=True)
        incl = jnp.dot(tri_b, jnp.broadcast_to(tot, (nb, LANES)).astype(BF16),
                       preferred_element_type=F32)[:, :1]
        excl = incl - tot
        blk = jnp.sum(jnp.where(incl <= slot, 1.0, 0.0), axis=0, keepdims=True)
        onehot = jnp.where(blk_iota == blk, 1.0, 0.0)
        oh16 = onehot.astype(BF16)
        g_t = jnp.dot(cw_t.astype(BF16), oh16, preferred_element_type=F32)
        base = jnp.sum(onehot * excl, axis=0, keepdims=True)
        tl = jnp.sum(jnp.where(g_t + base <= slot, 1.0, 0.0), axis=0, keepdims=True)
        idx_ref[pl.ds(e, 1), :] = (blk * LANES + tl).astype(I32)
        hi, mid, lo = _split3(a_tb)
        rows = (jnp.dot(hi, oh16, preferred_element_type=F32)
                + jnp.dot(mid, oh16, preferred_element_type=F32)
                + jnp.dot(lo, oh16, preferred_element_type=F32))
        gate_ref[pl.ds(e, 1), :] = jnp.sum(jnp.where(lane_iota == tl, rows, 0.0), axis=0, keepdims=True)
        return carry

    lax.fori_loop(0, ne, per_expert, 0)


def topk_select(aff_t, cap):
    ne, n = aff_t.shape
    nb = max(LANES, -(-n // LANES))
    nb = -(-nb // LANES) * LANES
    cap_pad = max(LANES, cap)
    padded = jnp.pad(aff_t, ((0, 0), (0, nb * LANES - n)), constant_values=-1.0)
    a_bt = padded.reshape(ne, nb, LANES)
    a_tb = jnp.swapaxes(a_bt, 1, 2)
    kern = functools.partial(_topk_kernel, cap=cap, cap_pad=cap_pad)
    idx, gate = pl.pallas_call(
        kern,
        out_shape=(jax.ShapeDtypeStruct((ne, cap_pad), I32), jax.ShapeDtypeStruct((ne, cap_pad), F32)),
        scratch_shapes=[pltpu.VMEM((ne, 8, LANES), I32), pltpu.VMEM((ne, 8, LANES), I32)],
        compiler_params=pltpu.CompilerParams(vmem_limit_bytes=VMEM_LIMIT),
        name="topk_select",
    )(a_bt, a_tb)
    return idx[:, :cap], gate[:, :cap]


def _moe_kernel(idx_ref, gate_ref, g2_ref, h_hbm, wg_hbm, wu_hbm, wd_hbm, xin_hbm, out_hbm,
                xs, ab, y_scr, wg_f32, wu_f32, wd_f32, wgu, wd, sem_x, sem_a, sem_o, sem_w, *, tc, layer):
    del xin_hbm
    nct = pl.num_programs(1)
    ct = pl.program_id(1)
    s = pl.program_id(0) * nct + ct
    last = pl.num_programs(0) * nct - 1
    slot = s % 2
    base = s * tc
    nxt = jnp.minimum(s + 1, last) * tc

    def x_copy(b, r, sl):
        return pltpu.make_async_copy(h_hbm.at[pl.ds(idx_ref[b + r], 1)], xs.at[sl, pl.ds(r, 1)], sem_x.at[sl])

    def a_copy(r):
        return pltpu.make_async_copy(out_hbm.at[pl.ds(idx_ref[base + r], 1)], ab.at[slot, pl.ds(r, 1)],
                                     sem_a.at[slot])

    def o_copy(b, r, sl):
        return pltpu.make_async_copy(ab.at[sl, pl.ds(r, 1)], out_hbm.at[pl.ds(idx_ref[b + r], 1)], sem_o.at[sl])

    def rows(fn):
        def body(r, c):
            fn(r)
            return c
        lax.fori_loop(0, tc, body, 0, unroll=8)

    ne = pl.num_programs(0)
    expert = pl.program_id(0)

    def w_copies(e):
        return (pltpu.make_async_copy(wg_hbm.at[layer, e], wg_f32, sem_w.at[0]),
                pltpu.make_async_copy(wu_hbm.at[layer, e], wu_f32, sem_w.at[1]),
                pltpu.make_async_copy(wd_hbm.at[layer, e], wd_f32, sem_w.at[2]))

    @pl.when(s == 0)
    def _():
        for cp in w_copies(0):
            cp.start()
        rows(lambda r: x_copy(0, r, 0).start())

    @pl.when(ct == 0)
    def _():
        for cp in w_copies(expert):
            cp.wait()
        f = wd.shape[0]
        wgu[:, :f] = wg_f32[...].astype(BF16)
        wgu[:, f:] = wu_f32[...].astype(BF16)
        wd[...] = wd_f32[...].astype(BF16)

    @pl.when(jnp.logical_and(ct == 0, expert + 1 < ne))
    def _():
        for cp in w_copies(expert + 1):
            cp.start()

    def x_wait(sl):
        pltpu.make_async_copy(h_hbm.at[pl.ds(0, tc)], xs.at[sl], sem_x.at[sl]).wait()

    def o_wait(sl):
        pltpu.make_async_copy(ab.at[sl], out_hbm.at[pl.ds(0, tc)], sem_o.at[sl]).wait()

    x_wait(slot)

    @pl.when(jnp.logical_and(ct == 0, s > 0))
    def _():
        o_wait(1 - slot)

    for r in range(tc):
        x_copy(nxt, r, 1 - slot).start()
        a_copy(r).start()
    xu = xs[slot]
    x_lo = pltpu.bitcast(xu << 16, F32).astype(BF16)
    x_hi = pltpu.bitcast(xu & jnp.uint32(0xFFFF0000), F32).astype(BF16)
    half = xu.shape[1]
    ab2 = (jnp.dot(x_lo, wgu[:half], preferred_element_type=F32)
           + jnp.dot(x_hi, wgu[half:], preferred_element_type=F32))
    f = ab2.shape[1] // 2
    a, b = ab2[:, :f], ab2[:, f:]
    gate = jnp.concatenate([gate_ref[0]] * (f // LANES), axis=1)
    hm = (a * jax.nn.sigmoid(a) * b * gate).astype(BF16)
    y_scr[...] = jnp.dot(hm, wd[...], preferred_element_type=F32) * g2_ref[...]

    pltpu.make_async_copy(out_hbm.at[pl.ds(0, tc)], ab.at[slot], sem_a.at[slot]).wait()

    @pl.when(ct > 0)
    def _():
        o_wait(1 - slot)

    ab[slot] = ab[slot] + y_scr[...]
    rows(lambda r: o_copy(base, r, slot).start())

    @pl.when(s == last)
    def _():
        o_wait(slot)
        x_wait(1 - slot)


def moe_apply(h, x, idx, gate, w_gate, w_up, w_down, layer, g2):
    n, d = x.shape
    ne, cap = idx.shape
    f = w_down.shape[2]
    assert f % LANES == 0
    tc = _tile(cap, 256)
    gate_b = jnp.broadcast_to(gate[:, :, None], (ne, cap, LANES))
    kern = functools.partial(_moe_kernel, tc=tc, layer=layer)
    hbm = pl.BlockSpec(memory_space=pl.ANY)
    grid_spec = pltpu.PrefetchScalarGridSpec(
        num_scalar_prefetch=1,
        grid=(ne, cap // tc),
        in_specs=[pl.BlockSpec((1, tc, LANES), lambda e, c, idx: (e, c, 0)),
                  pl.BlockSpec((1, d), lambda e, c, idx: (0, 0)),
                  hbm, hbm, hbm, hbm, hbm],
        out_specs=hbm,
        scratch_shapes=[pltpu.VMEM((2, tc, d // 2), U32), pltpu.VMEM((2, tc, d), F32), pltpu.VMEM((tc, d), F32),
                        pltpu.VMEM((d, f), F32), pltpu.VMEM((d, f), F32), pltpu.VMEM((f, d), F32),
                        pltpu.VMEM((d, 2 * f), BF16), pltpu.VMEM((f, d), BF16),
                        pltpu.SemaphoreType.DMA((2,)), pltpu.SemaphoreType.DMA((2,)),
                        pltpu.SemaphoreType.DMA((2,)), pltpu.SemaphoreType.DMA((3,))],
    )
    return pl.pallas_call(
        kern,
        grid_spec=grid_spec,
        out_shape=jax.ShapeDtypeStruct((n, d), F32),
        input_output_aliases={7: 0},
        compiler_params=_params("arbitrary", "arbitrary"),
        name="moe_apply",
    )(idx.reshape(-1), gate_b, g2, h, w_gate, w_up, w_down, x)


def moe_layer(x, g, sh, sc, g2, w_r, w_gate, w_up, w_down, layer):
    n = x.shape[0]
    ne = w_r.shape[1]
    h, aff_t = router(x, g, sh, sc, w_r.T)
    idx, gate = topk_select(aff_t, EC_CAPACITY_FACTOR * n // ne)
    return moe_apply(h, x, idx, gate, w_gate, w_up, w_down, layer, g2)


def _rope_perm(width):
    quarter = HEAD_DIM // 4
    starts = jnp.array([0, 2, 4, 6, 1, 3, 5, 7]) * quarter
    blk = (starts[:, None] + jnp.arange(quarter)[None, :]).reshape(-1)
    return (jnp.arange(0, width, 2 * HEAD_DIM)[:, None] + blk[None, :]).reshape(-1)


def _permute_heads(w):
    rows, width = w.shape
    quarter = HEAD_DIM // 4
    w6 = w.reshape(rows, width // (2 * HEAD_DIM), 2, 2, 2, quarter)
    return jnp.transpose(w6, (0, 1, 4, 2, 3, 5)).reshape(rows, width)


def _rope_tables(n, n_ctx):
    quarter = HEAD_DIM // 4
    t = jnp.arange(n, dtype=I32)
    freqs = ROPE_THETA ** (-jnp.arange(quarter, dtype=F32) / quarter)
    ang_r = (t // GRID_W).astype(F32)[:, None] * freqs
    ang_c = (t % GRID_W).astype(F32)[:, None] * freqs
    cos = jnp.concatenate([jnp.cos(ang_r), jnp.cos(ang_c)] * 2, axis=1)
    sin = jnp.concatenate([jnp.sin(ang_r), jnp.sin(ang_c)] * 2, axis=1)
    cos = jnp.concatenate([cos, jnp.ones((n_ctx, HEAD_DIM), F32)], axis=0)
    sin = jnp.concatenate([sin, jnp.zeros((n_ctx, HEAD_DIM), F32)], axis=0)
    return cos, sin


def kernel(x, c, ctx, c_ctx, ada_w, ada_b, norm1_g, norm2_g, router_w, exp_w_gate, exp_w_up, exp_w_down,
           pool_w, pool_scale, gm_w_in, gm_v_g, gm_w_s, gm_b_s, gm_w_out, da_w_q, da_w_k, da_w_v, da_w_o,
           da_q_g, da_k_g, da_lam_q1, da_lam_k1, da_lam_q2, da_lam_k2, da_sub_g):
    bsz, n, d = x.shape
    assert bsz == 1 and c.shape[0] == 1
    n_ctx = ctx.shape[1]
    depth = ada_w.shape[0]
    xs, cs = x[0], ctx[0]

    s8 = jnp.concatenate([c, c_ctx[None], jnp.zeros((6, d), F32)], axis=0)
    mod = adaln(s8, ada_w, ada_b)

    def mods(i, row):
        return [mod[i, row:row + 1, k * d:(k + 1) * d] for k in range(6)]

    def row(v):
        return v.reshape(1, -1)

    for i in range(depth):
        kind, slot = i % N_MIXERS, i // N_MIXERS
        keep_ctx = any(j % N_MIXERS == 2 for j in range(i + 1, depth))
        sh1, sc1, g1, sh2, sc2, g2 = mods(i, 0)
        csh1, csc1, cg1, csh2, csc2, cg2 = mods(i, 1)
        n1, n2 = row(norm1_g[i]), row(norm2_g[i])
        streams = [(xs, sh1, sc1, g1)] + ([(cs, csh1, csc1, cg1)] if keep_ctx else [])

        if kind == 0:
            w = pool_w[slot].astype(BF16)
            outs = [pool_mixer(s, row_rstd(s), n1 * (1.0 + sc), sh, w, row(pool_scale[slot]), g)
                    for s, sh, sc, g in streams]
        elif kind == 1:
            w_in = gm_w_in[slot].astype(BF16)
            w_out = gm_w_out[slot].astype(BF16)
            w_s = gm_w_s[slot].astype(BF16)
            width = w_out.shape[0]
            bs_full = jnp.repeat(gm_b_s[slot].T, width // gm_w_s.shape[1], axis=1)
            outs = []
            for s, sh, sc, g in streams:
                uv, ssq = mm_gelu(norm_mod(s, n1, sh, sc, BF16), w_in)
                z = chunk_gate(uv, ssq, row(gm_v_g[slot]), w_s, bs_full)
                outs.append(mm_resid(z, w_out, s, g))
        else:
            assert not keep_ctx
            lam_init = 0.8 - 0.6 * math.exp(-0.3 * i)
            h_all = norm_mod_pair(xs, cs, n1, jnp.concatenate([sh1, csh1]), jnp.concatenate([sc1, csc1]), BF16)
            cos, sin = _rope_tables(n, n_ctx)
            gperm = (_rope_perm(2 * HEAD_DIM) % HEAD_DIM).reshape(2, HEAD_DIM)
            q_scale = HEAD_DIM ** -0.5 * math.log2(math.e)
            q = mm_qk(h_all, _permute_heads(da_w_q[slot]).astype(BF16), da_q_g[slot][gperm], cos, sin, q_scale, m=n)
            k = mm_qk(h_all, _permute_heads(da_w_k[slot]).astype(BF16), da_k_g[slot][gperm], cos, sin, 1.0)
            vt = mm_transposed(h_all, da_w_v[slot].T.astype(BF16))
            lamv = jnp.stack([da_lam_q1[slot], da_lam_k1[slot], da_lam_q2[slot], da_lam_k2[slot]])
            score_bound = (HEAD_DIM * q_scale * BOUND_SLACK) * jnp.max(jnp.abs(da_q_g[slot])) * jnp.max(jnp.abs(da_k_g[slot]))
            o = diff_attention(q, k, vt, lamv, row(da_sub_g[slot]), lam_init, score_bound)
            outs = [mm_resid(o, da_w_o[slot].astype(BF16), xs, g1)]

        xs = moe_layer(outs[0], n2, sh2, sc2, g2, router_w[i], exp_w_gate, exp_w_up, exp_w_down, i)
        if keep_ctx:
            cs = moe_layer(outs[1], n2, csh2, csc2, cg2, router_w[i], exp_w_gate, exp_w_up, exp_w_down, i)
    return xs[None]
```

```python
import functools
import math

import jax
import jax.numpy as jnp
from jax import lax
from jax.experimental import pallas as pl
from jax.experimental.pallas import tpu as pltpu

F32 = jnp.float32
BF16 = jnp.bfloat16
I32 = jnp.int32
U32 = jnp.uint32

NORM_EPS = 1e-6
LANES = 128
GRID_W = 64
CHUNK = 128
POOL_WINDOWS = (2, 4, 8, 16)
HEAD_DIM = 128
ROPE_THETA = 10000.0
EC_CAPACITY_FACTOR = 2
N_MIXERS = 3
VMEM_LIMIT = 56 * 1024 * 1024
ROW_TILE = 256
WIDE_TILE = 512
MM_TM, MM_TN = 1024, 1024
ADALN_TN = 1024
ATTN_TQ, ATTN_TK_MAX, ATTN_TC = 1024, 1536, 256
MOE_TC = 256
BOUND_SLACK = 1.01
MAX_FIXED_OFFSET = 60.0


def _params(*sem):
    return pltpu.CompilerParams(dimension_semantics=sem, vmem_limit_bytes=VMEM_LIMIT)


def _tile(n, t):
    t = min(n, t)
    assert n % t == 0, (n, t)
    return t


def _norm_mod(x, g, sh, sc):
    ms = jnp.mean(x * x, axis=-1, keepdims=True)
    return (x * lax.rsqrt(ms + NORM_EPS) * g) * (1.0 + sc) + sh


def _adaln_kernel(s_ref, w_ref, b_ref, o_ref):
    s = s_ref[...]
    s = s * jax.nn.sigmoid(s)
    o_ref[0] = jnp.dot(s, w_ref[0], preferred_element_type=F32) + b_ref[0]


def adaln(s8, ada_w, ada_b):
    depth, d, n6 = ada_w.shape
    tn = _tile(n6, ADALN_TN)
    return pl.pallas_call(
        _adaln_kernel,
        grid=(depth, n6 // tn),
        in_specs=[pl.BlockSpec((8, d), lambda l, j: (0, 0)),
                  pl.BlockSpec((1, d, tn), lambda l, j: (l, 0, j)),
                  pl.BlockSpec((1, 1, tn), lambda l, j: (l, 0, j))],
        out_specs=pl.BlockSpec((1, 8, tn), lambda l, j: (l, 0, j)),
        out_shape=jax.ShapeDtypeStruct((depth, 8, n6), F32),
        compiler_params=_params("parallel", "parallel"),
        name="adaln",
    )(s8, ada_w, ada_b.reshape(depth, 1, n6))


def _norm_mod_kernel(x_ref, g_ref, sh_ref, sc_ref, o_ref):
    o_ref[...] = _norm_mod(x_ref[...], g_ref[...], sh_ref[...], sc_ref[...]).astype(o_ref.dtype)


def norm_mod(x, g, sh, sc, dtype):
    n, d = x.shape
    tm = _tile(n, ROW_TILE)
    vec = pl.BlockSpec((1, d), lambda i: (0, 0))
    return pl.pallas_call(
        _norm_mod_kernel,
        grid=(n // tm,),
        in_specs=[pl.BlockSpec((tm, d), lambda i: (i, 0)), vec, vec, vec],
        out_specs=pl.BlockSpec((tm, d), lambda i: (i, 0)),
        out_shape=jax.ShapeDtypeStruct((n, d), dtype),
        compiler_params=_params("parallel"),
        name="norm_mod",
    )(x, g, sh, sc)


def _norm_mod_pair_kernel(x_ref, c_ref, g_ref, sh_ref, sc_ref, o_ref, *, nx):
    is_ctx = pl.program_id(0) >= nx
    src = jnp.where(is_ctx, c_ref[...], x_ref[...])
    sh = jnp.where(is_ctx, sh_ref[1:2], sh_ref[0:1])
    sc = jnp.where(is_ctx, sc_ref[1:2], sc_ref[0:1])
    o_ref[...] = _norm_mod(src, g_ref[...], sh, sc).astype(o_ref.dtype)


def norm_mod_pair(x, ctx, g, sh2, sc2, dtype):
    n, d = x.shape
    nc = ctx.shape[0]
    tm = _tile(nc, ROW_TILE)
    assert n % tm == 0
    nx = n // tm
    vec = pl.BlockSpec((1, d), lambda i: (0, 0))
    vec2 = pl.BlockSpec((2, d), lambda i: (0, 0))
    return pl.pallas_call(
        functools.partial(_norm_mod_pair_kernel, nx=nx),
        grid=(nx + nc // tm,),
        in_specs=[pl.BlockSpec((tm, d), lambda i: (jnp.minimum(i, nx - 1), 0)),
                  pl.BlockSpec((tm, d), lambda i: (jnp.maximum(i - nx, 0), 0)), vec, vec2, vec2],
        out_specs=pl.BlockSpec((tm, d), lambda i: (i, 0)),
        out_shape=jax.ShapeDtypeStruct((n + nc, d), dtype),
        compiler_params=_params("parallel"),
        name="norm_mod_pair",
    )(x, ctx, g, sh2, sc2)


def _rstd_kernel(x_ref, o_ref):
    x = x_ref[...]
    o_ref[...] = jnp.broadcast_to(lax.rsqrt(jnp.mean(x * x, axis=-1, keepdims=True) + NORM_EPS), o_ref.shape)


def row_rstd(x):
    n, d = x.shape
    tm = _tile(n, ROW_TILE)
    return pl.pallas_call(
        _rstd_kernel,
        grid=(n // tm,),
        in_specs=[pl.BlockSpec((tm, d), lambda i: (i, 0))],
        out_specs=pl.BlockSpec((tm, LANES), lambda i: (i, 0)),
        out_shape=jax.ShapeDtypeStruct((n, LANES), F32),
        compiler_params=_params("parallel"),
        name="row_rstd",
    )(x)


def _pool_kernel(cur_ref, prev_ref, next_ref, rc_ref, rp_ref, rn_ref, gs_ref, sh_ref, w_ref, ps_ref, g1_ref, o_ref,
                 *, n, tm):
    g = pl.program_id(0)
    i = pl.program_id(1)
    last = pl.num_programs(1) - 1
    reps = cur_ref.shape[1] // LANES

    def modulated(x, r):
        return x * jnp.concatenate([r] * reps, axis=1) * gs_ref[...] + sh_ref[...]

    t = i * tm + lax.broadcasted_iota(I32, (tm, 1), 0)

    for gi, win in enumerate(POOL_WINDOWS):
        @pl.when(g == gi)
        def _(win=win):
            x_cur = cur_ref[...]
            cur = modulated(x_cur, rc_ref[...])
            prev = jnp.where(i == 0, 0.0, modulated(prev_ref[...], rp_ref[...]))
            nxt = jnp.where(i == last, 0.0, modulated(next_ref[...], rn_ref[...]))
            ext = jnp.concatenate([prev, cur, nxt], axis=0)
            half = win // 2
            s = ext
            step = 1
            while step < win:
                m = s.shape[0] - step
                s = s[:m] + s[step:step + m]
                step *= 2
            wsum = s[8 - half:8 - half + tm]
            cnt = jnp.minimum(t + half, n) - jnp.maximum(t - half, 0)
            dlt = (wsum / cnt.astype(F32) - cur).astype(BF16)
            y = jnp.dot(dlt, w_ref[0], preferred_element_type=F32) * ps_ref[...]
            o_ref[...] = x_cur + g1_ref[...] * y


def pool_mixer(x, rstd, gs, sh, w, pscale, g1):
    n, d = x.shape
    ng, dg, _ = w.shape
    tm = _tile(n, WIDE_TILE)
    nb8 = n // 8
    kern = functools.partial(_pool_kernel, n=n, tm=tm)
    col = pl.BlockSpec((1, dg), lambda g, i: (0, g))

    def halo(width, sel):
        before = pl.BlockSpec((8, width), lambda g, i: (jnp.maximum(i * (tm // 8) - 1, 0), sel(g)))
        after = pl.BlockSpec((8, width), lambda g, i: (jnp.minimum((i + 1) * (tm // 8), nb8 - 1), sel(g)))
        return pl.BlockSpec((tm, width), lambda g, i: (i, sel(g))), before, after

    return pl.pallas_call(
        kern,
        grid=(ng, n // tm),
        in_specs=[*halo(dg, lambda g: g), *halo(LANES, lambda g: 0), col, col,
                  pl.BlockSpec((1, dg, dg), lambda g, i: (g, 0, 0)), col, col],
        out_specs=pl.BlockSpec((tm, dg), lambda g, i: (i, g)),
        out_shape=jax.ShapeDtypeStruct((n, d), F32),
        compiler_params=_params("parallel", "parallel"),
        name="pool_mixer",
    )(x, x, x, rstd, rstd, rstd, gs, sh, w, pscale, g1)


def _mm_call(kern, a, w, extra, extra_specs, out_shape, out_specs, tm, tn, name, scratch=(), m=None):
    k = a.shape[1]
    m = a.shape[0] if m is None else m
    n = w.shape[1]
    return pl.pallas_call(
        kern,
        grid=(pl.cdiv(m, tm), n // tn),
        in_specs=[pl.BlockSpec((tm, k), lambda i, j: (i, 0)),
                  pl.BlockSpec((k, tn), lambda i, j: (0, j))] + list(extra_specs),
        out_specs=out_specs,
        out_shape=out_shape,
        scratch_shapes=list(scratch),
        compiler_params=_params("parallel", "arbitrary"),
        name=name,
    )(a, w, *extra)


def _mm_t_kernel(a_ref, wt_ref, o_ref):
    o_ref[...] = lax.dot_general(wt_ref[...], a_ref[...], (((1,), (1,)), ((), ())),
                                 preferred_element_type=F32).astype(o_ref.dtype)


def mm_transposed(a, wt, dtype=BF16):
    m, k = a.shape
    n = wt.shape[0]
    tm, tn = min(m, MM_TM), _tile(n, MM_TN)
    return pl.pallas_call(
        _mm_t_kernel,
        grid=(pl.cdiv(m, tm), n // tn),
        in_specs=[pl.BlockSpec((tm, k), lambda i, j: (i, 0)),
                  pl.BlockSpec((tn, k), lambda i, j: (j, 0))],
        out_specs=pl.BlockSpec((tn, tm), lambda i, j: (j, i)),
        out_shape=jax.ShapeDtypeStruct((n, m), dtype),
        compiler_params=_params("parallel", "arbitrary"),
        name="mm_transposed",
    )(a, wt)


def _mm_qk_kernel(a_ref, w_ref, g_ref, cos_ref, sin_ref, o_ref, *, scale):
    y = jnp.dot(a_ref[...], w_ref[...], preferred_element_type=F32)
    tn = y.shape[1]
    cos = cos_ref[...]
    sin = sin_ref[...]
    ga_cos, ga_sin = cos * g_ref[0:1], sin * g_ref[0:1]
    gb_cos, gb_sin = cos * g_ref[1:2], sin * g_ref[1:2]
    comp0 = lax.broadcasted_iota(I32, (1, HEAD_DIM), 1) < HEAD_DIM // 2
    for hd in range(tn // (2 * HEAD_DIM)):
        a = y[:, 2 * hd * HEAD_DIM:(2 * hd + 1) * HEAD_DIM]
        b = y[:, (2 * hd + 1) * HEAD_DIM:(2 * hd + 2) * HEAD_DIM]
        sq = a * a + b * b
        ssq0 = jnp.sum(jnp.where(comp0, sq, 0.0), axis=-1, keepdims=True)
        ssq1 = jnp.sum(jnp.where(comp0, 0.0, sq), axis=-1, keepdims=True)
        rstd = jnp.where(comp0, lax.rsqrt(ssq0 * (1.0 / HEAD_DIM) + NORM_EPS),
                         lax.rsqrt(ssq1 * (1.0 / HEAD_DIM) + NORM_EPS)) * scale
        o_ref[:, 2 * hd * HEAD_DIM:(2 * hd + 1) * HEAD_DIM] = ((a * ga_cos - b * gb_sin) * rstd).astype(o_ref.dtype)
        o_ref[:, (2 * hd + 1) * HEAD_DIM:(2 * hd + 2) * HEAD_DIM] = ((b * gb_cos + a * ga_sin) * rstd).astype(o_ref.dtype)


def mm_qk(a, w, gain, cos, sin, scale, m=None):
    m = a.shape[0] if m is None else m
    n = w.shape[1]
    tm, tn = min(m, MM_TM), _tile(n, MM_TN)
    kern = functools.partial(_mm_qk_kernel, scale=scale)
    tab = pl.BlockSpec((tm, HEAD_DIM), lambda i, j: (i, 0))
    return _mm_call(kern, a, w, (gain, cos, sin),
                    (pl.BlockSpec((2, HEAD_DIM), lambda i, j: (0, 0)), tab, tab),
                    jax.ShapeDtypeStruct((m, n), BF16),
                    pl.BlockSpec((tm, tn), lambda i, j: (i, j)), tm, tn, "mm_qk", m=m)


def _mm_gelu_kernel(a_ref, w_ref, o_ref, ssq_ref, *, nj_half):
    j = pl.program_id(1)
    y = jax.nn.gelu(jnp.dot(a_ref[...], w_ref[...], preferred_element_type=F32))
    o_ref[...] = y.astype(o_ref.dtype)

    @pl.when(j == nj_half)
    def _():
        ssq_ref[...] = jnp.zeros_like(ssq_ref)

    @pl.when(j >= nj_half)
    def _():
        ssq_ref[...] += jnp.sum(y * y, axis=-1, keepdims=True)


def mm_gelu(a, w):
    m, _ = a.shape
    n = w.shape[1]
    tm, tn = min(m, MM_TM), _tile(n // 2, MM_TN)
    kern = functools.partial(_mm_gelu_kernel, nj_half=(n // 2) // tn)
    return _mm_call(kern, a, w, (), (),
                    (jax.ShapeDtypeStruct((m, n), BF16), jax.ShapeDtypeStruct((m, LANES), F32)),
                    (pl.BlockSpec((tm, tn), lambda i, j: (i, j)),
                     pl.BlockSpec((tm, LANES), lambda i, j: (i, 0))), tm, tn, "mm_gelu")


def _mm_resid_kernel(a_ref, w_ref, x_ref, g_ref, o_ref):
    y = jnp.dot(a_ref[...], w_ref[...], preferred_element_type=F32)
    o_ref[...] = x_ref[...] + g_ref[...] * y


def mm_resid(a, w, x, g1):
    m, _ = a.shape
    n = w.shape[1]
    tm, tn = min(m, MM_TM), _tile(n, MM_TN)
    blk = pl.BlockSpec((tm, tn), lambda i, j: (i, j))
    return _mm_call(_mm_resid_kernel, a, w, (x, g1),
                    (blk, pl.BlockSpec((1, tn), lambda i, j: (0, j))),
                    jax.ShapeDtypeStruct((m, n), F32), blk, tm, tn, "mm_resid")


def _chunk_gate_kernel(u_ref, v_ref, ssq_ref, vg_ref, ws_ref, bs_ref, z_ref, *, width):
    tm = u_ref.shape[0]
    rstd = lax.rsqrt(ssq_ref[:, :1] * (1.0 / width) + NORM_EPS)

    def body(g, carry):
        col = pl.multiple_of(g * LANES, LANES)
        vg = vg_ref[:, pl.ds(col, LANES)]
        chunks = [slice(c * CHUNK, (c + 1) * CHUNK) for c in range(tm // CHUNK)]
        vn = jnp.concatenate([(v_ref[rows, pl.ds(col, LANES)].astype(F32) * rstd[rows] * vg).astype(BF16)
                              for rows in chunks], axis=1)
        sv = jnp.dot(ws_ref[g], vn, preferred_element_type=F32)
        bias = bs_ref[:, pl.ds(col, LANES)]
        for c, rows in enumerate(chunks):
            gate = sv[:, c * LANES:(c + 1) * LANES] + bias
            z_ref[rows, pl.ds(col, LANES)] = (u_ref[rows, pl.ds(col, LANES)].astype(F32) * gate).astype(BF16)
        return carry

    lax.fori_loop(0, width // LANES, body, 0, unroll=2)


def chunk_gate(uv, ssq, vg, ws, bs_full):
    n, w2 = uv.shape
    width = w2 // 2
    tm = _tile(n, WIDE_TILE)
    return pl.pallas_call(
        functools.partial(_chunk_gate_kernel, width=width),
        grid=(n // tm,),
        in_specs=[pl.BlockSpec((tm, width), lambda i: (i, 0)),
                  pl.BlockSpec((tm, width), lambda i: (i, 1)),
                  pl.BlockSpec((tm, LANES), lambda i: (i, 0)),
                  pl.BlockSpec((1, width), lambda i: (0, 0)),
                  pl.BlockSpec(ws.shape, lambda i: (0, 0, 0)),
                  pl.BlockSpec((CHUNK, width), lambda i: (0, 0))],
        out_specs=pl.BlockSpec((tm, width), lambda i: (i, 0)),
        out_shape=jax.ShapeDtypeStruct((n, width), BF16),
        compiler_params=_params("parallel"),
        name="chunk_gate",
    )(uv, uv, ssq, vg, ws, bs_full)


def _attn_kernel(lamv_ref, off_ref, q_ref, k_ref, vt_ref, sg_ref, o_ref, m_scr, l_scr, acc_scr, p_scr,
                 *, lam_init, tc, tk, bounded):
    nbuf = p_scr.shape[0]
    nchunk = q_ref.shape[0] // tc
    lane = lax.broadcasted_iota(I32, (1, 2 * HEAD_DIM), 1) % HEAD_DIM
    qcs = [q_ref[...] * jnp.where((lane < HEAD_DIM // 2) == (c == 0), 1.0, 0.0).astype(BF16) for c in range(2)]

    l_scr[...] = jnp.zeros_like(l_scr)
    acc_scr[...] = jnp.zeros_like(acc_scr)
    if not bounded:
        m_scr[...] = jnp.full_like(m_scr, -jnp.inf)

    def key_tile(j, carry):
        k0 = pl.multiple_of(j * tk, tk)
        vt = vt_ref[:, pl.ds(k0, tk)]
        kt = k_ref[pl.ds(k0, tk), :]
        scores = {}
        for c in range(2):
            for r in range(nchunk):
                scores[r, c] = lax.dot_general(kt, qcs[c][r * tc:(r + 1) * tc], (((1,), (1,)), ((), ())),
                                               preferred_element_type=F32)
        for r in range(nchunk):
            qs = slice(r * tc, (r + 1) * tc)
            for c in range(2):
                buf = (2 * r + c) % nbuf
                s = scores[r, c]
                if bounded:
                    m_new = off_ref[:, :1]
                else:
                    m_prev = m_scr[c, :, qs]
                    m_new = jnp.maximum(m_prev, jnp.max(s, axis=0, keepdims=True))
                psum = jnp.zeros((16, tc), F32)
                for g in range(tk // 16):
                    pg = jnp.exp2(s[16 * g:16 * (g + 1)] - m_new)
                    psum = psum + pg
                    p_scr[buf, 16 * g:16 * (g + 1), :] = pg.astype(BF16)
                lsum = jnp.sum(psum, axis=0, keepdims=True)
                pv = jnp.dot(vt, p_scr[buf], preferred_element_type=F32)
                if bounded:
                    l_scr[c, :, qs] += lsum
                    acc_scr[c, :, qs] += pv
                else:
                    alpha = jnp.exp2(m_prev - m_new)
                    l_scr[c, :, qs] = alpha * l_scr[c, :, qs] + lsum
                    acc_scr[c, :, qs] = alpha * acc_scr[c, :, qs] + pv
                    m_scr[c, :, qs] = m_new
        return carry

    lax.fori_loop(0, k_ref.shape[0] // tk, key_tile, 0)

    lv = lamv_ref[...]
    lam = (jnp.exp(jnp.sum(lv[0:1] * lv[1:2], axis=-1, keepdims=True))
           - jnp.exp(jnp.sum(lv[2:3] * lv[3:4], axis=-1, keepdims=True)) + lam_init)
    ot = acc_scr[0] / l_scr[0] - lam * (acc_scr[1] / l_scr[1])
    ot = ot * lax.rsqrt(jnp.mean(ot * ot, axis=0, keepdims=True) + NORM_EPS)
    o_ref[...] = (ot.T * (sg_ref[...] * (1.0 - lam_init))).astype(o_ref.dtype)


def _key_tile(nk, cap):
    best = LANES
    for t in range(LANES, cap + 1, LANES):
        if nk % t == 0:
            best = t
    return best


def diff_attention(q, k, vt, lamv, sub_g, lam_init, score_bound):
    n, d = q.shape
    nk = k.shape[0]
    hw = 2 * HEAD_DIM
    heads = d // hw
    tq = _tile(n, ATTN_TQ)
    tk = _key_tile(nk, ATTN_TK_MAX)
    tc = min(tq, ATTN_TC)

    safe = score_bound <= MAX_FIXED_OFFSET
    off = jnp.full((1, LANES), score_bound, F32)

    def call(bounded):
        kern = functools.partial(_attn_kernel, lam_init=lam_init, tc=tc, tk=tk, bounded=bounded)
        return pl.pallas_call(
            kern,
            grid=(heads, n // tq),
            in_specs=[pl.BlockSpec((4, HEAD_DIM), lambda h, i: (0, 0)),
                      pl.BlockSpec((1, LANES), lambda h, i: (0, 0)),
                      pl.BlockSpec((tq, hw), lambda h, i: (i, h)),
                      pl.BlockSpec((nk, hw), lambda h, i: (0, h)),
                      pl.BlockSpec((hw, nk), lambda h, i: (h, 0)),
                      pl.BlockSpec((1, hw), lambda h, i: (0, 0))],
            out_specs=pl.BlockSpec((tq, hw), lambda h, i: (i, h)),
            out_shape=jax.ShapeDtypeStruct((n, d), BF16),
            scratch_shapes=[pltpu.VMEM((2, 1, tq), F32), pltpu.VMEM((2, 1, tq), F32),
                            pltpu.VMEM((2, hw, tq), F32), pltpu.VMEM((2 * (tq // tc), tk, tc), BF16)],
            compiler_params=_params("parallel", "arbitrary"),
            name="diff_attention_bounded" if bounded else "diff_attention_online",
        )(lamv, off, q, k, vt, sub_g)

    return lax.cond(safe, lambda: call(True), lambda: call(False))


def _router_kernel(x_ref, g_ref, sh_ref, sc_ref, wrt_ref, h_ref, aff_ref):
    h = _norm_mod(x_ref[...], g_ref[...], sh_ref[...], sc_ref[...])
    half = h.shape[1] // 2
    h_hi = h.astype(BF16)
    h_hi32 = h_hi.astype(F32)
    bits = pltpu.bitcast(h_hi32, U32)
    h_ref[...] = (bits[:, :half] >> 16) | (bits[:, half:] & jnp.uint32(0xFFFF0000))
    h_lo = (h - h_hi32).astype(BF16)
    w = wrt_ref[...]
    w_hi = w.astype(BF16)
    w_lo = (w - w_hi.astype(F32)).astype(BF16)
    dims = (((1,), (1,)), ((), ()))
    logits = (lax.dot_general(w_hi, h_hi, dims, preferred_element_type=F32)
              + lax.dot_general(w_lo, h_hi, dims, preferred_element_type=F32)
              + lax.dot_general(w_hi, h_lo, dims, preferred_element_type=F32))
    ex = jnp.exp(logits - jnp.max(logits, axis=0, keepdims=True))
    aff_ref[...] = ex / jnp.sum(ex, axis=0, keepdims=True)


def router(x, g, sh, sc, w_r_t):
    n, d = x.shape
    e = w_r_t.shape[0]
    tm = _tile(n, ROW_TILE)
    vec = pl.BlockSpec((1, d), lambda i: (0, 0))
    return pl.pallas_call(
        _router_kernel,
        grid=(n // tm,),
        in_specs=[pl.BlockSpec((tm, d), lambda i: (i, 0)), vec, vec, vec,
                  pl.BlockSpec((e, d), lambda i: (0, 0))],
        out_specs=(pl.BlockSpec((tm, d // 2), lambda i: (i, 0)), pl.BlockSpec((e, tm), lambda i: (0, i))),
        out_shape=(jax.ShapeDtypeStruct((n, d // 2), U32), jax.ShapeDtypeStruct((e, n), F32)),
        compiler_params=_params("parallel"),
        name="router",
    )(x, g, sh, sc, w_r_t)


def _split3(a):
    hi = a.astype(BF16)
    r1 = a - hi.astype(F32)
    mid = r1.astype(BF16)
    lo = (r1 - mid.astype(F32)).astype(BF16)
    return hi, mid, lo


def _topk_kernel(abt_ref, atb_ref, idx_ref, gate_ref, thr_scr, cut_scr, *, cap, cap_pad):
    ne, nb, _ = abt_ref.shape
    ntok = nb * LANES
    bits = pltpu.bitcast(abt_ref[...], I32)
    tok_bt = (lax.broadcasted_iota(I32, (1, nb, LANES), 1) * LANES
              + lax.broadcasted_iota(I32, (1, nb, LANES), 2))

    def count(ones):
        c = jnp.sum(ones, axis=2, keepdims=True)
        return jnp.sum(c, axis=1, keepdims=True)

    def thr_body(_, carry):
        lo, hi = carry
        mid = lo + ((hi - lo + 1) >> 1)
        ok = count(jnp.where(bits >= mid, 1, 0)) >= cap
        return jnp.where(ok, mid, lo), jnp.where(ok, hi, mid - 1)

    lo0 = jnp.zeros((ne, 1, 1), I32)
    hi0 = jnp.full((ne, 1, 1), 0x7F800000, I32)
    thr, _ = lax.fori_loop(0, 31, thr_body, (lo0, hi0))

    tok_eq = jnp.where(bits == thr, tok_bt, ntok)
    need = cap - count(jnp.where(bits > thr, 1, 0))

    def cut_body(_, carry):
        lo, hi = carry
        mid = (lo + hi) >> 1
        ok = count(jnp.where(tok_eq < mid, 1, 0)) >= need
        return jnp.where(ok, lo, mid), jnp.where(ok, mid, hi)

    _, cut = lax.fori_loop(0, int(math.log2(ntok)) + 1, cut_body,
                           (jnp.zeros((ne, 1, 1), I32), jnp.full((ne, 1, 1), ntok, I32)))
    thr_scr[...] = jnp.broadcast_to(thr, thr_scr.shape)
    cut_scr[...] = jnp.broadcast_to(cut, cut_scr.shape)

    r_i = lax.broadcasted_iota(I32, (LANES, LANES), 0)
    c_i = lax.broadcasted_iota(I32, (LANES, LANES), 1)
    tri_t = jnp.where(c_i <= r_i, 1.0, 0.0).astype(BF16)
    rb_i = lax.broadcasted_iota(I32, (nb, nb), 0)
    cb_i = lax.broadcasted_iota(I32, (nb, nb), 1)
    tri_b = jnp.where(cb_i <= rb_i, 1.0, 0.0).astype(BF16)
    tok_b = lax.broadcasted_iota(I32, (nb, LANES), 0) * LANES + lax.broadcasted_iota(I32, (nb, LANES), 1)
    tok_t = lax.broadcasted_iota(I32, (LANES, nb), 1) * LANES + lax.broadcasted_iota(I32, (LANES, nb), 0)
    slot = lax.broadcasted_iota(I32, (1, cap_pad), 1).astype(F32)
    blk_iota = lax.broadcasted_iota(I32, (nb, 1), 0).astype(F32)
    lane_iota = lax.broadcasted_iota(I32, (LANES, 1), 0).astype(F32)

    def select(b, tok, th, ct):
        return jnp.where(b > th, 1.0, jnp.where(b == th, jnp.where(tok < ct, 1.0, 0.0), 0.0))

    def per_expert(e, carry):
        th = thr_scr[e][:1, :1]
        ct = cut_scr[e][:1, :1]
        a_tb = atb_ref[e]
        sel_bt = select(pltpu.bitcast(abt_ref[e], I32), tok_b, th, ct)
        sel_tb = select(pltpu.bitcast(a_tb, I32), tok_t, th, ct)
        cw_t = jnp.dot(tri_t, sel_tb.astype(BF16), preferred_element_type=F32)
        tot = jnp.sum(sel_bt, axis=1, keepdims=True)
        incl = jnp.dot(tri_b, jnp.broadcast_to(tot, (nb, LANES)).astype(BF16),
                       preferred_element_type=F32)[:, :1]
        excl = incl - tot
        blk = jnp.sum(jnp.where(incl <= slot, 1.0, 0.0), axis=0, keepdims=True)
        onehot = jnp.where(blk_iota == blk, 1.0, 0.0)
        oh16 = onehot.astype(BF16)
        g_t = jnp.dot(cw_t.astype(BF16), oh16, preferred_element_type=F32)
        base = jnp.sum(onehot * excl, axis=0, keepdims=True)
        tl = jnp.sum(jnp.where(g_t + base <= slot, 1.0, 0.0), axis=0, keepdims=True)
        idx_ref[pl.ds(e, 1), :] = (blk * LANES + tl).astype(I32)
        hi, mid, lo = _split3(a_tb)
        rows = (jnp.dot(hi, oh16, preferred_element_type=F32)
                + jnp.dot(mid, oh16, preferred_element_type=F32)
                + jnp.dot(lo, oh16, preferred_element_type=F32))
        gate_ref[pl.ds(e, 1), :] = jnp.sum(jnp.where(lane_iota == tl, rows, 0.0), axis=0, keepdims=True)
        return carry

    lax.fori_loop(0, ne, per_expert, 0)


def topk_select(aff_t, cap):
    ne, n = aff_t.shape
    nb = max(LANES, -(-n // LANES))
    nb = -(-nb // LANES) * LANES
    cap_pad = max(LANES, cap)
    padded = jnp.pad(aff_t, ((0, 0), (0, nb * LANES - n)), constant_values=-1.0)
    a_bt = padded.reshape(ne, nb, LANES)
    a_tb = jnp.swapaxes(a_bt, 1, 2)
    kern = functools.partial(_topk_kernel, cap=cap, cap_pad=cap_pad)
    idx, gate = pl.pallas_call(
        kern,
        out_shape=(jax.ShapeDtypeStruct((ne, cap_pad), I32), jax.ShapeDtypeStruct((ne, cap_pad), F32)),
        scratch_shapes=[pltpu.VMEM((ne, 8, LANES), I32), pltpu.VMEM((ne, 8, LANES), I32)],
        compiler_params=pltpu.CompilerParams(vmem_limit_bytes=VMEM_LIMIT),
        name="topk_select",
    )(a_bt, a_tb)
    return idx[:, :cap], gate[:, :cap]


def _moe_kernel(idx_ref, gate_ref, g2_ref, h_hbm, wg_hbm, wu_hbm, wd_hbm, xin_hbm, out_hbm,
                xs, ab, y_scr, wg_f32, wu_f32, wd_f32, wgu, wd, sem_x, sem_a, sem_o, sem_w, *, tc, layer):
    del xin_hbm
    nct = pl.num_programs(1)
    ct = pl.program_id(1)
    s = pl.program_id(0) * nct + ct
    last = pl.num_programs(0) * nct - 1
    slot = s % 2
    base = s * tc
    nxt = jnp.minimum(s + 1, last) * tc

    def x_copy(b, r, sl):
        return pltpu.make_async_copy(h_hbm.at[pl.ds(idx_ref[b + r], 1)], xs.at[sl, pl.ds(r, 1)], sem_x.at[sl])

    def a_copy(r):
        return pltpu.make_async_copy(out_hbm.at[pl.ds(idx_ref[base + r], 1)], ab.at[slot, pl.ds(r, 1)],
                                     sem_a.at[slot])

    def o_copy(b, r, sl):
        return pltpu.make_async_copy(ab.at[sl, pl.ds(r, 1)], out_hbm.at[pl.ds(idx_ref[b + r], 1)], sem_o.at[sl])

    def rows(fn):
        def body(r, c):
            fn(r)
            return c
        lax.fori_loop(0, tc, body, 0, unroll=8)

    ne = pl.num_programs(0)
    expert = pl.program_id(0)

    def w_copies(e):
        return (pltpu.make_async_copy(wg_hbm.at[layer, e], wg_f32, sem_w.at[0]),
                pltpu.make_async_copy(wu_hbm.at[layer, e], wu_f32, sem_w.at[1]),
                pltpu.make_async_copy(wd_hbm.at[layer, e], wd_f32, sem_w.at[2]))

    @pl.when(s == 0)
    def _():
        for cp in w_copies(0):
            cp.start()
        rows(lambda r: x_copy(0, r, 0).start())

    @pl.when(ct == 0)
    def _():
        for cp in w_copies(expert):
            cp.wait()
        f = wd.shape[0]
        wgu[:, :f] = wg_f32[...].astype(BF16)
        wgu[:, f:] = wu_f32[...].astype(BF16)
        wd[...] = wd_f32[...].astype(BF16)

    @pl.when(jnp.logical_and(ct == 0, expert + 1 < ne))
    def _():
        for cp in w_copies(expert + 1):
            cp.start()

    def x_wait(sl):
        pltpu.make_async_copy(h_hbm.at[pl.ds(0, tc)], xs.at[sl], sem_x.at[sl]).wait()

    def o_wait(sl):
        pltpu.make_async_copy(ab.at[sl], out_hbm.at[pl.ds(0, tc)], sem_o.at[sl]).wait()

    x_wait(slot)

    @pl.when(jnp.logical_and(ct == 0, s > 0))
    def _():
        o_wait(1 - slot)

    for r in range(tc):
        x_copy(nxt, r, 1 - slot).start()
        a_copy(r).start()
    xu = xs[slot]
    x_lo = pltpu.bitcast(xu << 16, F32).astype(BF16)
    x_hi = pltpu.bitcast(xu & jnp.uint32(0xFFFF0000), F32).astype(BF16)
    half = xu.shape[1]
    ab2 = (jnp.dot(x_lo, wgu[:half], preferred_element_type=F32)
           + jnp.dot(x_hi, wgu[half:], preferred_element_type=F32))
    f = ab2.shape[1] // 2
    a, b = ab2[:, :f], ab2[:, f:]
    gate = jnp.concatenate([gate_ref[0]] * (f // LANES), axis=1)
    hm = (a * jax.nn.sigmoid(a) * b * gate).astype(BF16)
    y_scr[...] = jnp.dot(hm, wd[...], preferred_element_type=F32) * g2_ref[...]

    pltpu.make_async_copy(out_hbm.at[pl.ds(0, tc)], ab.at[slot], sem_a.at[slot]).wait()

    @pl.when(ct > 0)
    def _():
        o_wait(1 - slot)

    ab[slot] = ab[slot] + y_scr[...]
    rows(lambda r: o_copy(base, r, slot).start())

    @pl.when(s == last)
    def _():
        o_wait(slot)
        x_wait(1 - slot)


def moe_apply(h, x, idx, gate, w_gate, w_up, w_down, layer, g2):
    n, d = x.shape
    ne, cap = idx.shape
    f = w_down.shape[2]
    assert f % LANES == 0
    tc = _tile(cap, MOE_TC)
    gate_b = jnp.broadcast_to(gate[:, :, None], (ne, cap, LANES))
    kern = functools.partial(_moe_kernel, tc=tc, layer=layer)
    hbm = pl.BlockSpec(memory_space=pl.ANY)
    grid_spec = pltpu.PrefetchScalarGridSpec(
        num_scalar_prefetch=1,
        grid=(ne, cap // tc),
        in_specs=[pl.BlockSpec((1, tc, LANES), lambda e, c, idx: (e, c, 0)),
                  pl.BlockSpec((1, d), lambda e, c, idx: (0, 0)),
                  hbm, hbm, hbm, hbm, hbm],
        out_specs=hbm,
        scratch_shapes=[pltpu.VMEM((2, tc, d // 2), U32), pltpu.VMEM((2, tc, d), F32), pltpu.VMEM((tc, d), F32),
                        pltpu.VMEM((d, f), F32), pltpu.VMEM((d, f), F32), pltpu.VMEM((f, d), F32),
                        pltpu.VMEM((d, 2 * f), BF16), pltpu.VMEM((f, d), BF16),
                        pltpu.SemaphoreType.DMA((2,)), pltpu.SemaphoreType.DMA((2,)),
                        pltpu.SemaphoreType.DMA((2,)), pltpu.SemaphoreType.DMA((3,))],
    )
    return pl.pallas_call(
        kern,
        grid_spec=grid_spec,
        out_shape=jax.ShapeDtypeStruct((n, d), F32),
        input_output_aliases={7: 0},
        compiler_params=_params("arbitrary", "arbitrary"),
        name="moe_apply",
    )(idx.reshape(-1), gate_b, g2, h, w_gate, w_up, w_down, x)


def moe_layer(x, g, sh, sc, g2, w_r, w_gate, w_up, w_down, layer):
    n = x.shape[0]
    ne = w_r.shape[1]
    h, aff_t = router(x, g, sh, sc, w_r.T)
    idx, gate = topk_select(aff_t, EC_CAPACITY_FACTOR * n // ne)
    return moe_apply(h, x, idx, gate, w_gate, w_up, w_down, layer, g2)


def _rope_perm(width):
    quarter = HEAD_DIM // 4
    starts = jnp.array([0, 2, 4, 6, 1, 3, 5, 7]) * quarter
    blk = (starts[:, None] + jnp.arange(quarter)[None, :]).reshape(-1)
    return (jnp.arange(0, width, 2 * HEAD_DIM)[:, None] + blk[None, :]).reshape(-1)


def _permute_heads(w):
    rows, width = w.shape
    quarter = HEAD_DIM // 4
    w6 = w.reshape(rows, width // (2 * HEAD_DIM), 2, 2, 2, quarter)
    return jnp.transpose(w6, (0, 1, 4, 2, 3, 5)).reshape(rows, width)


def _rope_tables(n, n_ctx):
    quarter = HEAD_DIM // 4
    t = jnp.arange(n, dtype=I32)
    freqs = ROPE_THETA ** (-jnp.arange(quarter, dtype=F32) / quarter)
    ang_r = (t // GRID_W).astype(F32)[:, None] * freqs
    ang_c = (t % GRID_W).astype(F32)[:, None] * freqs
    cos = jnp.concatenate([jnp.cos(ang_r), jnp.cos(ang_c)] * 2, axis=1)
    sin = jnp.concatenate([jnp.sin(ang_r), jnp.sin(ang_c)] * 2, axis=1)
    cos = jnp.concatenate([cos, jnp.ones((n_ctx, HEAD_DIM), F32)], axis=0)
    sin = jnp.concatenate([sin, jnp.zeros((n_ctx, HEAD_DIM), F32)], axis=0)
    return cos, sin


def kernel(x, c, ctx, c_ctx, ada_w, ada_b, norm1_g, norm2_g, router_w, exp_w_gate, exp_w_up, exp_w_down,
           pool_w, pool_scale, gm_w_in, gm_v_g, gm_w_s, gm_b_s, gm_w_out, da_w_q, da_w_k, da_w_v, da_w_o,
           da_q_g, da_k_g, da_lam_q1, da_lam_k1, da_lam_q2, da_lam_k2, da_sub_g):
    bsz, n, d = x.shape
    assert bsz == 1 and c.shape[0] == 1
    n_ctx = ctx.shape[1]
    depth = ada_w.shape[0]
    xs, cs = x[0], ctx[0]

    s8 = jnp.concatenate([c, c_ctx[None], jnp.zeros((6, d), F32)], axis=0)
    mod = adaln(s8, ada_w, ada_b)

    def mods(i, row):
        return [mod[i, row:row + 1, k * d:(k + 1) * d] for k in range(6)]

    def row(v):
        return v.reshape(1, -1)

    for i in range(depth):
        kind, slot = i % N_MIXERS, i // N_MIXERS
        keep_ctx = any(j % N_MIXERS == 2 for j in range(i + 1, depth))
        sh1, sc1, g1, sh2, sc2, g2 = mods(i, 0)
        csh1, csc1, cg1, csh2, csc2, cg2 = mods(i, 1)
        n1, n2 = row(norm1_g[i]), row(norm2_g[i])
        streams = [(xs, sh1, sc1, g1)] + ([(cs, csh1, csc1, cg1)] if keep_ctx else [])

        if kind == 0:
            w = pool_w[slot].astype(BF16)
            outs = [pool_mixer(s, row_rstd(s), n1 * (1.0 + sc), sh, w, row(pool_scale[slot]), g)
                    for s, sh, sc, g in streams]
        elif kind == 1:
            w_in = gm_w_in[slot].astype(BF16)
            w_out = gm_w_out[slot].astype(BF16)
            w_s = gm_w_s[slot].astype(BF16)
            width = w_out.shape[0]
            bs_full = jnp.repeat(gm_b_s[slot].T, width // gm_w_s.shape[1], axis=1)
            outs = []
            for s, sh, sc, g in streams:
                uv, ssq = mm_gelu(norm_mod(s, n1, sh, sc, BF16), w_in)
                z = chunk_gate(uv, ssq, row(gm_v_g[slot]), w_s, bs_full)
                outs.append(mm_resid(z, w_out, s, g))
        else:
            assert not keep_ctx
            lam_init = 0.8 - 0.6 * math.exp(-0.3 * i)
            h_all = norm_mod_pair(xs, cs, n1, jnp.concatenate([sh1, csh1]), jnp.concatenate([sc1, csc1]), BF16)
            cos, sin = _rope_tables(n, n_ctx)
            gperm = (_rope_perm(2 * HEAD_DIM) % HEAD_DIM).reshape(2, HEAD_DIM)
            q_scale = HEAD_DIM ** -0.5 * math.log2(math.e)
            q = mm_qk(h_all, _permute_heads(da_w_q[slot]).astype(BF16), da_q_g[slot][gperm], cos, sin, q_scale, m=n)
            k = mm_qk(h_all, _permute_heads(da_w_k[slot]).astype(BF16), da_k_g[slot][gperm], cos, sin, 1.0)
            vt = mm_transposed(h_all, da_w_v[slot].T.astype(BF16))
            lamv = jnp.stack([da_lam_q1[slot], da_lam_k1[slot], da_lam_q2[slot], da_lam_k2[slot]])
            score_bound = (HEAD_DIM * q_scale * BOUND_SLACK) * jnp.max(jnp.abs(da_q_g[slot])) * jnp.max(jnp.abs(da_k_g[slot]))
            o = diff_attention(q, k, vt, lamv, row(da_sub_g[slot]), lam_init, score_bound)
            outs = [mm_resid(o, da_w_o[slot].astype(BF16), xs, g1)]

        xs = moe_layer(outs[0], n2, sh2, sc2, g2, router_w[i], exp_w_gate, exp_w_up, exp_w_down, i)
        if keep_ctx:
            cs = moe_layer(outs[1], n2, csh2, csc2, cg2, router_w[i], exp_w_gate, exp_w_up, exp_w_down, i)
    return xs[None]
```

```python
import functools
import math

import jax
import jax.numpy as jnp
from jax import lax
from jax.experimental import pallas as pl
from jax.experimental.pallas import tpu as pltpu

F32 = jnp.float32
BF16 = jnp.bfloat16
I32 = jnp.int32
U32 = jnp.uint32

NORM_EPS = 1e-6
LANES = 128
GRID_W = 64
CHUNK = 128
POOL_WINDOWS = (2, 4, 8, 16)
HEAD_DIM = 128
ROPE_THETA = 10000.0
EC_CAPACITY_FACTOR = 2
N_MIXERS = 3
VMEM_LIMIT = 56 * 1024 * 1024
ROW_TILE = 256
WIDE_TILE = 512
MM_TM, MM_TN = 1024, 1024
ADALN_TN = 1024
ATTN_TQ, ATTN_TK_MAX, ATTN_TC = 2048, 1536, 256
MOE_TC = 256
BOUND_SLACK = 1.01
MAX_FIXED_OFFSET = 60.0


def _params(*sem):
    return pltpu.CompilerParams(dimension_semantics=sem, vmem_limit_bytes=VMEM_LIMIT)


def _tile(n, t):
    t = min(n, t)
    assert n % t == 0, (n, t)
    return t


def _norm_mod(x, g, sh, sc):
    ms = jnp.mean(x * x, axis=-1, keepdims=True)
    return (x * lax.rsqrt(ms + NORM_EPS) * g) * (1.0 + sc) + sh


def _adaln_kernel(s_ref, w_ref, b_ref, o_ref):
    s = s_ref[...]
    s = s * jax.nn.sigmoid(s)
    o_ref[0] = jnp.dot(s, w_ref[0], preferred_element_type=F32) + b_ref[0]


def adaln(s8, ada_w, ada_b):
    depth, d, n6 = ada_w.shape
    tn = _tile(n6, ADALN_TN)
    return pl.pallas_call(
        _adaln_kernel,
        grid=(depth, n6 // tn),
        in_specs=[pl.BlockSpec((8, d), lambda l, j: (0, 0)),
                  pl.BlockSpec((1, d, tn), lambda l, j: (l, 0, j)),
                  pl.BlockSpec((1, 1, tn), lambda l, j: (l, 0, j))],
        out_specs=pl.BlockSpec((1, 8, tn), lambda l, j: (l, 0, j)),
        out_shape=jax.ShapeDtypeStruct((depth, 8, n6), F32),
        compiler_params=_params("parallel", "parallel"),
        name="adaln",
    )(s8, ada_w, ada_b.reshape(depth, 1, n6))


def _norm_mod_kernel(x_ref, g_ref, sh_ref, sc_ref, o_ref):
    o_ref[...] = _norm_mod(x_ref[...], g_ref[...], sh_ref[...], sc_ref[...]).astype(o_ref.dtype)


def norm_mod(x, g, sh, sc, dtype):
    n, d = x.shape
    tm = _tile(n, ROW_TILE)
    vec = pl.BlockSpec((1, d), lambda i: (0, 0))
    return pl.pallas_call(
        _norm_mod_kernel,
        grid=(n // tm,),
        in_specs=[pl.BlockSpec((tm, d), lambda i: (i, 0)), vec, vec, vec],
        out_specs=pl.BlockSpec((tm, d), lambda i: (i, 0)),
        out_shape=jax.ShapeDtypeStruct((n, d), dtype),
        compiler_params=_params("parallel"),
        name="norm_mod",
    )(x, g, sh, sc)


def _norm_mod_pair_kernel(x_ref, c_ref, g_ref, sh_ref, sc_ref, o_ref, *, nx):
    is_ctx = pl.program_id(0) >= nx
    src = jnp.where(is_ctx, c_ref[...], x_ref[...])
    sh = jnp.where(is_ctx, sh_ref[1:2], sh_ref[0:1])
    sc = jnp.where(is_ctx, sc_ref[1:2], sc_ref[0:1])
    o_ref[...] = _norm_mod(src, g_ref[...], sh, sc).astype(o_ref.dtype)


def norm_mod_pair(x, ctx, g, sh2, sc2, dtype):
    n, d = x.shape
    nc = ctx.shape[0]
    tm = _tile(nc, ROW_TILE)
    assert n % tm == 0
    nx = n // tm
    vec = pl.BlockSpec((1, d), lambda i: (0, 0))
    vec2 = pl.BlockSpec((2, d), lambda i: (0, 0))
    return pl.pallas_call(
        functools.partial(_norm_mod_pair_kernel, nx=nx),
        grid=(nx + nc // tm,),
        in_specs=[pl.BlockSpec((tm, d), lambda i: (jnp.minimum(i, nx - 1), 0)),
                  pl.BlockSpec((tm, d), lambda i: (jnp.maximum(i - nx, 0), 0)), vec, vec2, vec2],
        out_specs=pl.BlockSpec((tm, d), lambda i: (i, 0)),
        out_shape=jax.ShapeDtypeStruct((n + nc, d), dtype),
        compiler_params=_params("parallel"),
        name="norm_mod_pair",
    )(x, ctx, g, sh2, sc2)


def _rstd_kernel(x_ref, o_ref):
    x = x_ref[...]
    o_ref[...] = jnp.broadcast_to(lax.rsqrt(jnp.mean(x * x, axis=-1, keepdims=True) + NORM_EPS), o_ref.shape)


def row_rstd(x):
    n, d = x.shape
    tm = _tile(n, ROW_TILE)
    return pl.pallas_call(
        _rstd_kernel,
        grid=(n // tm,),
        in_specs=[pl.BlockSpec((tm, d), lambda i: (i, 0))],
        out_specs=pl.BlockSpec((tm, LANES), lambda i: (i, 0)),
        out_shape=jax.ShapeDtypeStruct((n, LANES), F32),
        compiler_params=_params("parallel"),
        name="row_rstd",
    )(x)


def _pool_kernel(cur_ref, prev_ref, next_ref, rc_ref, rp_ref, rn_ref, gs_ref, sh_ref, w_ref, ps_ref, g1_ref, o_ref,
                 *, n, tm):
    g = pl.program_id(0)
    i = pl.program_id(1)
    last = pl.num_programs(1) - 1
    reps = cur_ref.shape[1] // LANES

    def modulated(x, r):
        return x * jnp.concatenate([r] * reps, axis=1) * gs_ref[...] + sh_ref[...]

    t = i * tm + lax.broadcasted_iota(I32, (tm, 1), 0)

    for gi, win in enumerate(POOL_WINDOWS):
        @pl.when(g == gi)
        def _(win=win):
            x_cur = cur_ref[...]
            cur = modulated(x_cur, rc_ref[...])
            prev = jnp.where(i == 0, 0.0, modulated(prev_ref[...], rp_ref[...]))
            nxt = jnp.where(i == last, 0.0, modulated(next_ref[...], rn_ref[...]))
            ext = jnp.concatenate([prev, cur, nxt], axis=0)
            half = win // 2
            s = ext
            step = 1
            while step < win:
                m = s.shape[0] - step
                s = s[:m] + s[step:step + m]
                step *= 2
            wsum = s[8 - half:8 - half + tm]
            cnt = jnp.minimum(t + half, n) - jnp.maximum(t - half, 0)
            dlt = (wsum / cnt.astype(F32) - cur).astype(BF16)
            y = jnp.dot(dlt, w_ref[0], preferred_element_type=F32) * ps_ref[...]
            o_ref[...] = x_cur + g1_ref[...] * y


def pool_mixer(x, rstd, gs, sh, w, pscale, g1):
    n, d = x.shape
    ng, dg, _ = w.shape
    tm = _tile(n, WIDE_TILE)
    nb8 = n // 8
    kern = functools.partial(_pool_kernel, n=n, tm=tm)
    col = pl.BlockSpec((1, dg), lambda g, i: (0, g))

    def halo(width, sel):
        before = pl.BlockSpec((8, width), lambda g, i: (jnp.maximum(i * (tm // 8) - 1, 0), sel(g)))
        after = pl.BlockSpec((8, width), lambda g, i: (jnp.minimum((i + 1) * (tm // 8), nb8 - 1), sel(g)))
        return pl.BlockSpec((tm, width), lambda g, i: (i, sel(g))), before, after

    return pl.pallas_call(
        kern,
        grid=(ng, n // tm),
        in_specs=[*halo(dg, lambda g: g), *halo(LANES, lambda g: 0), col, col,
                  pl.BlockSpec((1, dg, dg), lambda g, i: (g, 0, 0)), col, col],
        out_specs=pl.BlockSpec((tm, dg), lambda g, i: (i, g)),
        out_shape=jax.ShapeDtypeStruct((n, d), F32),
        compiler_params=_params("parallel", "parallel"),
        name="pool_mixer",
    )(x, x, x, rstd, rstd, rstd, gs, sh, w, pscale, g1)


def _mm_call(kern, a, w, extra, extra_specs, out_shape, out_specs, tm, tn, name, scratch=(), m=None):
    k = a.shape[1]
    m = a.shape[0] if m is None else m
    n = w.shape[1]
    return pl.pallas_call(
        kern,
        grid=(pl.cdiv(m, tm), n // tn),
        in_specs=[pl.BlockSpec((tm, k), lambda i, j: (i, 0)),
                  pl.BlockSpec((k, tn), lambda i, j: (0, j))] + list(extra_specs),
        out_specs=out_specs,
        out_shape=out_shape,
        scratch_shapes=list(scratch),
        compiler_params=_params("parallel", "arbitrary"),
        name=name,
    )(a, w, *extra)


def _mm_t_kernel(a_ref, wt_ref, o_ref):
    o_ref[...] = lax.dot_general(wt_ref[...], a_ref[...], (((1,), (1,)), ((), ())),
                                 preferred_element_type=F32).astype(o_ref.dtype)


def mm_transposed(a, wt, dtype=BF16):
    m, k = a.shape
    n = wt.shape[0]
    tm, tn = min(m, MM_TM), _tile(n, MM_TN)
    return pl.pallas_call(
        _mm_t_kernel,
        grid=(pl.cdiv(m, tm), n // tn),
        in_specs=[pl.BlockSpec((tm, k), lambda i, j: (i, 0)),
                  pl.BlockSpec((tn, k), lambda i, j: (j, 0))],
        out_specs=pl.BlockSpec((tn, tm), lambda i, j: (j, i)),
        out_shape=jax.ShapeDtypeStruct((n, m), dtype),
        compiler_params=_params("parallel", "arbitrary"),
        name="mm_transposed",
    )(a, wt)


def _mm_qk_kernel(a_ref, w_ref, g_ref, cos_ref, sin_ref, o_ref, *, scale):
    y = jnp.dot(a_ref[...], w_ref[...], preferred_element_type=F32)
    tn = y.shape[1]
    cos = cos_ref[...]
    sin = sin_ref[...]
    ga_cos, ga_sin = cos * g_ref[0:1], sin * g_ref[0:1]
    gb_cos, gb_sin = cos * g_ref[1:2], sin * g_ref[1:2]
    comp0 = lax.broadcasted_iota(I32, (1, HEAD_DIM), 1) < HEAD_DIM // 2
    for hd in range(tn // (2 * HEAD_DIM)):
        a = y[:, 2 * hd * HEAD_DIM:(2 * hd + 1) * HEAD_DIM]
        b = y[:, (2 * hd + 1) * HEAD_DIM:(2 * hd + 2) * HEAD_DIM]
        sq = a * a + b * b
        ssq0 = jnp.sum(jnp.where(comp0, sq, 0.0), axis=-1, keepdims=True)
        ssq1 = jnp.sum(jnp.where(comp0, 0.0, sq), axis=-1, keepdims=True)
        rstd = jnp.where(comp0, lax.rsqrt(ssq0 * (1.0 / HEAD_DIM) + NORM_EPS),
                         lax.rsqrt(ssq1 * (1.0 / HEAD_DIM) + NORM_EPS)) * scale
        o_ref[:, 2 * hd * HEAD_DIM:(2 * hd + 1) * HEAD_DIM] = ((a * ga_cos - b * gb_sin) * rstd).astype(o_ref.dtype)
        o_ref[:, (2 * hd + 1) * HEAD_DIM:(2 * hd + 2) * HEAD_DIM] = ((b * gb_cos + a * ga_sin) * rstd).astype(o_ref.dtype)


def mm_qk(a, w, gain, cos, sin, scale, m=None):
    m = a.shape[0] if m is None else m
    n = w.shape[1]
    tm, tn = min(m, MM_TM), _tile(n, MM_TN)
    kern = functools.partial(_mm_qk_kernel, scale=scale)
    tab = pl.BlockSpec((tm, HEAD_DIM), lambda i, j: (i, 0))
    return _mm_call(kern, a, w, (gain, cos, sin),
                    (pl.BlockSpec((2, HEAD_DIM), lambda i, j: (0, 0)), tab, tab),
                    jax.ShapeDtypeStruct((m, n), BF16),
                    pl.BlockSpec((tm, tn), lambda i, j: (i, j)), tm, tn, "mm_qk", m=m)


def _mm_gelu_kernel(a_ref, w_ref, o_ref, ssq_ref, *, nj_half):
    j = pl.program_id(1)
    y = jax.nn.gelu(jnp.dot(a_ref[...], w_ref[...], preferred_element_type=F32))
    o_ref[...] = y.astype(o_ref.dtype)

    @pl.when(j == nj_half)
    def _():
        ssq_ref[...] = jnp.zeros_like(ssq_ref)

    @pl.when(j >= nj_half)
    def _():
        ssq_ref[...] += jnp.sum(y * y, axis=-1, keepdims=True)


def mm_gelu(a, w):
    m, _ = a.shape
    n = w.shape[1]
    tm, tn = min(m, MM_TM), _tile(n // 2, MM_TN)
    kern = functools.partial(_mm_gelu_kernel, nj_half=(n // 2) // tn)
    return _mm_call(kern, a, w, (), (),
                    (jax.ShapeDtypeStruct((m, n), BF16), jax.ShapeDtypeStruct((m, LANES), F32)),
                    (pl.BlockSpec((tm, tn), lambda i, j: (i, j)),
                     pl.BlockSpec((tm, LANES), lambda i, j: (i, 0))), tm, tn, "mm_gelu")


def _mm_resid_kernel(a_ref, w_ref, x_ref, g_ref, o_ref):
    y = jnp.dot(a_ref[...], w_ref[...], preferred_element_type=F32)
    o_ref[...] = x_ref[...] + g_ref[...] * y


def mm_resid(a, w, x, g1):
    m, _ = a.shape
    n = w.shape[1]
    tm, tn = min(m, MM_TM), _tile(n, MM_TN)
    blk = pl.BlockSpec((tm, tn), lambda i, j: (i, j))
    return _mm_call(_mm_resid_kernel, a, w, (x, g1),
                    (blk, pl.BlockSpec((1, tn), lambda i, j: (0, j))),
                    jax.ShapeDtypeStruct((m, n), F32), blk, tm, tn, "mm_resid")


def _chunk_gate_kernel(u_ref, v_ref, ssq_ref, vg_ref, ws_ref, bs_ref, z_ref, *, width):
    tm = u_ref.shape[0]
    rstd = lax.rsqrt(ssq_ref[:, :1] * (1.0 / width) + NORM_EPS)

    def body(g, carry):
        col = pl.multiple_of(g * LANES, LANES)
        vg = vg_ref[:, pl.ds(col, LANES)]
        chunks = [slice(c * CHUNK, (c + 1) * CHUNK) for c in range(tm // CHUNK)]
        vn = jnp.concatenate([(v_ref[rows, pl.ds(col, LANES)].astype(F32) * rstd[rows] * vg).astype(BF16)
                              for rows in chunks], axis=1)
        sv = jnp.dot(ws_ref[g], vn, preferred_element_type=F32)
        bias = bs_ref[:, pl.ds(col, LANES)]
        for c, rows in enumerate(chunks):
            gate = sv[:, c * LANES:(c + 1) * LANES] + bias
            z_ref[rows, pl.ds(col, LANES)] = (u_ref[rows, pl.ds(col, LANES)].astype(F32) * gate).astype(BF16)
        return carry

    lax.fori_loop(0, width // LANES, body, 0, unroll=2)


def chunk_gate(uv, ssq, vg, ws, bs_full):
    n, w2 = uv.shape
    width = w2 // 2
    tm = _tile(n, WIDE_TILE)
    return pl.pallas_call(
        functools.partial(_chunk_gate_kernel, width=width),
        grid=(n // tm,),
        in_specs=[pl.BlockSpec((tm, width), lambda i: (i, 0)),
                  pl.BlockSpec((tm, width), lambda i: (i, 1)),
                  pl.BlockSpec((tm, LANES), lambda i: (i, 0)),
                  pl.BlockSpec((1, width), lambda i: (0, 0)),
                  pl.BlockSpec(ws.shape, lambda i: (0, 0, 0)),
                  pl.BlockSpec((CHUNK, width), lambda i: (0, 0))],
        out_specs=pl.BlockSpec((tm, width), lambda i: (i, 0)),
        out_shape=jax.ShapeDtypeStruct((n, width), BF16),
        compiler_params=_params("parallel"),
        name="chunk_gate",
    )(uv, uv, ssq, vg, ws, bs_full)


def _attn_kernel(lamv_ref, off_ref, q_ref, k_ref, vt_ref, sg_ref, o_ref, m_scr, l_scr, acc_scr, p_scr,
                 *, lam_init, tc, bounded):
    ki = pl.program_id(2)
    tk = k_ref.shape[0]
    nbuf = p_scr.shape[0]
    nchunk = q_ref.shape[0] // tc
    lane = lax.broadcasted_iota(I32, (1, 2 * HEAD_DIM), 1) % HEAD_DIM
    qcs = [q_ref[...] * jnp.where((lane < HEAD_DIM // 2) == (c == 0), 1.0, 0.0).astype(BF16) for c in range(2)]

    @pl.when(ki == 0)
    def _():
        l_scr[...] = jnp.zeros_like(l_scr)
        acc_scr[...] = jnp.zeros_like(acc_scr)
        if not bounded:
            m_scr[...] = jnp.full_like(m_scr, -jnp.inf)

    vt = vt_ref[...]
    kt = k_ref[...]
    scores = {}
    for c in range(2):
        for r in range(nchunk):
            scores[r, c] = lax.dot_general(kt, qcs[c][r * tc:(r + 1) * tc], (((1,), (1,)), ((), ())),
                                           preferred_element_type=F32)
    for r in range(nchunk):
        qs = slice(r * tc, (r + 1) * tc)
        for c in range(2):
            buf = (2 * r + c) % nbuf
            s = scores[r, c]
            if bounded:
                m_new = off_ref[:, :1]
            else:
                m_prev = m_scr[c, :, qs]
                m_new = jnp.maximum(m_prev, jnp.max(s, axis=0, keepdims=True))
            psum = jnp.zeros((16, tc), F32)
            for g in range(tk // 16):
                pg = jnp.exp2(s[16 * g:16 * (g + 1)] - m_new)
                psum = psum + pg
                p_scr[buf, 16 * g:16 * (g + 1), :] = pg.astype(BF16)
            lsum = jnp.sum(psum, axis=0, keepdims=True)
            pv = jnp.dot(vt, p_scr[buf], preferred_element_type=F32)
            if bounded:
                l_scr[c, :, qs] += lsum
                acc_scr[c, :, qs] += pv
            else:
                alpha = jnp.exp2(m_prev - m_new)
                l_scr[c, :, qs] = alpha * l_scr[c, :, qs] + lsum
                acc_scr[c, :, qs] = alpha * acc_scr[c, :, qs] + pv
                m_scr[c, :, qs] = m_new

    @pl.when(ki == pl.num_programs(2) - 1)
    def _():
        lv = lamv_ref[...]
        lam = (jnp.exp(jnp.sum(lv[0:1] * lv[1:2], axis=-1, keepdims=True))
               - jnp.exp(jnp.sum(lv[2:3] * lv[3:4], axis=-1, keepdims=True)) + lam_init)
        ot = acc_scr[0] / l_scr[0] - lam * (acc_scr[1] / l_scr[1])
        ot = ot * lax.rsqrt(jnp.mean(ot * ot, axis=0, keepdims=True) + NORM_EPS)
        o_ref[...] = (ot.T * (sg_ref[...] * (1.0 - lam_init))).astype(o_ref.dtype)


def _key_tile(nk, cap):
    best = LANES
    for t in range(LANES, cap + 1, LANES):
        if nk % t == 0:
            best = t
    return best


def diff_attention(q, k, vt, lamv, sub_g, lam_init, score_bound):
    n, d = q.shape
    nk = k.shape[0]
    hw = 2 * HEAD_DIM
    heads = d // hw
    tq = _tile(n, ATTN_TQ)
    tk = _key_tile(nk, ATTN_TK_MAX)
    tc = min(tq, ATTN_TC)

    safe = score_bound <= MAX_FIXED_OFFSET
    off = jnp.full((1, LANES), score_bound, F32)

    def call(bounded):
        kern = functools.partial(_attn_kernel, lam_init=lam_init, tc=tc, bounded=bounded)
        return pl.pallas_call(
            kern,
            grid=(heads, n // tq, nk // tk),
            in_specs=[pl.BlockSpec((4, HEAD_DIM), lambda h, i, j: (0, 0)),
                      pl.BlockSpec((1, LANES), lambda h, i, j: (0, 0)),
                      pl.BlockSpec((tq, hw), lambda h, i, j: (i, h)),
                      pl.BlockSpec((tk, hw), lambda h, i, j: (j, h)),
                      pl.BlockSpec((hw, tk), lambda h, i, j: (h, j)),
                      pl.BlockSpec((1, hw), lambda h, i, j: (0, 0))],
            out_specs=pl.BlockSpec((tq, hw), lambda h, i, j: (i, h)),
            out_shape=jax.ShapeDtypeStruct((n, d), BF16),
            scratch_shapes=[pltpu.VMEM((2, 1, tq), F32), pltpu.VMEM((2, 1, tq), F32),
                            pltpu.VMEM((2, hw, tq), F32), pltpu.VMEM((2 * (tq // tc), tk, tc), BF16)],
            compiler_params=_params("parallel", "parallel", "arbitrary"),
            name="diff_attention_bounded" if bounded else "diff_attention_online",
        )(lamv, off, q, k, vt, sub_g)

    return lax.cond(safe, lambda: call(True), lambda: call(False))


def _router_kernel(x_ref, g_ref, sh_ref, sc_ref, wrt_ref, h_ref, aff_ref):
    h = _norm_mod(x_ref[...], g_ref[...], sh_ref[...], sc_ref[...])
    half = h.shape[1] // 2
    h_hi = h.astype(BF16)
    h_hi32 = h_hi.astype(F32)
    bits = pltpu.bitcast(h_hi32, U32)
    h_ref[...] = (bits[:, :half] >> 16) | (bits[:, half:] & jnp.uint32(0xFFFF0000))
    h_lo = (h - h_hi32).astype(BF16)
    w = wrt_ref[...]
    w_hi = w.astype(BF16)
    w_lo = (w - w_hi.astype(F32)).astype(BF16)
    dims = (((1,), (1,)), ((), ()))
    logits = (lax.dot_general(w_hi, h_hi, dims, preferred_element_type=F32)
              + lax.dot_general(w_lo, h_hi, dims, preferred_element_type=F32)
              + lax.dot_general(w_hi, h_lo, dims, preferred_element_type=F32))
    ex = jnp.exp(logits - jnp.max(logits, axis=0, keepdims=True))
    aff_ref[...] = ex / jnp.sum(ex, axis=0, keepdims=True)


def router(x, g, sh, sc, w_r_t):
    n, d = x.shape
    e = w_r_t.shape[0]
    tm = _tile(n, ROW_TILE)
    vec = pl.BlockSpec((1, d), lambda i: (0, 0))
    return pl.pallas_call(
        _router_kernel,
        grid=(n // tm,),
        in_specs=[pl.BlockSpec((tm, d), lambda i: (i, 0)), vec, vec, vec,
                  pl.BlockSpec((e, d), lambda i: (0, 0))],
        out_specs=(pl.BlockSpec((tm, d // 2), lambda i: (i, 0)), pl.BlockSpec((e, tm), lambda i: (0, i))),
        out_shape=(jax.ShapeDtypeStruct((n, d // 2), U32), jax.ShapeDtypeStruct((e, n), F32)),
        compiler_params=_params("parallel"),
        name="router",
    )(x, g, sh, sc, w_r_t)


def _split3(a):
    hi = a.astype(BF16)
    r1 = a - hi.astype(F32)
    mid = r1.astype(BF16)
    lo = (r1 - mid.astype(F32)).astype(BF16)
    return hi, mid, lo


def _topk_kernel(abt_ref, atb_ref, idx_ref, gate_ref, thr_scr, cut_scr, *, cap, cap_pad):
    ne, nb, _ = abt_ref.shape
    ntok = nb * LANES
    bits = pltpu.bitcast(abt_ref[...], I32)
    tok_bt = (lax.broadcasted_iota(I32, (1, nb, LANES), 1) * LANES
              + lax.broadcasted_iota(I32, (1, nb, LANES), 2))

    def count(ones):
        c = jnp.sum(ones, axis=2, keepdims=True)
        return jnp.sum(c, axis=1, keepdims=True)

    def thr_body(_, carry):
        lo, hi = carry
        mid = lo + ((hi - lo + 1) >> 1)
        ok = count(jnp.where(bits >= mid, 1, 0)) >= cap
        return jnp.where(ok, mid, lo), jnp.where(ok, hi, mid - 1)

    lo0 = jnp.zeros((ne, 1, 1), I32)
    hi0 = jnp.full((ne, 1, 1), 0x7F800000, I32)
    thr, _ = lax.fori_loop(0, 31, thr_body, (lo0, hi0))

    tok_eq = jnp.where(bits == thr, tok_bt, ntok)
    need = cap - count(jnp.where(bits > thr, 1, 0))

    def cut_body(_, carry):
        lo, hi = carry
        mid = (lo + hi) >> 1
        ok = count(jnp.where(tok_eq < mid, 1, 0)) >= need
        return jnp.where(ok, lo, mid), jnp.where(ok, mid, hi)

    _, cut = lax.fori_loop(0, int(math.log2(ntok)) + 1, cut_body,
                           (jnp.zeros((ne, 1, 1), I32), jnp.full((ne, 1, 1), ntok, I32)))
    thr_scr[...] = jnp.broadcast_to(thr, thr_scr.shape)
    cut_scr[...] = jnp.broadcast_to(cut, cut_scr.shape)

    r_i = lax.broadcasted_iota(I32, (LANES, LANES), 0)
    c_i = lax.broadcasted_iota(I32, (LANES, LANES), 1)
    tri_t = jnp.where(c_i <= r_i, 1.0, 0.0).astype(BF16)
    rb_i = lax.broadcasted_iota(I32, (nb, nb), 0)
    cb_i = lax.broadcasted_iota(I32, (nb, nb), 1)
    tri_b = jnp.where(cb_i <= rb_i, 1.0, 0.0).astype(BF16)
    tok_b = lax.broadcasted_iota(I32, (nb, LANES), 0) * LANES + lax.broadcasted_iota(I32, (nb, LANES), 1)
    tok_t = lax.broadcasted_iota(I32, (LANES, nb), 1) * LANES + lax.broadcasted_iota(I32, (LANES, nb), 0)
    slot = lax.broadcasted_iota(I32, (1, cap_pad), 1).astype(F32)
    blk_iota = lax.broadcasted_iota(I32, (nb, 1), 0).astype(F32)
    lane_iota = lax.broadcasted_iota(I32, (LANES, 1), 0).astype(F32)

    def select(b, tok, th, ct):
        return jnp.where(b > th, 1.0, jnp.where(b == th, jnp.where(tok < ct, 1.0, 0.0), 0.0))

    def per_expert(e, carry):
        th = thr_scr[e][:1, :1]
        ct = cut_scr[e][:1, :1]
        a_tb = atb_ref[e]
        sel_bt = select(pltpu.bitcast(abt_ref[e], I32), tok_b, th, ct)
        sel_tb = select(pltpu.bitcast(a_tb, I32), tok_t, th, ct)
        cw_t = jnp.dot(tri_t, sel_tb.astype(BF16), preferred_element_type=F32)
        tot = jnp.sum(sel_bt, axis=1, keepdims=True)
        incl = jnp.dot(tri_b, jnp.broadcast_to(tot, (nb, LANES)).astype(BF16),
                       preferred_element_type=F32)[:, :1]
        excl = incl - tot
        blk = jnp.sum(jnp.where(incl <= slot, 1.0, 0.0), axis=0, keepdims=True)
        onehot = jnp.where(blk_iota == blk, 1.0, 0.0)
        oh16 = onehot.astype(BF16)
        g_t = jnp.dot(cw_t.astype(BF16), oh16, preferred_element_type=F32)
        base = jnp.sum(onehot * excl, axis=0, keepdims=True)
        tl = jnp.sum(jnp.where(g_t + base <= slot, 1.0, 0.0), axis=0, keepdims=True)
        idx_ref[pl.ds(e, 1), :] = (blk * LANES + tl).astype(I32)
        hi, mid, lo = _split3(a_tb)
        rows = (jnp.dot(hi, oh16, preferred_element_type=F32)
                + jnp.dot(mid, oh16, preferred_element_type=F32)
                + jnp.dot(lo, oh16, preferred_element_type=F32))
        gate_ref[pl.ds(e, 1), :] = jnp.sum(jnp.where(lane_iota == tl, rows, 0.0), axis=0, keepdims=True)
        return carry

    lax.fori_loop(0, ne, per_expert, 0)


def topk_select(aff_t, cap):
    ne, n = aff_t.shape
    nb = max(LANES, -(-n // LANES))
    nb = -(-nb // LANES) * LANES
    cap_pad = max(LANES, cap)
    padded = jnp.pad(aff_t, ((0, 0), (0, nb * LANES - n)), constant_values=-1.0)
    a_bt = padded.reshape(ne, nb, LANES)
    a_tb = jnp.swapaxes(a_bt, 1, 2)
    kern = functools.partial(_topk_kernel, cap=cap, cap_pad=cap_pad)
    idx, gate = pl.pallas_call(
        kern,
        out_shape=(jax.ShapeDtypeStruct((ne, cap_pad), I32), jax.ShapeDtypeStruct((ne, cap_pad), F32)),
        scratch_shapes=[pltpu.VMEM((ne, 8, LANES), I32), pltpu.VMEM((ne, 8, LANES), I32)],
        compiler_params=pltpu.CompilerParams(vmem_limit_bytes=VMEM_LIMIT),
        name="topk_select",
    )(a_bt, a_tb)
    return idx[:, :cap], gate[:, :cap]


def _moe_kernel(idx_ref, gate_ref, g2_ref, h_hbm, wg_hbm, wu_hbm, wd_hbm, xin_hbm, out_hbm,
                xs, ab, y_scr, wg_f32, wu_f32, wd_f32, wgu, wd, sem_x, sem_a, sem_o, sem_w, *, tc, layer):
    del xin_hbm
    nct = pl.num_programs(1)
    ct = pl.program_id(1)
    s = pl.program_id(0) * nct + ct
    last = pl.num_programs(0) * nct - 1
    slot = s % 2
    base = s * tc
    nxt = jnp.minimum(s + 1, last) * tc

    def x_copy(b, r, sl):
        return pltpu.make_async_copy(h_hbm.at[pl.ds(idx_ref[b + r], 1)], xs.at[sl, pl.ds(r, 1)], sem_x.at[sl])

    def a_copy(r):
        return pltpu.make_async_copy(out_hbm.at[pl.ds(idx_ref[base + r], 1)], ab.at[slot, pl.ds(r, 1)],
                                     sem_a.at[slot])

    def o_copy(b, r, sl):
        return pltpu.make_async_copy(ab.at[sl, pl.ds(r, 1)], out_hbm.at[pl.ds(idx_ref[b + r], 1)], sem_o.at[sl])

    def rows(fn):
        def body(r, c):
            fn(r)
            return c
        lax.fori_loop(0, tc, body, 0, unroll=8)

    ne = pl.num_programs(0)
    expert = pl.program_id(0)

    def w_copies(e):
        return (pltpu.make_async_copy(wg_hbm.at[layer, e], wg_f32, sem_w.at[0]),
                pltpu.make_async_copy(wu_hbm.at[layer, e], wu_f32, sem_w.at[1]),
                pltpu.make_async_copy(wd_hbm.at[layer, e], wd_f32, sem_w.at[2]))

    @pl.when(s == 0)
    def _():
        for cp in w_copies(0):
            cp.start()
        rows(lambda r: x_copy(0, r, 0).start())

    @pl.when(ct == 0)
    def _():
        for cp in w_copies(expert):
            cp.wait()
        f = wd.shape[0]
        wgu[:, :f] = wg_f32[...].astype(BF16)
        wgu[:, f:] = wu_f32[...].astype(BF16)
        wd[...] = wd_f32[...].astype(BF16)

    @pl.when(jnp.logical_and(ct == 0, expert + 1 < ne))
    def _():
        for cp in w_copies(expert + 1):
            cp.start()

    def x_wait(sl):
        pltpu.make_async_copy(h_hbm.at[pl.ds(0, tc)], xs.at[sl], sem_x.at[sl]).wait()

    def o_wait(sl):
        pltpu.make_async_copy(ab.at[sl], out_hbm.at[pl.ds(0, tc)], sem_o.at[sl]).wait()

    x_wait(slot)

    @pl.when(jnp.logical_and(ct == 0, s > 0))
    def _():
        o_wait(1 - slot)

    for r in range(tc):
        x_copy(nxt, r, 1 - slot).start()
        a_copy(r).start()
    xu = xs[slot]
    x_lo = pltpu.bitcast(xu << 16, F32).astype(BF16)
    x_hi = pltpu.bitcast(xu & jnp.uint32(0xFFFF0000), F32).astype(BF16)
    half = xu.shape[1]
    ab2 = (jnp.dot(x_lo, wgu[:half], preferred_element_type=F32)
           + jnp.dot(x_hi, wgu[half:], preferred_element_type=F32))
    f = ab2.shape[1] // 2
    a, b = ab2[:, :f], ab2[:, f:]
    gate = jnp.concatenate([gate_ref[0]] * (f // LANES), axis=1)
    hm = (a * jax.nn.sigmoid(a) * b * gate).astype(BF16)
    y_scr[...] = jnp.dot(hm, wd[...], preferred_element_type=F32) * g2_ref[...]

    pltpu.make_async_copy(out_hbm.at[pl.ds(0, tc)], ab.at[slot], sem_a.at[slot]).wait()

    @pl.when(ct > 0)
    def _():
        o_wait(1 - slot)

    ab[slot] = ab[slot] + y_scr[...]
    for r in range(tc):
        o_copy(base, r, slot).start()

    @pl.when(s == last)
    def _():
        o_wait(slot)
        x_wait(1 - slot)


def moe_apply(h, x, idx, gate, w_gate, w_up, w_down, layer, g2):
    n, d = x.shape
    ne, cap = idx.shape
    f = w_down.shape[2]
    assert f % LANES == 0
    tc = _tile(cap, MOE_TC)
    gate_b = jnp.broadcast_to(gate[:, :, None], (ne, cap, LANES))
    kern = functools.partial(_moe_kernel, tc=tc, layer=layer)
    hbm = pl.BlockSpec(memory_space=pl.ANY)
    grid_spec = pltpu.PrefetchScalarGridSpec(
        num_scalar_prefetch=1,
        grid=(ne, cap // tc),
        in_specs=[pl.BlockSpec((1, tc, LANES), lambda e, c, idx: (e, c, 0)),
                  pl.BlockSpec((1, d), lambda e, c, idx: (0, 0)),
                  hbm, hbm, hbm, hbm, hbm],
        out_specs=hbm,
        scratch_shapes=[pltpu.VMEM((2, tc, d // 2), U32), pltpu.VMEM((2, tc, d), F32), pltpu.VMEM((tc, d), F32),
                        pltpu.VMEM((d, f), F32), pltpu.VMEM((d, f), F32), pltpu.VMEM((f, d), F32),
                        pltpu.VMEM((d, 2 * f), BF16), pltpu.VMEM((f, d), BF16),
                        pltpu.SemaphoreType.DMA((2,)), pltpu.SemaphoreType.DMA((2,)),
                        pltpu.SemaphoreType.DMA((2,)), pltpu.SemaphoreType.DMA((3,))],
    )
    return pl.pallas_call(
        kern,
        grid_spec=grid_spec,
        out_shape=jax.ShapeDtypeStruct((n, d), F32),
        input_output_aliases={7: 0},
        compiler_params=_params("arbitrary", "arbitrary"),
        name="moe_apply",
    )(idx.reshape(-1), gate_b, g2, h, w_gate, w_up, w_down, x)


def moe_layer(x, g, sh, sc, g2, w_r, w_gate, w_up, w_down, layer):
    n = x.shape[0]
    ne = w_r.shape[1]
    h, aff_t = router(x, g, sh, sc, w_r.T)
    idx, gate = topk_select(aff_t, EC_CAPACITY_FACTOR * n // ne)
    return moe_apply(h, x, idx, gate, w_gate, w_up, w_down, layer, g2)


def _rope_perm(width):
    quarter = HEAD_DIM // 4
    starts = jnp.array([0, 2, 4, 6, 1, 3, 5, 7]) * quarter
    blk = (starts[:, None] + jnp.arange(quarter)[None, :]).reshape(-1)
    return (jnp.arange(0, width, 2 * HEAD_DIM)[:, None] + blk[None, :]).reshape(-1)


def _permute_heads(w):
    rows, width = w.shape
    quarter = HEAD_DIM // 4
    w6 = w.reshape(rows, width // (2 * HEAD_DIM), 2, 2, 2, quarter)
    return jnp.transpose(w6, (0, 1, 4, 2, 3, 5)).reshape(rows, width)


def _rope_tables(n, n_ctx):
    quarter = HEAD_DIM // 4
    t = jnp.arange(n, dtype=I32)
    freqs = ROPE_THETA ** (-jnp.arange(quarter, dtype=F32) / quarter)
    ang_r = (t // GRID_W).astype(F32)[:, None] * freqs
    ang_c = (t % GRID_W).astype(F32)[:, None] * freqs
    cos = jnp.concatenate([jnp.cos(ang_r), jnp.cos(ang_c)] * 2, axis=1)
    sin = jnp.concatenate([jnp.sin(ang_r), jnp.sin(ang_c)] * 2, axis=1)
    cos = jnp.concatenate([cos, jnp.ones((n_ctx, HEAD_DIM), F32)], axis=0)
    sin = jnp.concatenate([sin, jnp.zeros((n_ctx, HEAD_DIM), F32)], axis=0)
    return cos, sin


def kernel(x, c, ctx, c_ctx, ada_w, ada_b, norm1_g, norm2_g, router_w, exp_w_gate, exp_w_up, exp_w_down,
           pool_w, pool_scale, gm_w_in, gm_v_g, gm_w_s, gm_b_s, gm_w_out, da_w_q, da_w_k, da_w_v, da_w_o,
           da_q_g, da_k_g, da_lam_q1, da_lam_k1, da_lam_q2, da_lam_k2, da_sub_g):
    bsz, n, d = x.shape
    assert bsz == 1 and c.shape[0] == 1
    n_ctx = ctx.shape[1]
    depth = ada_w.shape[0]
    xs, cs = x[0], ctx[0]

    s8 = jnp.concatenate([c, c_ctx[None], jnp.zeros((6, d), F32)], axis=0)
    mod = adaln(s8, ada_w, ada_b)

    def mods(i, row):
        return [mod[i, row:row + 1, k * d:(k + 1) * d] for k in range(6)]

    def row(v):
        return v.reshape(1, -1)

    for i in range(depth):
        kind, slot = i % N_MIXERS, i // N_MIXERS
        keep_ctx = any(j % N_MIXERS == 2 for j in range(i + 1, depth))
        sh1, sc1, g1, sh2, sc2, g2 = mods(i, 0)
        csh1, csc1, cg1, csh2, csc2, cg2 = mods(i, 1)
        n1, n2 = row(norm1_g[i]), row(norm2_g[i])
        streams = [(xs, sh1, sc1, g1)] + ([(cs, csh1, csc1, cg1)] if keep_ctx else [])

        if kind == 0:
            w = pool_w[slot].astype(BF16)
            outs = [pool_mixer(s, row_rstd(s), n1 * (1.0 + sc), sh, w, row(pool_scale[slot]), g)
                    for s, sh, sc, g in streams]
        elif kind == 1:
            w_in = gm_w_in[slot].astype(BF16)
            w_out = gm_w_out[slot].astype(BF16)
            w_s = gm_w_s[slot].astype(BF16)
            width = w_out.shape[0]
            bs_full = jnp.repeat(gm_b_s[slot].T, width // gm_w_s.shape[1], axis=1)
            outs = []
            for s, sh, sc, g in streams:
                uv, ssq = mm_gelu(norm_mod(s, n1, sh, sc, BF16), w_in)
                z = chunk_gate(uv, ssq, row(gm_v_g[slot]), w_s, bs_full)
                outs.append(mm_resid(z, w_out, s, g))
        else:
            assert not keep_ctx
            lam_init = 0.8 - 0.6 * math.exp(-0.3 * i)
            h_all = norm_mod_pair(xs, cs, n1, jnp.concatenate([sh1, csh1]), jnp.concatenate([sc1, csc1]), BF16)
            cos, sin = _rope_tables(n, n_ctx)
            gperm = (_rope_perm(2 * HEAD_DIM) % HEAD_DIM).reshape(2, HEAD_DIM)
            q_scale = HEAD_DIM ** -0.5 * math.log2(math.e)
            q = mm_qk(h_all, _permute_heads(da_w_q[slot]).astype(BF16), da_q_g[slot][gperm], cos, sin, q_scale, m=n)
            k = mm_qk(h_all, _permute_heads(da_w_k[slot]).astype(BF16), da_k_g[slot][gperm], cos, sin, 1.0)
            vt = mm_transposed(h_all, da_w_v[slot].T.astype(BF16))
            lamv = jnp.stack([da_lam_q1[slot], da_lam_k1[slot], da_lam_q2[slot], da_lam_k2[slot]])
            score_bound = (HEAD_DIM * q_scale * BOUND_SLACK) * jnp.max(jnp.abs(da_q_g[slot])) * jnp.max(jnp.abs(da_k_g[slot]))
            o = diff_attention(q, k, vt, lamv, row(da_sub_g[slot]), lam_init, score_bound)
            outs = [mm_resid(o, da_w_o[slot].astype(BF16), xs, g1)]

        xs = moe_layer(outs[0], n2, sh2, sc2, g2, router_w[i], exp_w_gate, exp_w_up, exp_w_down, i)
        if keep_ctx:
            cs = moe_layer(outs[1], n2, csh2, csc2, cg2, router_w[i], exp_w_gate, exp_w_up, exp_w_down, i)
    return xs[None]
```

```python
import functools
import math

import jax
import jax.numpy as jnp
from jax import lax
from jax.experimental import pallas as pl
from jax.experimental.pallas import tpu as pltpu

F32 = jnp.float32
BF16 = jnp.bfloat16
I32 = jnp.int32
U32 = jnp.uint32

NORM_EPS = 1e-6
LANES = 128
SUBLANES = 8
BF16_ROWS = 16
GRID_W = 64
CHUNK = 128
POOL_WINDOWS = (2, 4, 8, 16)
HEAD_DIM = 128
ROPE_THETA = 10000.0
EC_CAPACITY_FACTOR = 2
N_MIXERS = 3
VMEM_LIMIT = 56 * 1024 * 1024
ROW_TILE = 256
WIDE_TILE = 512
MM_TM, MM_TN = 1024, 1024
ADALN_TN = 1024
ATTN_TQ, ATTN_TK_MAX, ATTN_TC = 2048, 1664, 256
MOE_TC = 256
BOUND_SLACK = 1.01
MAX_FIXED_OFFSET = 60.0


def _params(*sem):
    return pltpu.CompilerParams(dimension_semantics=sem, vmem_limit_bytes=VMEM_LIMIT)


def _tile(n, t):
    t = min(n, t)
    assert n % t == 0, (n, t)
    return t


def _norm_mod(x, g, sh, sc):
    ms = jnp.mean(x * x, axis=-1, keepdims=True)
    return (x * lax.rsqrt(ms + NORM_EPS) * g) * (1.0 + sc) + sh


def _adaln_kernel(s_ref, w_ref, b_ref, o_ref):
    s = s_ref[...]
    s = s * jax.nn.sigmoid(s)
    o_ref[0] = jnp.dot(s, w_ref[0], preferred_element_type=F32) + b_ref[0]


def adaln(s8, ada_w, ada_b):
    depth, d, n6 = ada_w.shape
    tn = _tile(n6, ADALN_TN)
    return pl.pallas_call(
        _adaln_kernel,
        grid=(depth, n6 // tn),
        in_specs=[pl.BlockSpec((8, d), lambda l, j: (0, 0)),
                  pl.BlockSpec((1, d, tn), lambda l, j: (l, 0, j)),
                  pl.BlockSpec((1, 1, tn), lambda l, j: (l, 0, j))],
        out_specs=pl.BlockSpec((1, 8, tn), lambda l, j: (l, 0, j)),
        out_shape=jax.ShapeDtypeStruct((depth, 8, n6), F32),
        compiler_params=_params("parallel", "parallel"),
        name="adaln",
    )(s8, ada_w, ada_b.reshape(depth, 1, n6))


def _norm_mod_kernel(x_ref, g_ref, sh_ref, sc_ref, o_ref):
    o_ref[...] = _norm_mod(x_ref[...], g_ref[...], sh_ref[...], sc_ref[...]).astype(o_ref.dtype)


def norm_mod(x, g, sh, sc, dtype):
    n, d = x.shape
    tm = _tile(n, ROW_TILE)
    vec = pl.BlockSpec((1, d), lambda i: (0, 0))
    return pl.pallas_call(
        _norm_mod_kernel,
        grid=(n // tm,),
        in_specs=[pl.BlockSpec((tm, d), lambda i: (i, 0)), vec, vec, vec],
        out_specs=pl.BlockSpec((tm, d), lambda i: (i, 0)),
        out_shape=jax.ShapeDtypeStruct((n, d), dtype),
        compiler_params=_params("parallel"),
        name="norm_mod",
    )(x, g, sh, sc)


def _norm_mod_pair_kernel(x_ref, c_ref, g_ref, sh_ref, sc_ref, o_ref, *, nx):
    is_ctx = pl.program_id(0) >= nx
    src = jnp.where(is_ctx, c_ref[...], x_ref[...])
    sh = jnp.where(is_ctx, sh_ref[1:2], sh_ref[0:1])
    sc = jnp.where(is_ctx, sc_ref[1:2], sc_ref[0:1])
    o_ref[...] = _norm_mod(src, g_ref[...], sh, sc).astype(o_ref.dtype)


def norm_mod_pair(x, ctx, g, sh2, sc2, dtype):
    n, d = x.shape
    nc = ctx.shape[0]
    tm = _tile(nc, ROW_TILE)
    assert n % tm == 0
    nx = n // tm
    vec = pl.BlockSpec((1, d), lambda i: (0, 0))
    vec2 = pl.BlockSpec((2, d), lambda i: (0, 0))
    return pl.pallas_call(
        functools.partial(_norm_mod_pair_kernel, nx=nx),
        grid=(nx + nc // tm,),
        in_specs=[pl.BlockSpec((tm, d), lambda i: (jnp.minimum(i, nx - 1), 0)),
                  pl.BlockSpec((tm, d), lambda i: (jnp.maximum(i - nx, 0), 0)), vec, vec2, vec2],
        out_specs=pl.BlockSpec((tm, d), lambda i: (i, 0)),
        out_shape=jax.ShapeDtypeStruct((n + nc, d), dtype),
        compiler_params=_params("parallel"),
        name="norm_mod_pair",
    )(x, ctx, g, sh2, sc2)


def _rstd_kernel(x_ref, o_ref):
    x = x_ref[...]
    o_ref[...] = jnp.broadcast_to(lax.rsqrt(jnp.mean(x * x, axis=-1, keepdims=True) + NORM_EPS), o_ref.shape)


def row_rstd(x):
    n, d = x.shape
    tm = _tile(n, ROW_TILE)
    return pl.pallas_call(
        _rstd_kernel,
        grid=(n // tm,),
        in_specs=[pl.BlockSpec((tm, d), lambda i: (i, 0))],
        out_specs=pl.BlockSpec((tm, LANES), lambda i: (i, 0)),
        out_shape=jax.ShapeDtypeStruct((n, LANES), F32),
        compiler_params=_params("parallel"),
        name="row_rstd",
    )(x)


def _pool_kernel(cur_ref, prev_ref, next_ref, rc_ref, rp_ref, rn_ref, gs_ref, sh_ref, w_ref, ps_ref, g1_ref, o_ref,
                 *, n, tm):
    g = pl.program_id(0)
    i = pl.program_id(1)
    last = pl.num_programs(1) - 1
    reps = cur_ref.shape[1] // LANES

    def modulated(x, r):
        return x * jnp.concatenate([r] * reps, axis=1) * gs_ref[...] + sh_ref[...]

    t = i * tm + lax.broadcasted_iota(I32, (tm, 1), 0)

    for gi, win in enumerate(POOL_WINDOWS):
        @pl.when(g == gi)
        def _(win=win):
            x_cur = cur_ref[...]
            cur = modulated(x_cur, rc_ref[...])
            prev = jnp.where(i == 0, 0.0, modulated(prev_ref[...], rp_ref[...]))
            nxt = jnp.where(i == last, 0.0, modulated(next_ref[...], rn_ref[...]))
            ext = jnp.concatenate([prev, cur, nxt], axis=0)
            half = win // 2
            s = ext
            step = 1
            while step < win:
                m = s.shape[0] - step
                s = s[:m] + s[step:step + m]
                step *= 2
            wsum = s[SUBLANES - half:SUBLANES - half + tm]
            cnt = jnp.minimum(t + half, n) - jnp.maximum(t - half, 0)
            dlt = (wsum / cnt.astype(F32) - cur).astype(BF16)
            y = jnp.dot(dlt, w_ref[0], preferred_element_type=F32) * ps_ref[...]
            o_ref[...] = x_cur + g1_ref[...] * y


def pool_mixer(x, rstd, gs, sh, w, pscale, g1):
    n, d = x.shape
    ng, dg, _ = w.shape
    tm = _tile(n, WIDE_TILE)
    assert max(POOL_WINDOWS) // 2 <= SUBLANES
    halo_blocks = n // SUBLANES
    kern = functools.partial(_pool_kernel, n=n, tm=tm)
    col = pl.BlockSpec((1, dg), lambda g, i: (0, g))

    def halo(width, sel):
        per_tile = tm // SUBLANES
        before = pl.BlockSpec((SUBLANES, width), lambda g, i: (jnp.maximum(i * per_tile - 1, 0), sel(g)))
        after = pl.BlockSpec((SUBLANES, width),
                             lambda g, i: (jnp.minimum((i + 1) * per_tile, halo_blocks - 1), sel(g)))
        return pl.BlockSpec((tm, width), lambda g, i: (i, sel(g))), before, after

    return pl.pallas_call(
        kern,
        grid=(ng, n // tm),
        in_specs=[*halo(dg, lambda g: g), *halo(LANES, lambda g: 0), col, col,
                  pl.BlockSpec((1, dg, dg), lambda g, i: (g, 0, 0)), col, col],
        out_specs=pl.BlockSpec((tm, dg), lambda g, i: (i, g)),
        out_shape=jax.ShapeDtypeStruct((n, d), F32),
        compiler_params=_params("parallel", "parallel"),
        name="pool_mixer",
    )(x, x, x, rstd, rstd, rstd, gs, sh, w, pscale, g1)


def _mm_call(kern, a, w, extra, extra_specs, out_shape, out_specs, tm, tn, name, scratch=(), m=None):
    k = a.shape[1]
    m = a.shape[0] if m is None else m
    n = w.shape[1]
    return pl.pallas_call(
        kern,
        grid=(pl.cdiv(m, tm), n // tn),
        in_specs=[pl.BlockSpec((tm, k), lambda i, j: (i, 0)),
                  pl.BlockSpec((k, tn), lambda i, j: (0, j))] + list(extra_specs),
        out_specs=out_specs,
        out_shape=out_shape,
        scratch_shapes=list(scratch),
        compiler_params=_params("parallel", "arbitrary"),
        name=name,
    )(a, w, *extra)


def _mm_t_kernel(a_ref, wt_ref, o_ref):
    o_ref[...] = lax.dot_general(wt_ref[...], a_ref[...], (((1,), (1,)), ((), ())),
                                 preferred_element_type=F32).astype(o_ref.dtype)


def mm_transposed(a, wt, dtype=BF16):
    m, k = a.shape
    n = wt.shape[0]
    tm, tn = min(m, MM_TM), _tile(n, MM_TN)
    return pl.pallas_call(
        _mm_t_kernel,
        grid=(pl.cdiv(m, tm), n // tn),
        in_specs=[pl.BlockSpec((tm, k), lambda i, j: (i, 0)),
                  pl.BlockSpec((tn, k), lambda i, j: (j, 0))],
        out_specs=pl.BlockSpec((tn, tm), lambda i, j: (j, i)),
        out_shape=jax.ShapeDtypeStruct((n, m), dtype),
        compiler_params=_params("parallel", "arbitrary"),
        name="mm_transposed",
    )(a, wt)


def _mm_qk_kernel(a_ref, w_ref, g_ref, cos_ref, sin_ref, o_ref, *, scale):
    y = jnp.dot(a_ref[...], w_ref[...], preferred_element_type=F32)
    tn = y.shape[1]
    cos = cos_ref[...]
    sin = sin_ref[...]
    ga_cos, ga_sin = cos * g_ref[0:1], sin * g_ref[0:1]
    gb_cos, gb_sin = cos * g_ref[1:2], sin * g_ref[1:2]
    comp0 = lax.broadcasted_iota(I32, (1, HEAD_DIM), 1) < HEAD_DIM // 2
    for hd in range(tn // (2 * HEAD_DIM)):
        a = y[:, 2 * hd * HEAD_DIM:(2 * hd + 1) * HEAD_DIM]
        b = y[:, (2 * hd + 1) * HEAD_DIM:(2 * hd + 2) * HEAD_DIM]
        sq = a * a + b * b
        ssq0 = jnp.sum(jnp.where(comp0, sq, 0.0), axis=-1, keepdims=True)
        ssq1 = jnp.sum(jnp.where(comp0, 0.0, sq), axis=-1, keepdims=True)
        rstd = jnp.where(comp0, lax.rsqrt(ssq0 * (1.0 / HEAD_DIM) + NORM_EPS),
                         lax.rsqrt(ssq1 * (1.0 / HEAD_DIM) + NORM_EPS)) * scale
        o_ref[:, 2 * hd * HEAD_DIM:(2 * hd + 1) * HEAD_DIM] = ((a * ga_cos - b * gb_sin) * rstd).astype(o_ref.dtype)
        o_ref[:, (2 * hd + 1) * HEAD_DIM:(2 * hd + 2) * HEAD_DIM] = ((b * gb_cos + a * ga_sin) * rstd).astype(o_ref.dtype)


def mm_qk(a, w, gain, cos, sin, scale, m=None):
    m = a.shape[0] if m is None else m
    n = w.shape[1]
    tm, tn = min(m, MM_TM), _tile(n, MM_TN)
    kern = functools.partial(_mm_qk_kernel, scale=scale)
    tab = pl.BlockSpec((tm, HEAD_DIM), lambda i, j: (i, 0))
    return _mm_call(kern, a, w, (gain, cos, sin),
                    (pl.BlockSpec((2, HEAD_DIM), lambda i, j: (0, 0)), tab, tab),
                    jax.ShapeDtypeStruct((m, n), BF16),
                    pl.BlockSpec((tm, tn), lambda i, j: (i, j)), tm, tn, "mm_qk", m=m)


def _mm_gelu_kernel(a_ref, w_ref, o_ref, ssq_ref, *, nj_half):
    j = pl.program_id(1)
    y = jax.nn.gelu(jnp.dot(a_ref[...], w_ref[...], preferred_element_type=F32))
    o_ref[...] = y.astype(o_ref.dtype)

    @pl.when(j == nj_half)
    def _():
        ssq_ref[...] = jnp.zeros_like(ssq_ref)

    @pl.when(j >= nj_half)
    def _():
        ssq_ref[...] += jnp.sum(y * y, axis=-1, keepdims=True)


def mm_gelu(a, w):
    m, _ = a.shape
    n = w.shape[1]
    tm, tn = min(m, MM_TM), _tile(n // 2, MM_TN)
    kern = functools.partial(_mm_gelu_kernel, nj_half=(n // 2) // tn)
    return _mm_call(kern, a, w, (), (),
                    (jax.ShapeDtypeStruct((m, n), BF16), jax.ShapeDtypeStruct((m, LANES), F32)),
                    (pl.BlockSpec((tm, tn), lambda i, j: (i, j)),
                     pl.BlockSpec((tm, LANES), lambda i, j: (i, 0))), tm, tn, "mm_gelu")


def _mm_resid_kernel(a_ref, w_ref, x_ref, g_ref, o_ref):
    y = jnp.dot(a_ref[...], w_ref[...], preferred_element_type=F32)
    o_ref[...] = x_ref[...] + g_ref[...] * y


def mm_resid(a, w, x, g1):
    m, _ = a.shape
    n = w.shape[1]
    tm, tn = min(m, MM_TM), _tile(n, MM_TN)
    blk = pl.BlockSpec((tm, tn), lambda i, j: (i, j))
    return _mm_call(_mm_resid_kernel, a, w, (x, g1),
                    (blk, pl.BlockSpec((1, tn), lambda i, j: (0, j))),
                    jax.ShapeDtypeStruct((m, n), F32), blk, tm, tn, "mm_resid")


def _chunk_gate_kernel(u_ref, v_ref, ssq_ref, vg_ref, ws_ref, bs_ref, z_ref, *, width):
    tm = u_ref.shape[0]
    rstd = lax.rsqrt(ssq_ref[:, :1] * (1.0 / width) + NORM_EPS)

    def body(g, carry):
        col = pl.multiple_of(g * LANES, LANES)
        vg = vg_ref[:, pl.ds(col, LANES)]
        chunks = [slice(c * CHUNK, (c + 1) * CHUNK) for c in range(tm // CHUNK)]
        vn = jnp.concatenate([(v_ref[rows, pl.ds(col, LANES)].astype(F32) * rstd[rows] * vg).astype(BF16)
                              for rows in chunks], axis=1)
        sv = jnp.dot(ws_ref[g], vn, preferred_element_type=F32)
        bias = bs_ref[:, pl.ds(col, LANES)]
        for c, rows in enumerate(chunks):
            gate = sv[:, c * LANES:(c + 1) * LANES] + bias
            z_ref[rows, pl.ds(col, LANES)] = (u_ref[rows, pl.ds(col, LANES)].astype(F32) * gate).astype(BF16)
        return carry

    lax.fori_loop(0, width // LANES, body, 0, unroll=8)


def chunk_gate(uv, ssq, vg, ws, bs_full):
    n, w2 = uv.shape
    width = w2 // 2
    tm = _tile(n, WIDE_TILE)
    return pl.pallas_call(
        functools.partial(_chunk_gate_kernel, width=width),
        grid=(n // tm,),
        in_specs=[pl.BlockSpec((tm, width), lambda i: (i, 0)),
                  pl.BlockSpec((tm, width), lambda i: (i, 1)),
                  pl.BlockSpec((tm, LANES), lambda i: (i, 0)),
                  pl.BlockSpec((1, width), lambda i: (0, 0)),
                  pl.BlockSpec(ws.shape, lambda i: (0, 0, 0)),
                  pl.BlockSpec((CHUNK, width), lambda i: (0, 0))],
        out_specs=pl.BlockSpec((tm, width), lambda i: (i, 0)),
        out_shape=jax.ShapeDtypeStruct((n, width), BF16),
        compiler_params=_params("parallel"),
        name="chunk_gate",
    )(uv, uv, ssq, vg, ws, bs_full)


def _attn_kernel(lamv_ref, off_ref, q_ref, k_ref, vt_ref, sg_ref, o_ref, m_scr, l_scr, acc_scr, p_scr,
                 *, lam_init, tc, bounded):
    ki = pl.program_id(2)
    tk = k_ref.shape[0]
    nbuf = p_scr.shape[0]
    nchunk = q_ref.shape[0] // tc
    lane = lax.broadcasted_iota(I32, (1, 2 * HEAD_DIM), 1) % HEAD_DIM
    qcs = [q_ref[...] * jnp.where((lane < HEAD_DIM // 2) == (c == 0), 1.0, 0.0).astype(BF16) for c in range(2)]

    @pl.when(ki == 0)
    def _():
        l_scr[...] = jnp.zeros_like(l_scr)
        acc_scr[...] = jnp.zeros_like(acc_scr)
        if not bounded:
            m_scr[...] = jnp.full_like(m_scr, -jnp.inf)

    vt = vt_ref[...]
    kt = k_ref[...]
    scores = {}
    for c in range(2):
        for r in range(nchunk):
            scores[r, c] = lax.dot_general(kt, qcs[c][r * tc:(r + 1) * tc], (((1,), (1,)), ((), ())),
                                           preferred_element_type=F32)
    for r in range(nchunk):
        qs = slice(r * tc, (r + 1) * tc)
        for c in range(2):
            buf = (2 * r + c) % nbuf
            s = scores[r, c]
            if bounded:
                m_new = off_ref[:, :1]
            else:
                m_prev = m_scr[c, :, qs]
                m_new = jnp.maximum(m_prev, jnp.max(s, axis=0, keepdims=True))
            psum = jnp.zeros((BF16_ROWS, tc), F32)
            for g in range(tk // BF16_ROWS):
                keys = slice(BF16_ROWS * g, BF16_ROWS * (g + 1))
                pg = jnp.exp2(s[keys] - m_new)
                psum = psum + pg
                p_scr[buf, keys, :] = pg.astype(BF16)
            lsum = jnp.sum(psum, axis=0, keepdims=True)
            pv = jnp.dot(vt, p_scr[buf], preferred_element_type=F32)
            if bounded:
                l_scr[c, :, qs] += lsum
                acc_scr[c, :, qs] += pv
            else:
                alpha = jnp.exp2(m_prev - m_new)
                l_scr[c, :, qs] = alpha * l_scr[c, :, qs] + lsum
                acc_scr[c, :, qs] = alpha * acc_scr[c, :, qs] + pv
                m_scr[c, :, qs] = m_new

    @pl.when(ki == pl.num_programs(2) - 1)
    def _():
        lv = lamv_ref[...]
        lam = (jnp.exp(jnp.sum(lv[0:1] * lv[1:2], axis=-1, keepdims=True))
               - jnp.exp(jnp.sum(lv[2:3] * lv[3:4], axis=-1, keepdims=True)) + lam_init)
        ot = acc_scr[0] / l_scr[0] - lam * (acc_scr[1] / l_scr[1])
        ot = ot * lax.rsqrt(jnp.mean(ot * ot, axis=0, keepdims=True) + NORM_EPS)
        o_ref[...] = (ot.T * (sg_ref[...] * (1.0 - lam_init))).astype(o_ref.dtype)


def _key_tile(nk, cap):
    best = LANES
    for t in range(LANES, cap + 1, LANES):
        if nk % t == 0:
            best = t
    return best


def diff_attention(q, k, vt, lamv, sub_g, lam_init, score_bound):
    n, d = q.shape
    nk = k.shape[0]
    hw = 2 * HEAD_DIM
    heads = d // hw
    tq = _tile(n, ATTN_TQ)
    tk = _key_tile(nk, ATTN_TK_MAX)
    tc = min(tq, ATTN_TC)

    safe = score_bound <= MAX_FIXED_OFFSET
    off = jnp.full((1, LANES), score_bound, F32)

    def call(bounded):
        kern = functools.partial(_attn_kernel, lam_init=lam_init, tc=tc, bounded=bounded)
        return pl.pallas_call(
            kern,
            grid=(heads, n // tq, nk // tk),
            in_specs=[pl.BlockSpec((4, HEAD_DIM), lambda h, i, j: (0, 0)),
                      pl.BlockSpec((1, LANES), lambda h, i, j: (0, 0)),
                      pl.BlockSpec((tq, hw), lambda h, i, j: (i, h)),
                      pl.BlockSpec((tk, hw), lambda h, i, j: (j, h)),
                      pl.BlockSpec((hw, tk), lambda h, i, j: (h, j)),
                      pl.BlockSpec((1, hw), lambda h, i, j: (0, 0))],
            out_specs=pl.BlockSpec((tq, hw), lambda h, i, j: (i, h)),
            out_shape=jax.ShapeDtypeStruct((n, d), BF16),
            scratch_shapes=[pltpu.VMEM((2, 1, tq), F32), pltpu.VMEM((2, 1, tq), F32),
                            pltpu.VMEM((2, hw, tq), F32), pltpu.VMEM((2 * (tq // tc), tk, tc), BF16)],
            compiler_params=_params("parallel", "parallel", "arbitrary"),
            name="diff_attention_bounded" if bounded else "diff_attention_online",
        )(lamv, off, q, k, vt, sub_g)

    return lax.cond(safe, lambda: call(True), lambda: call(False))


def _router_kernel(x_ref, g_ref, sh_ref, sc_ref, wrt_ref, h_ref, aff_ref):
    h = _norm_mod(x_ref[...], g_ref[...], sh_ref[...], sc_ref[...])
    half = h.shape[1] // 2
    h_hi = h.astype(BF16)
    h_hi32 = h_hi.astype(F32)
    bits = pltpu.bitcast(h_hi32, U32)
    h_ref[...] = (bits[:, :half] >> 16) | (bits[:, half:] & jnp.uint32(0xFFFF0000))
    h_lo = (h - h_hi32).astype(BF16)
    w = wrt_ref[...]
    w_hi = w.astype(BF16)
    w_lo = (w - w_hi.astype(F32)).astype(BF16)
    dims = (((1,), (1,)), ((), ()))
    logits = (lax.dot_general(w_hi, h_hi, dims, preferred_element_type=F32)
              + lax.dot_general(w_lo, h_hi, dims, preferred_element_type=F32)
              + lax.dot_general(w_hi, h_lo, dims, preferred_element_type=F32))
    ex = jnp.exp(logits - jnp.max(logits, axis=0, keepdims=True))
    aff_ref[...] = ex / jnp.sum(ex, axis=0, keepdims=True)


def router(x, g, sh, sc, w_r_t):
    n, d = x.shape
    e = w_r_t.shape[0]
    tm = _tile(n, ROW_TILE)
    vec = pl.BlockSpec((1, d), lambda i: (0, 0))
    return pl.pallas_call(
        _router_kernel,
        grid=(n // tm,),
        in_specs=[pl.BlockSpec((tm, d), lambda i: (i, 0)), vec, vec, vec,
                  pl.BlockSpec((e, d), lambda i: (0, 0))],
        out_specs=(pl.BlockSpec((tm, d // 2), lambda i: (i, 0)), pl.BlockSpec((e, tm), lambda i: (0, i))),
        out_shape=(jax.ShapeDtypeStruct((n, d // 2), U32), jax.ShapeDtypeStruct((e, n), F32)),
        compiler_params=_params("parallel"),
        name="router",
    )(x, g, sh, sc, w_r_t)


def _split3(a):
    hi = a.astype(BF16)
    r1 = a - hi.astype(F32)
    mid = r1.astype(BF16)
    lo = (r1 - mid.astype(F32)).astype(BF16)
    return hi, mid, lo


def _topk_kernel(abt_ref, atb_ref, idx_ref, gate_ref, thr_scr, cut_scr, *, cap, cap_pad):
    ne, nb, _ = abt_ref.shape
    ntok = nb * LANES
    bits = pltpu.bitcast(abt_ref[...], I32)
    tok_bt = (lax.broadcasted_iota(I32, (1, nb, LANES), 1) * LANES
              + lax.broadcasted_iota(I32, (1, nb, LANES), 2))

    def count(ones):
        c = jnp.sum(ones, axis=2, keepdims=True)
        return jnp.sum(c, axis=1, keepdims=True)

    def thr_body(_, carry):
        lo, hi = carry
        mid = lo + ((hi - lo + 1) >> 1)
        ok = count(jnp.where(bits >= mid, 1, 0)) >= cap
        return jnp.where(ok, mid, lo), jnp.where(ok, hi, mid - 1)

    lo0 = jnp.zeros((ne, 1, 1), I32)
    hi0 = jnp.full((ne, 1, 1), 0x7F800000, I32)
    thr, _ = lax.fori_loop(0, 31, thr_body, (lo0, hi0))

    tok_eq = jnp.where(bits == thr, tok_bt, ntok)
    need = cap - count(jnp.where(bits > thr, 1, 0))

    def cut_body(_, carry):
        lo, hi = carry
        mid = (lo + hi) >> 1
        ok = count(jnp.where(tok_eq < mid, 1, 0)) >= need
        return jnp.where(ok, lo, mid), jnp.where(ok, mid, hi)

    _, cut = lax.fori_loop(0, int(math.log2(ntok)) + 1, cut_body,
                           (jnp.zeros((ne, 1, 1), I32), jnp.full((ne, 1, 1), ntok, I32)))
    thr_scr[...] = jnp.broadcast_to(thr, thr_scr.shape)
    cut_scr[...] = jnp.broadcast_to(cut, cut_scr.shape)

    r_i = lax.broadcasted_iota(I32, (LANES, LANES), 0)
    c_i = lax.broadcasted_iota(I32, (LANES, LANES), 1)
    tri_t = jnp.where(c_i <= r_i, 1.0, 0.0).astype(BF16)
    rb_i = lax.broadcasted_iota(I32, (nb, nb), 0)
    cb_i = lax.broadcasted_iota(I32, (nb, nb), 1)
    tri_b = jnp.where(cb_i <= rb_i, 1.0, 0.0).astype(BF16)
    tok_b = lax.broadcasted_iota(I32, (nb, LANES), 0) * LANES + lax.broadcasted_iota(I32, (nb, LANES), 1)
    tok_t = lax.broadcasted_iota(I32, (LANES, nb), 1) * LANES + lax.broadcasted_iota(I32, (LANES, nb), 0)
    slot = lax.broadcasted_iota(I32, (1, cap_pad), 1).astype(F32)
    blk_iota = lax.broadcasted_iota(I32, (nb, 1), 0).astype(F32)
    lane_iota = lax.broadcasted_iota(I32, (LANES, 1), 0).astype(F32)

    def select(b, tok, th, ct):
        return jnp.where(b > th, 1.0, jnp.where(b == th, jnp.where(tok < ct, 1.0, 0.0), 0.0))

    def per_expert(e, carry):
        th = thr_scr[e][:1, :1]
        ct = cut_scr[e][:1, :1]
        a_tb = atb_ref[e]
        sel_bt = select(pltpu.bitcast(abt_ref[e], I32), tok_b, th, ct)
        sel_tb = select(pltpu.bitcast(a_tb, I32), tok_t, th, ct)
        cw_t = jnp.dot(tri_t, sel_tb.astype(BF16), preferred_element_type=F32)
        tot = jnp.sum(sel_bt, axis=1, keepdims=True)
        incl = jnp.dot(tri_b, jnp.broadcast_to(tot, (nb, LANES)).astype(BF16),
                       preferred_element_type=F32)[:, :1]
        excl = incl - tot
        blk = jnp.sum(jnp.where(incl <= slot, 1.0, 0.0), axis=0, keepdims=True)
        onehot = jnp.where(blk_iota == blk, 1.0, 0.0)
        oh16 = onehot.astype(BF16)
        g_t = jnp.dot(cw_t.astype(BF16), oh16, preferred_element_type=F32)
        base = jnp.sum(onehot * excl, axis=0, keepdims=True)
        tl = jnp.sum(jnp.where(g_t + base <= slot, 1.0, 0.0), axis=0, keepdims=True)
        idx_ref[pl.ds(e, 1), :] = (blk * LANES + tl).astype(I32)
        hi, mid, lo = _split3(a_tb)
        rows = (jnp.dot(hi, oh16, preferred_element_type=F32)
                + jnp.dot(mid, oh16, preferred_element_type=F32)
                + jnp.dot(lo, oh16, preferred_element_type=F32))
        gate_ref[pl.ds(e, 1), :] = jnp.sum(jnp.where(lane_iota == tl, rows, 0.0), axis=0, keepdims=True)
        return carry

    lax.fori_loop(0, ne, per_expert, 0)


def topk_select(aff_t, cap):
    ne, n = aff_t.shape
    nb = max(LANES, -(-n // LANES))
    nb = -(-nb // LANES) * LANES
    cap_pad = max(LANES, cap)
    padded = jnp.pad(aff_t, ((0, 0), (0, nb * LANES - n)), constant_values=-1.0)
    a_bt = padded.reshape(ne, nb, LANES)
    a_tb = jnp.swapaxes(a_bt, 1, 2)
    kern = functools.partial(_topk_kernel, cap=cap, cap_pad=cap_pad)
    idx, gate = pl.pallas_call(
        kern,
        out_shape=(jax.ShapeDtypeStruct((ne, cap_pad), I32), jax.ShapeDtypeStruct((ne, cap_pad), F32)),
        scratch_shapes=[pltpu.VMEM((ne, 8, LANES), I32), pltpu.VMEM((ne, 8, LANES), I32)],
        compiler_params=pltpu.CompilerParams(vmem_limit_bytes=VMEM_LIMIT),
        name="topk_select",
    )(a_bt, a_tb)
    return idx[:, :cap], gate[:, :cap]


def _moe_kernel(idx_ref, gate_ref, g2_ref, h_hbm, wg_hbm, wu_hbm, wd_hbm, xin_hbm, out_hbm,
                xs, ab, y_scr, wg_f32, wu_f32, wd_f32, wgu, wd, sem_x, sem_a, sem_o, sem_w, *, tc, layer):
    del xin_hbm
    nct = pl.num_programs(1)
    ct = pl.program_id(1)
    s = pl.program_id(0) * nct + ct
    last = pl.num_programs(0) * nct - 1
    slot = s % 2
    base = s * tc
    nxt = jnp.minimum(s + 1, last) * tc

    def x_copy(b, r, sl):
        return pltpu.make_async_copy(h_hbm.at[pl.ds(idx_ref[b + r], 1)], xs.at[sl, pl.ds(r, 1)], sem_x.at[sl])

    def a_copy(r):
        return pltpu.make_async_copy(out_hbm.at[pl.ds(idx_ref[base + r], 1)], ab.at[slot, pl.ds(r, 1)],
                                     sem_a.at[slot])

    def o_copy(b, r, sl):
        return pltpu.make_async_copy(ab.at[sl, pl.ds(r, 1)], out_hbm.at[pl.ds(idx_ref[b + r], 1)], sem_o.at[sl])

    def rows(fn):
        def body(r, c):
            fn(r)
            return c
        lax.fori_loop(0, tc, body, 0, unroll=8)

    ne = pl.num_programs(0)
    expert = pl.program_id(0)

    def w_copies(e):
        return (pltpu.make_async_copy(wg_hbm.at[layer, e], wg_f32, sem_w.at[0]),
                pltpu.make_async_copy(wu_hbm.at[layer, e], wu_f32, sem_w.at[1]),
                pltpu.make_async_copy(wd_hbm.at[layer, e], wd_f32, sem_w.at[2]))

    @pl.when(s == 0)
    def _():
        for cp in w_copies(0):
            cp.start()
        rows(lambda r: x_copy(0, r, 0).start())

    @pl.when(ct == 0)
    def _():
        for cp in w_copies(expert):
            cp.wait()
        f = wd.shape[0]
        wgu[:, :f] = wg_f32[...].astype(BF16)
        wgu[:, f:] = wu_f32[...].astype(BF16)
        wd[...] = wd_f32[...].astype(BF16)

    @pl.when(jnp.logical_and(ct == 0, expert + 1 < ne))
    def _():
        for cp in w_copies(expert + 1):
            cp.start()

    def x_wait(sl):
        pltpu.make_async_copy(h_hbm.at[pl.ds(0, tc)], xs.at[sl], sem_x.at[sl]).wait()

    def o_wait(sl):
        pltpu.make_async_copy(ab.at[sl], out_hbm.at[pl.ds(0, tc)], sem_o.at[sl]).wait()

    x_wait(slot)

    @pl.when(jnp.logical_and(ct == 0, s > 0))
    def _():
        o_wait(1 - slot)

    for r in range(tc):
        x_copy(nxt, r, 1 - slot).start()
        a_copy(r).start()
    xu = xs[slot]
    x_lo = pltpu.bitcast(xu << 16, F32).astype(BF16)
    x_hi = pltpu.bitcast(xu & jnp.uint32(0xFFFF0000), F32).astype(BF16)
    half = xu.shape[1]
    ab2 = (jnp.dot(x_lo, wgu[:half], preferred_element_type=F32)
           + jnp.dot(x_hi, wgu[half:], preferred_element_type=F32))
    f = ab2.shape[1] // 2
    a, b = ab2[:, :f], ab2[:, f:]
    gate = jnp.concatenate([gate_ref[0]] * (f // LANES), axis=1)
    hm = (a * jax.nn.sigmoid(a) * b * gate).astype(BF16)
    y_scr[...] = jnp.dot(hm, wd[...], preferred_element_type=F32) * g2_ref[...]

    pltpu.make_async_copy(out_hbm.at[pl.ds(0, tc)], ab.at[slot], sem_a.at[slot]).wait()

    @pl.when(ct > 0)
    def _():
        o_wait(1 - slot)

    ab[slot] = ab[slot] + y_scr[...]
    for r in range(tc):
        o_copy(base, r, slot).start()

    @pl.when(s == last)
    def _():
        o_wait(slot)
        x_wait(1 - slot)


def moe_apply(h, x, idx, gate, w_gate, w_up, w_down, layer, g2):
    n, d = x.shape
    ne, cap = idx.shape
    f = w_down.shape[2]
    assert f % LANES == 0
    tc = _tile(cap, MOE_TC)
    gate_b = jnp.broadcast_to(gate[:, :, None], (ne, cap, LANES))
    kern = functools.partial(_moe_kernel, tc=tc, layer=layer)
    hbm = pl.BlockSpec(memory_space=pl.ANY)
    grid_spec = pltpu.PrefetchScalarGridSpec(
        num_scalar_prefetch=1,
        grid=(ne, cap // tc),
        in_specs=[pl.BlockSpec((1, tc, LANES), lambda e, c, idx: (e, c, 0)),
                  pl.BlockSpec((1, d), lambda e, c, idx: (0, 0)),
                  hbm, hbm, hbm, hbm, hbm],
        out_specs=hbm,
        scratch_shapes=[pltpu.VMEM((2, tc, d // 2), U32), pltpu.VMEM((2, tc, d), F32), pltpu.VMEM((tc, d), F32),
                        pltpu.VMEM((d, f), F32), pltpu.VMEM((d, f), F32), pltpu.VMEM((f, d), F32),
                        pltpu.VMEM((d, 2 * f), BF16), pltpu.VMEM((f, d), BF16),
                        pltpu.SemaphoreType.DMA((2,)), pltpu.SemaphoreType.DMA((2,)),
                        pltpu.SemaphoreType.DMA((2,)), pltpu.SemaphoreType.DMA((3,))],
    )
    return pl.pallas_call(
        kern,
        grid_spec=grid_spec,
        out_shape=jax.ShapeDtypeStruct((n, d), F32),
        input_output_aliases={7: 0},
        compiler_params=_params("arbitrary", "arbitrary"),
        name="moe_apply",
    )(idx.reshape(-1), gate_b, g2, h, w_gate, w_up, w_down, x)


def moe_layer(x, g, sh, sc, g2, w_r, w_gate, w_up, w_down, layer):
    n = x.shape[0]
    ne = w_r.shape[1]
    h, aff_t = router(x, g, sh, sc, w_r.T)
    idx, gate = topk_select(aff_t, EC_CAPACITY_FACTOR * n // ne)
    return moe_apply(h, x, idx, gate, w_gate, w_up, w_down, layer, g2)


def _rope_perm(width):
    quarter = HEAD_DIM // 4
    starts = jnp.array([0, 2, 4, 6, 1, 3, 5, 7]) * quarter
    blk = (starts[:, None] + jnp.arange(quarter)[None, :]).reshape(-1)
    return (jnp.arange(0, width, 2 * HEAD_DIM)[:, None] + blk[None, :]).reshape(-1)


def _permute_heads(w):
    rows, width = w.shape
    quarter = HEAD_DIM // 4
    w6 = w.reshape(rows, width // (2 * HEAD_DIM), 2, 2, 2, quarter)
    return jnp.transpose(w6, (0, 1, 4, 2, 3, 5)).reshape(rows, width)


def _rope_tables(n, n_ctx):
    quarter = HEAD_DIM // 4
    t = jnp.arange(n, dtype=I32)
    freqs = ROPE_THETA ** (-jnp.arange(quarter, dtype=F32) / quarter)
    ang_r = (t // GRID_W).astype(F32)[:, None] * freqs
    ang_c = (t % GRID_W).astype(F32)[:, None] * freqs
    cos = jnp.concatenate([jnp.cos(ang_r), jnp.cos(ang_c)] * 2, axis=1)
    sin = jnp.concatenate([jnp.sin(ang_r), jnp.sin(ang_c)] * 2, axis=1)
    cos = jnp.concatenate([cos, jnp.ones((n_ctx, HEAD_DIM), F32)], axis=0)
    sin = jnp.concatenate([sin, jnp.zeros((n_ctx, HEAD_DIM), F32)], axis=0)
    return cos, sin


def kernel(x, c, ctx, c_ctx, ada_w, ada_b, norm1_g, norm2_g, router_w, exp_w_gate, exp_w_up, exp_w_down,
           pool_w, pool_scale, gm_w_in, gm_v_g, gm_w_s, gm_b_s, gm_w_out, da_w_q, da_w_k, da_w_v, da_w_o,
           da_q_g, da_k_g, da_lam_q1, da_lam_k1, da_lam_q2, da_lam_k2, da_sub_g):
    bsz, n, d = x.shape
    assert bsz == 1 and c.shape[0] == 1
    n_ctx = ctx.shape[1]
    depth = ada_w.shape[0]
    xs, cs = x[0], ctx[0]

    s8 = jnp.concatenate([c, c_ctx[None], jnp.zeros((6, d), F32)], axis=0)
    mod = adaln(s8, ada_w, ada_b)

    def mods(i, row):
        return [mod[i, row:row + 1, k * d:(k + 1) * d] for k in range(6)]

    def row(v):
        return v.reshape(1, -1)

    for i in range(depth):
        kind, slot = i % N_MIXERS, i // N_MIXERS
        keep_ctx = any(j % N_MIXERS == 2 for j in range(i + 1, depth))
        sh1, sc1, g1, sh2, sc2, g2 = mods(i, 0)
        csh1, csc1, cg1, csh2, csc2, cg2 = mods(i, 1)
        n1, n2 = row(norm1_g[i]), row(norm2_g[i])
        streams = [(xs, sh1, sc1, g1)] + ([(cs, csh1, csc1, cg1)] if keep_ctx else [])

        if kind == 0:
            w = pool_w[slot].astype(BF16)
            outs = [pool_mixer(s, row_rstd(s), n1 * (1.0 + sc), sh, w, row(pool_scale[slot]), g)
                    for s, sh, sc, g in streams]
        elif kind == 1:
            w_in = gm_w_in[slot].astype(BF16)
            w_out = gm_w_out[slot].astype(BF16)
            w_s = gm_w_s[slot].astype(BF16)
            width = w_out.shape[0]
            bs_full = jnp.repeat(gm_b_s[slot].T, width // gm_w_s.shape[1], axis=1)
            outs = []
            for s, sh, sc, g in streams:
                uv, ssq = mm_gelu(norm_mod(s, n1, sh, sc, BF16), w_in)
                z = chunk_gate(uv, ssq, row(gm_v_g[slot]), w_s, bs_full)
                outs.append(mm_resid(z, w_out, s, g))
        else:
            assert not keep_ctx
            lam_init = 0.8 - 0.6 * math.exp(-0.3 * i)
            h_all = norm_mod_pair(xs, cs, n1, jnp.concatenate([sh1, csh1]), jnp.concatenate([sc1, csc1]), BF16)
            cos, sin = _rope_tables(n, n_ctx)
            gperm = (_rope_perm(2 * HEAD_DIM) % HEAD_DIM).reshape(2, HEAD_DIM)
            q_scale = HEAD_DIM ** -0.5 * math.log2(math.e)
            q = mm_qk(h_all, _permute_heads(da_w_q[slot]).astype(BF16), da_q_g[slot][gperm], cos, sin, q_scale, m=n)
            k = mm_qk(h_all, _permute_heads(da_w_k[slot]).astype(BF16), da_k_g[slot][gperm], cos, sin, 1.0)
            vt = mm_transposed(h_all, da_w_v[slot].T.astype(BF16))
            lamv = jnp.stack([da_lam_q1[slot], da_lam_k1[slot], da_lam_q2[slot], da_lam_k2[slot]])
            score_bound = (HEAD_DIM * q_scale * BOUND_SLACK) * jnp.max(jnp.abs(da_q_g[slot])) * jnp.max(jnp.abs(da_k_g[slot]))
            o = diff_attention(q, k, vt, lamv, row(da_sub_g[slot]), lam_init, score_bound)
            outs = [mm_resid(o, da_w_o[slot].astype(BF16), xs, g1)]

        xs = moe_layer(outs[0], n2, sh2, sc2, g2, router_w[i], exp_w_gate, exp_w_up, exp_w_down, i)
        if keep_ctx:
            cs = moe_layer(outs[1], n2, csh2, csc2, cg2, router_w[i], exp_w_gate, exp_w_up, exp_w_down, i)
    return xs[None]
```

```python
import functools
import math

import jax
import jax.numpy as jnp
from jax import lax
from jax.experimental import pallas as pl
from jax.experimental.pallas import tpu as pltpu

F32 = jnp.float32
BF16 = jnp.bfloat16
I32 = jnp.int32
U32 = jnp.uint32

NORM_EPS = 1e-6
LANES = 128
SUBLANES = 8
BF16_ROWS = 16
GRID_W = 64
CHUNK = 128
POOL_WINDOWS = (2, 4, 8, 16)
HEAD_DIM = 128
ROPE_THETA = 10000.0
EC_CAPACITY_FACTOR = 2
N_MIXERS = 3
VMEM_LIMIT = 56 * 1024 * 1024
ROW_TILE = 256
WIDE_TILE = 512
MM_TM, MM_TN = 1024, 1024
ADALN_TN = 1024
ATTN_TQ, ATTN_TK_MAX, ATTN_TC = 2048, 1536, 256
MOE_TC = 256
BOUND_SLACK = 1.01
MAX_FIXED_OFFSET = 60.0


def _params(*sem):
    return pltpu.CompilerParams(dimension_semantics=sem, vmem_limit_bytes=VMEM_LIMIT)


def _tile(n, t):
    t = min(n, t)
    assert n % t == 0, (n, t)
    return t


def _norm_mod(x, g, sh, sc):
    ms = jnp.mean(x * x, axis=-1, keepdims=True)
    return (x * lax.rsqrt(ms + NORM_EPS) * g) * (1.0 + sc) + sh


def _adaln_kernel(s_ref, w_ref, b_ref, o_ref):
    s = s_ref[...]
    s = s * jax.nn.sigmoid(s)
    o_ref[0] = jnp.dot(s, w_ref[0], preferred_element_type=F32) + b_ref[0]


def adaln(s8, ada_w, ada_b):
    depth, d, n6 = ada_w.shape
    tn = _tile(n6, ADALN_TN)
    return pl.pallas_call(
        _adaln_kernel,
        grid=(depth, n6 // tn),
        in_specs=[pl.BlockSpec((8, d), lambda l, j: (0, 0)),
                  pl.BlockSpec((1, d, tn), lambda l, j: (l, 0, j)),
                  pl.BlockSpec((1, 1, tn), lambda l, j: (l, 0, j))],
        out_specs=pl.BlockSpec((1, 8, tn), lambda l, j: (l, 0, j)),
        out_shape=jax.ShapeDtypeStruct((depth, 8, n6), F32),
        compiler_params=_params("parallel", "parallel"),
        name="adaln",
    )(s8, ada_w, ada_b.reshape(depth, 1, n6))


def _norm_mod_kernel(x_ref, g_ref, sh_ref, sc_ref, o_ref):
    o_ref[...] = _norm_mod(x_ref[...], g_ref[...], sh_ref[...], sc_ref[...]).astype(o_ref.dtype)


def norm_mod(x, g, sh, sc, dtype):
    n, d = x.shape
    tm = _tile(n, ROW_TILE)
    vec = pl.BlockSpec((1, d), lambda i: (0, 0))
    return pl.pallas_call(
        _norm_mod_kernel,
        grid=(n // tm,),
        in_specs=[pl.BlockSpec((tm, d), lambda i: (i, 0)), vec, vec, vec],
        out_specs=pl.BlockSpec((tm, d), lambda i: (i, 0)),
        out_shape=jax.ShapeDtypeStruct((n, d), dtype),
        compiler_params=_params("parallel"),
        name="norm_mod",
    )(x, g, sh, sc)


def _norm_mod_pair_kernel(x_ref, c_ref, g_ref, sh_ref, sc_ref, o_ref, *, nx):
    is_ctx = pl.program_id(0) >= nx
    src = jnp.where(is_ctx, c_ref[...], x_ref[...])
    sh = jnp.where(is_ctx, sh_ref[1:2], sh_ref[0:1])
    sc = jnp.where(is_ctx, sc_ref[1:2], sc_ref[0:1])
    o_ref[...] = _norm_mod(src, g_ref[...], sh, sc).astype(o_ref.dtype)


def norm_mod_pair(x, ctx, g, sh2, sc2, dtype):
    n, d = x.shape
    nc = ctx.shape[0]
    tm = _tile(nc, ROW_TILE)
    assert n % tm == 0
    nx = n // tm
    vec = pl.BlockSpec((1, d), lambda i: (0, 0))
    vec2 = pl.BlockSpec((2, d), lambda i: (0, 0))
    return pl.pallas_call(
        functools.partial(_norm_mod_pair_kernel, nx=nx),
        grid=(nx + nc // tm,),
        in_specs=[pl.BlockSpec((tm, d), lambda i: (jnp.minimum(i, nx - 1), 0)),
                  pl.BlockSpec((tm, d), lambda i: (jnp.maximum(i - nx, 0), 0)), vec, vec2, vec2],
        out_specs=pl.BlockSpec((tm, d), lambda i: (i, 0)),
        out_shape=jax.ShapeDtypeStruct((n + nc, d), dtype),
        compiler_params=_params("parallel"),
        name="norm_mod_pair",
    )(x, ctx, g, sh2, sc2)


def _rstd_kernel(x_ref, o_ref):
    x = x_ref[...]
    o_ref[...] = jnp.broadcast_to(lax.rsqrt(jnp.mean(x * x, axis=-1, keepdims=True) + NORM_EPS), o_ref.shape)


def row_rstd(x):
    n, d = x.shape
    tm = _tile(n, ROW_TILE)
    return pl.pallas_call(
        _rstd_kernel,
        grid=(n // tm,),
        in_specs=[pl.BlockSpec((tm, d), lambda i: (i, 0))],
        out_specs=pl.BlockSpec((tm, LANES), lambda i: (i, 0)),
        out_shape=jax.ShapeDtypeStruct((n, LANES), F32),
        compiler_params=_params("parallel"),
        name="row_rstd",
    )(x)


def _pool_kernel(cur_ref, prev_ref, next_ref, rc_ref, rp_ref, rn_ref, gs_ref, sh_ref, w_ref, ps_ref, g1_ref, o_ref,
                 *, n, tm):
    g = pl.program_id(0)
    i = pl.program_id(1)
    last = pl.num_programs(1) - 1
    reps = cur_ref.shape[1] // LANES

    def modulated(x, r):
        return x * jnp.concatenate([r] * reps, axis=1) * gs_ref[...] + sh_ref[...]

    t = i * tm + lax.broadcasted_iota(I32, (tm, 1), 0)

    for gi, win in enumerate(POOL_WINDOWS):
        @pl.when(g == gi)
        def _(win=win):
            x_cur = cur_ref[...]
            cur = modulated(x_cur, rc_ref[...])
            prev = jnp.where(i == 0, 0.0, modulated(prev_ref[...], rp_ref[...]))
            nxt = jnp.where(i == last, 0.0, modulated(next_ref[...], rn_ref[...]))
            ext = jnp.concatenate([prev, cur, nxt], axis=0)
            half = win // 2
            s = ext
            step = 1
            while step < win:
                m = s.shape[0] - step
                s = s[:m] + s[step:step + m]
                step *= 2
            wsum = s[SUBLANES - half:SUBLANES - half + tm]
            cnt = jnp.minimum(t + half, n) - jnp.maximum(t - half, 0)
            dlt = (wsum / cnt.astype(F32) - cur).astype(BF16)
            y = jnp.dot(dlt, w_ref[0], preferred_element_type=F32) * ps_ref[...]
            o_ref[...] = x_cur + g1_ref[...] * y


def pool_mixer(x, rstd, gs, sh, w, pscale, g1):
    n, d = x.shape
    ng, dg, _ = w.shape
    tm = _tile(n, WIDE_TILE)
    assert max(POOL_WINDOWS) // 2 <= SUBLANES
    halo_blocks = n // SUBLANES
    kern = functools.partial(_pool_kernel, n=n, tm=tm)
    col = pl.BlockSpec((1, dg), lambda g, i: (0, g))

    def halo(width, sel):
        per_tile = tm // SUBLANES
        before = pl.BlockSpec((SUBLANES, width), lambda g, i: (jnp.maximum(i * per_tile - 1, 0), sel(g)))
        after = pl.BlockSpec((SUBLANES, width),
                             lambda g, i: (jnp.minimum((i + 1) * per_tile, halo_blocks - 1), sel(g)))
        return pl.BlockSpec((tm, width), lambda g, i: (i, sel(g))), before, after

    return pl.pallas_call(
        kern,
        grid=(ng, n // tm),
        in_specs=[*halo(dg, lambda g: g), *halo(LANES, lambda g: 0), col, col,
                  pl.BlockSpec((1, dg, dg), lambda g, i: (g, 0, 0)), col, col],
        out_specs=pl.BlockSpec((tm, dg), lambda g, i: (i, g)),
        out_shape=jax.ShapeDtypeStruct((n, d), F32),
        compiler_params=_params("parallel", "parallel"),
        name="pool_mixer",
    )(x, x, x, rstd, rstd, rstd, gs, sh, w, pscale, g1)


def _mm_call(kern, a, w, extra, extra_specs, out_shape, out_specs, tm, tn, name, scratch=(), m=None):
    k = a.shape[1]
    m = a.shape[0] if m is None else m
    n = w.shape[1]
    return pl.pallas_call(
        kern,
        grid=(pl.cdiv(m, tm), n // tn),
        in_specs=[pl.BlockSpec((tm, k), lambda i, j: (i, 0)),
                  pl.BlockSpec((k, tn), lambda i, j: (0, j))] + list(extra_specs),
        out_specs=out_specs,
        out_shape=out_shape,
        scratch_shapes=list(scratch),
        compiler_params=_params("parallel", "arbitrary"),
        name=name,
    )(a, w, *extra)


def _mm_t_kernel(a_ref, wt_ref, o_ref):
    o_ref[...] = lax.dot_general(wt_ref[...], a_ref[...], (((1,), (1,)), ((), ())),
                                 preferred_element_type=F32).astype(o_ref.dtype)


def mm_transposed(a, wt, dtype=BF16):
    m, k = a.shape
    n = wt.shape[0]
    tm, tn = min(m, MM_TM), _tile(n, MM_TN)
    return pl.pallas_call(
        _mm_t_kernel,
        grid=(pl.cdiv(m, tm), n // tn),
        in_specs=[pl.BlockSpec((tm, k), lambda i, j: (i, 0)),
                  pl.BlockSpec((tn, k), lambda i, j: (j, 0))],
        out_specs=pl.BlockSpec((tn, tm), lambda i, j: (j, i)),
        out_shape=jax.ShapeDtypeStruct((n, m), dtype),
        compiler_params=_params("parallel", "arbitrary"),
        name="mm_transposed",
    )(a, wt)


def _mm_qk_kernel(a_ref, w_ref, g_ref, cos_ref, sin_ref, o_ref, *, scale):
    y = jnp.dot(a_ref[...], w_ref[...], preferred_element_type=F32)
    tn = y.shape[1]
    cos = cos_ref[...]
    sin = sin_ref[...]
    ga_cos, ga_sin = cos * g_ref[0:1], sin * g_ref[0:1]
    gb_cos, gb_sin = cos * g_ref[1:2], sin * g_ref[1:2]
    comp0 = lax.broadcasted_iota(I32, (1, HEAD_DIM), 1) < HEAD_DIM // 2
    for hd in range(tn // (2 * HEAD_DIM)):
        a = y[:, 2 * hd * HEAD_DIM:(2 * hd + 1) * HEAD_DIM]
        b = y[:, (2 * hd + 1) * HEAD_DIM:(2 * hd + 2) * HEAD_DIM]
        sq = a * a + b * b
        ssq0 = jnp.sum(jnp.where(comp0, sq, 0.0), axis=-1, keepdims=True)
        ssq1 = jnp.sum(jnp.where(comp0, 0.0, sq), axis=-1, keepdims=True)
        rstd = jnp.where(comp0, lax.rsqrt(ssq0 * (1.0 / HEAD_DIM) + NORM_EPS),
                         lax.rsqrt(ssq1 * (1.0 / HEAD_DIM) + NORM_EPS)) * scale
        o_ref[:, 2 * hd * HEAD_DIM:(2 * hd + 1) * HEAD_DIM] = ((a * ga_cos - b * gb_sin) * rstd).astype(o_ref.dtype)
        o_ref[:, (2 * hd + 1) * HEAD_DIM:(2 * hd + 2) * HEAD_DIM] = ((b * gb_cos + a * ga_sin) * rstd).astype(o_ref.dtype)


def mm_qk(a, w, gain, cos, sin, scale, m=None):
    m = a.shape[0] if m is None else m
    n = w.shape[1]
    tm, tn = min(m, MM_TM), _tile(n, MM_TN)
    kern = functools.partial(_mm_qk_kernel, scale=scale)
    tab = pl.BlockSpec((tm, HEAD_DIM), lambda i, j: (i, 0))
    return _mm_call(kern, a, w, (gain, cos, sin),
                    (pl.BlockSpec((2, HEAD_DIM), lambda i, j: (0, 0)), tab, tab),
                    jax.ShapeDtypeStruct((m, n), BF16),
                    pl.BlockSpec((tm, tn), lambda i, j: (i, j)), tm, tn, "mm_qk", m=m)


def _mm_gelu_kernel(a_ref, w_ref, o_ref, ssq_ref, *, nj_half):
    j = pl.program_id(1)
    y = jax.nn.gelu(jnp.dot(a_ref[...], w_ref[...], preferred_element_type=F32))
    o_ref[...] = y.astype(o_ref.dtype)

    @pl.when(j == nj_half)
    def _():
        ssq_ref[...] = jnp.zeros_like(ssq_ref)

    @pl.when(j >= nj_half)
    def _():
        ssq_ref[...] += jnp.sum(y * y, axis=-1, keepdims=True)


def mm_gelu(a, w):
    m, _ = a.shape
    n = w.shape[1]
    tm, tn = min(m, MM_TM), _tile(n // 2, MM_TN)
    kern = functools.partial(_mm_gelu_kernel, nj_half=(n // 2) // tn)
    return _mm_call(kern, a, w, (), (),
                    (jax.ShapeDtypeStruct((m, n), BF16), jax.ShapeDtypeStruct((m, LANES), F32)),
                    (pl.BlockSpec((tm, tn), lambda i, j: (i, j)),
                     pl.BlockSpec((tm, LANES), lambda i, j: (i, 0))), tm, tn, "mm_gelu")


def _mm_resid_kernel(a_ref, w_ref, x_ref, g_ref, o_ref):
    y = jnp.dot(a_ref[...], w_ref[...], preferred_element_type=F32)
    o_ref[...] = x_ref[...] + g_ref[...] * y


def mm_resid(a, w, x, g1):
    m, _ = a.shape
    n = w.shape[1]
    tm, tn = min(m, MM_TM), _tile(n, MM_TN)
    blk = pl.BlockSpec((tm, tn), lambda i, j: (i, j))
    return _mm_call(_mm_resid_kernel, a, w, (x, g1),
                    (blk, pl.BlockSpec((1, tn), lambda i, j: (0, j))),
                    jax.ShapeDtypeStruct((m, n), F32), blk, tm, tn, "mm_resid")


def _chunk_gate_kernel(u_ref, v_ref, ssq_ref, vg_ref, ws_ref, bs_ref, z_ref, *, width):
    tm = u_ref.shape[0]
    rstd = lax.rsqrt(ssq_ref[:, :1] * (1.0 / width) + NORM_EPS)

    def body(g, carry):
        col = pl.multiple_of(g * LANES, LANES)
        vg = vg_ref[:, pl.ds(col, LANES)]
        chunks = [slice(c * CHUNK, (c + 1) * CHUNK) for c in range(tm // CHUNK)]
        vn = jnp.concatenate([(v_ref[rows, pl.ds(col, LANES)].astype(F32) * rstd[rows] * vg).astype(BF16)
                              for rows in chunks], axis=1)
        sv = jnp.dot(ws_ref[g], vn, preferred_element_type=F32)
        bias = bs_ref[:, pl.ds(col, LANES)]
        for c, rows in enumerate(chunks):
            gate = sv[:, c * LANES:(c + 1) * LANES] + bias
            z_ref[rows, pl.ds(col, LANES)] = (u_ref[rows, pl.ds(col, LANES)].astype(F32) * gate).astype(BF16)
        return carry

    lax.fori_loop(0, width // LANES, body, 0, unroll=8)


def chunk_gate(uv, ssq, vg, ws, bs_full):
    n, w2 = uv.shape
    width = w2 // 2
    tm = _tile(n, WIDE_TILE)
    return pl.pallas_call(
        functools.partial(_chunk_gate_kernel, width=width),
        grid=(n // tm,),
        in_specs=[pl.BlockSpec((tm, width), lambda i: (i, 0)),
                  pl.BlockSpec((tm, width), lambda i: (i, 1)),
                  pl.BlockSpec((tm, LANES), lambda i: (i, 0)),
                  pl.BlockSpec((1, width), lambda i: (0, 0)),
                  pl.BlockSpec(ws.shape, lambda i: (0, 0, 0)),
                  pl.BlockSpec((CHUNK, width), lambda i: (0, 0))],
        out_specs=pl.BlockSpec((tm, width), lambda i: (i, 0)),
        out_shape=jax.ShapeDtypeStruct((n, width), BF16),
        compiler_params=_params("parallel"),
        name="chunk_gate",
    )(uv, uv, ssq, vg, ws, bs_full)


def _attn_kernel(lamv_ref, off_ref, q_ref, k_ref, vt_ref, sg_ref, o_ref, m_scr, l_scr, acc_scr, p_scr,
                 *, lam_init, tc, bounded):
    ki = pl.program_id(2)
    tk = k_ref.shape[0]
    nbuf = p_scr.shape[0]
    nchunk = q_ref.shape[0] // tc
    lane = lax.broadcasted_iota(I32, (1, 2 * HEAD_DIM), 1) % HEAD_DIM
    qcs = [q_ref[...] * jnp.where((lane < HEAD_DIM // 2) == (c == 0), 1.0, 0.0).astype(BF16) for c in range(2)]

    @pl.when(ki == 0)
    def _():
        l_scr[...] = jnp.zeros_like(l_scr)
        acc_scr[...] = jnp.zeros_like(acc_scr)
        if not bounded:
            m_scr[...] = jnp.full_like(m_scr, -jnp.inf)

    vt = vt_ref[...]
    kt = k_ref[...]
    scores = {}
    for c in range(2):
        for r in range(nchunk):
            scores[r, c] = lax.dot_general(kt, qcs[c][r * tc:(r + 1) * tc], (((1,), (1,)), ((), ())),
                                           preferred_element_type=F32)
    for r in range(nchunk):
        qs = slice(r * tc, (r + 1) * tc)
        for c in range(2):
            buf = (2 * r + c) % nbuf
            s = scores[r, c]
            if bounded:
                m_new = off_ref[:, :1]
            else:
                m_prev = m_scr[c, :, qs]
                m_new = jnp.maximum(m_prev, jnp.max(s, axis=0, keepdims=True))
            psum = jnp.zeros((BF16_ROWS, tc), F32)
            for g in range(tk // BF16_ROWS):
                keys = slice(BF16_ROWS * g, BF16_ROWS * (g + 1))
                pg = jnp.exp2(s[keys] - m_new)
                psum = psum + pg
                p_scr[buf, keys, :] = pg.astype(BF16)
            lsum = jnp.sum(psum, axis=0, keepdims=True)
            pv = jnp.dot(vt, p_scr[buf], preferred_element_type=F32)
            if bounded:
                l_scr[c, :, qs] += lsum
                acc_scr[c, :, qs] += pv
            else:
                alpha = jnp.exp2(m_prev - m_new)
                l_scr[c, :, qs] = alpha * l_scr[c, :, qs] + lsum
                acc_scr[c, :, qs] = alpha * acc_scr[c, :, qs] + pv
                m_scr[c, :, qs] = m_new

    @pl.when(ki == pl.num_programs(2) - 1)
    def _():
        lv = lamv_ref[...]
        lam = (jnp.exp(jnp.sum(lv[0:1] * lv[1:2], axis=-1, keepdims=True))
               - jnp.exp(jnp.sum(lv[2:3] * lv[3:4], axis=-1, keepdims=True)) + lam_init)
        ot = acc_scr[0] / l_scr[0] - lam * (acc_scr[1] / l_scr[1])
        ot = ot * lax.rsqrt(jnp.mean(ot * ot, axis=0, keepdims=True) + NORM_EPS)
        o_ref[...] = (ot.T * (sg_ref[...] * (1.0 - lam_init))).astype(o_ref.dtype)


def _key_tile(nk, cap):
    best = LANES
    for t in range(LANES, cap + 1, LANES):
        if nk % t == 0:
            best = t
    return best


def diff_attention(q, k, vt, lamv, sub_g, lam_init, score_bound):
    n, d = q.shape
    nk = k.shape[0]
    hw = 2 * HEAD_DIM
    heads = d // hw
    tq = _tile(n, ATTN_TQ)
    tk = _key_tile(nk, ATTN_TK_MAX)
    tc = min(tq, ATTN_TC)

    safe = score_bound <= MAX_FIXED_OFFSET
    off = jnp.full((1, LANES), score_bound, F32)

    def call(bounded):
        kern = functools.partial(_attn_kernel, lam_init=lam_init, tc=tc, bounded=bounded)
        return pl.pallas_call(
            kern,
            grid=(heads, n // tq, nk // tk),
            in_specs=[pl.BlockSpec((4, HEAD_DIM), lambda h, i, j: (0, 0)),
                      pl.BlockSpec((1, LANES), lambda h, i, j: (0, 0)),
                      pl.BlockSpec((tq, hw), lambda h, i, j: (i, h)),
                      pl.BlockSpec((tk, hw), lambda h, i, j: (j, h)),
                      pl.BlockSpec((hw, tk), lambda h, i, j: (h, j)),
                      pl.BlockSpec((1, hw), lambda h, i, j: (0, 0))],
            out_specs=pl.BlockSpec((tq, hw), lambda h, i, j: (i, h)),
            out_shape=jax.ShapeDtypeStruct((n, d), BF16),
            scratch_shapes=[pltpu.VMEM((2, 1, tq), F32), pltpu.VMEM((2, 1, tq), F32),
                            pltpu.VMEM((2, hw, tq), F32), pltpu.VMEM((2 * (tq // tc), tk, tc), BF16)],
            compiler_params=_params("parallel", "parallel", "arbitrary"),
            name="diff_attention_bounded" if bounded else "diff_attention_online",
        )(lamv, off, q, k, vt, sub_g)

    return lax.cond(safe, lambda: call(True), lambda: call(False))


def _router_kernel(x_ref, g_ref, sh_ref, sc_ref, wrt_ref, h_ref, aff_ref):
    h = _norm_mod(x_ref[...], g_ref[...], sh_ref[...], sc_ref[...])
    half = h.shape[1] // 2
    h_hi = h.astype(BF16)
    h_hi32 = h_hi.astype(F32)
    bits = pltpu.bitcast(h_hi32, U32)
    h_ref[...] = (bits[:, :half] >> 16) | (bits[:, half:] & jnp.uint32(0xFFFF0000))
    h_lo = (h - h_hi32).astype(BF16)
    w = wrt_ref[...]
    w_hi = w.astype(BF16)
    w_lo = (w - w_hi.astype(F32)).astype(BF16)
    dims = (((1,), (1,)), ((), ()))
    logits = (lax.dot_general(w_hi, h_hi, dims, preferred_element_type=F32)
              + lax.dot_general(w_lo, h_hi, dims, preferred_element_type=F32)
              + lax.dot_general(w_hi, h_lo, dims, preferred_element_type=F32))
    ex = jnp.exp(logits - jnp.max(logits, axis=0, keepdims=True))
    aff_ref[...] = ex / jnp.sum(ex, axis=0, keepdims=True)


def router(x, g, sh, sc, w_r_t):
    n, d = x.shape
    e = w_r_t.shape[0]
    tm = _tile(n, ROW_TILE)
    vec = pl.BlockSpec((1, d), lambda i: (0, 0))
    return pl.pallas_call(
        _router_kernel,
        grid=(n // tm,),
        in_specs=[pl.BlockSpec((tm, d), lambda i: (i, 0)), vec, vec, vec,
                  pl.BlockSpec((e, d), lambda i: (0, 0))],
        out_specs=(pl.BlockSpec((tm, d // 2), lambda i: (i, 0)), pl.BlockSpec((e, tm), lambda i: (0, i))),
        out_shape=(jax.ShapeDtypeStruct((n, d // 2), U32), jax.ShapeDtypeStruct((e, n), F32)),
        compiler_params=_params("parallel"),
        name="router",
    )(x, g, sh, sc, w_r_t)


def _split3(a):
    hi = a.astype(BF16)
    r1 = a - hi.astype(F32)
    mid = r1.astype(BF16)
    lo = (r1 - mid.astype(F32)).astype(BF16)
    return hi, mid, lo


def _topk_kernel(abt_ref, atb_ref, idx_ref, gate_ref, thr_scr, cut_scr, *, cap, cap_pad):
    ne, nb, _ = abt_ref.shape
    ntok = nb * LANES
    bits = pltpu.bitcast(abt_ref[...], I32)
    tok_bt = (lax.broadcasted_iota(I32, (1, nb, LANES), 1) * LANES
              + lax.broadcasted_iota(I32, (1, nb, LANES), 2))

    def count(ones):
        c = jnp.sum(ones, axis=1, keepdims=True)
        return jnp.sum(c, axis=2, keepdims=True)

    def thr_body(_, carry):
        lo, hi = carry
        mid = lo + ((hi - lo + 1) >> 1)
        ok = count(jnp.where(bits >= mid, 1, 0)) >= cap
        return jnp.where(ok, mid, lo), jnp.where(ok, hi, mid - 1)

    lo0 = jnp.zeros((ne, 1, 1), I32)
    hi0 = jnp.full((ne, 1, 1), 0x7F800000, I32)
    thr, _ = lax.fori_loop(0, 31, thr_body, (lo0, hi0))

    tok_eq = jnp.where(bits == thr, tok_bt, ntok)
    need = cap - count(jnp.where(bits > thr, 1, 0))

    def cut_body(_, carry):
        lo, hi = carry
        mid = (lo + hi) >> 1
        ok = count(jnp.where(tok_eq < mid, 1, 0)) >= need
        return jnp.where(ok, lo, mid), jnp.where(ok, mid, hi)

    _, cut = lax.fori_loop(0, int(math.log2(ntok)) + 1, cut_body,
                           (jnp.zeros((ne, 1, 1), I32), jnp.full((ne, 1, 1), ntok, I32)))
    thr_scr[...] = jnp.broadcast_to(thr, thr_scr.shape)
    cut_scr[...] = jnp.broadcast_to(cut, cut_scr.shape)

    r_i = lax.broadcasted_iota(I32, (LANES, LANES), 0)
    c_i = lax.broadcasted_iota(I32, (LANES, LANES), 1)
    tri_t = jnp.where(c_i <= r_i, 1.0, 0.0).astype(BF16)
    rb_i = lax.broadcasted_iota(I32, (nb, nb), 0)
    cb_i = lax.broadcasted_iota(I32, (nb, nb), 1)
    tri_b = jnp.where(cb_i <= rb_i, 1.0, 0.0).astype(BF16)
    tok_b = lax.broadcasted_iota(I32, (nb, LANES), 0) * LANES + lax.broadcasted_iota(I32, (nb, LANES), 1)
    tok_t = lax.broadcasted_iota(I32, (LANES, nb), 1) * LANES + lax.broadcasted_iota(I32, (LANES, nb), 0)
    slot = lax.broadcasted_iota(I32, (1, cap_pad), 1).astype(F32)
    blk_iota = lax.broadcasted_iota(I32, (nb, 1), 0).astype(F32)
    lane_iota = lax.broadcasted_iota(I32, (LANES, 1), 0).astype(F32)

    def select(b, tok, th, ct):
        return jnp.where(b > th, 1.0, jnp.where(b == th, jnp.where(tok < ct, 1.0, 0.0), 0.0))

    def per_expert(e, carry):
        th = thr_scr[e][:1, :1]
        ct = cut_scr[e][:1, :1]
        a_tb = atb_ref[e]
        sel_bt = select(pltpu.bitcast(abt_ref[e], I32), tok_b, th, ct)
        sel_tb = select(pltpu.bitcast(a_tb, I32), tok_t, th, ct)
        cw_t = jnp.dot(tri_t, sel_tb.astype(BF16), preferred_element_type=F32)
        tot = jnp.sum(sel_bt, axis=1, keepdims=True)
        incl = jnp.dot(tri_b, jnp.broadcast_to(tot, (nb, LANES)).astype(BF16),
                       preferred_element_type=F32)[:, :1]
        excl = incl - tot
        blk = jnp.sum(jnp.where(incl <= slot, 1.0, 0.0), axis=0, keepdims=True)
        onehot = jnp.where(blk_iota == blk, 1.0, 0.0)
        oh16 = onehot.astype(BF16)
        g_t = jnp.dot(cw_t.astype(BF16), oh16, preferred_element_type=F32)
        base = jnp.sum(onehot * excl, axis=0, keepdims=True)
        tl = jnp.sum(jnp.where(g_t + base <= slot, 1.0, 0.0), axis=0, keepdims=True)
        idx_ref[pl.ds(e, 1), :] = (blk * LANES + tl).astype(I32)
        hi, mid, lo = _split3(a_tb)
        rows = (jnp.dot(hi, oh16, preferred_element_type=F32)
                + jnp.dot(mid, oh16, preferred_element_type=F32)
                + jnp.dot(lo, oh16, preferred_element_type=F32))
        gate_ref[pl.ds(e, 1), :] = jnp.sum(jnp.where(lane_iota == tl, rows, 0.0), axis=0, keepdims=True)
        return carry

    lax.fori_loop(0, ne, per_expert, 0)


def topk_select(aff_t, cap):
    ne, n = aff_t.shape
    nb = max(LANES, -(-n // LANES))
    nb = -(-nb // LANES) * LANES
    cap_pad = max(LANES, cap)
    padded = jnp.pad(aff_t, ((0, 0), (0, nb * LANES - n)), constant_values=-1.0)
    a_bt = padded.reshape(ne, nb, LANES)
    a_tb = jnp.swapaxes(a_bt, 1, 2)
    kern = functools.partial(_topk_kernel, cap=cap, cap_pad=cap_pad)
    idx, gate = pl.pallas_call(
        kern,
        out_shape=(jax.ShapeDtypeStruct((ne, cap_pad), I32), jax.ShapeDtypeStruct((ne, cap_pad), F32)),
        scratch_shapes=[pltpu.VMEM((ne, 8, LANES), I32), pltpu.VMEM((ne, 8, LANES), I32)],
        compiler_params=pltpu.CompilerParams(vmem_limit_bytes=VMEM_LIMIT),
        name="topk_select",
    )(a_bt, a_tb)
    return idx[:, :cap], gate[:, :cap]


def _moe_kernel(idx_ref, gate_ref, g2_ref, h_hbm, wg_hbm, wu_hbm, wd_hbm, xin_hbm, out_hbm,
                xs, ab, y_scr, wg_f32, wu_f32, wd_f32, wgu, wd, sem_x, sem_a, sem_o, sem_w, *, tc, layer):
    del xin_hbm
    nct = pl.num_programs(1)
    ct = pl.program_id(1)
    s = pl.program_id(0) * nct + ct
    last = pl.num_programs(0) * nct - 1
    slot = s % 2
    base = s * tc
    nxt = jnp.minimum(s + 1, last) * tc

    def x_copy(b, r, sl):
        return pltpu.make_async_copy(h_hbm.at[pl.ds(idx_ref[b + r], 1)], xs.at[sl, pl.ds(r, 1)], sem_x.at[sl])

    def a_copy(r):
        return pltpu.make_async_copy(out_hbm.at[pl.ds(idx_ref[base + r], 1)], ab.at[slot, pl.ds(r, 1)],
                                     sem_a.at[slot])

    def o_copy(b, r, sl):
        return pltpu.make_async_copy(ab.at[sl, pl.ds(r, 1)], out_hbm.at[pl.ds(idx_ref[b + r], 1)], sem_o.at[sl])

    def rows(fn):
        def body(r, c):
            fn(r)
            return c
        lax.fori_loop(0, tc, body, 0, unroll=8)

    ne = pl.num_programs(0)
    expert = pl.program_id(0)

    def w_copies(e):
        return (pltpu.make_async_copy(wg_hbm.at[layer, e], wg_f32, sem_w.at[0]),
                pltpu.make_async_copy(wu_hbm.at[layer, e], wu_f32, sem_w.at[1]),
                pltpu.make_async_copy(wd_hbm.at[layer, e], wd_f32, sem_w.at[2]))

    @pl.when(s == 0)
    def _():
        for cp in w_copies(0):
            cp.start()
        rows(lambda r: x_copy(0, r, 0).start())

    @pl.when(ct == 0)
    def _():
        for cp in w_copies(expert):
            cp.wait()
        f = wd.shape[0]
        wgu[:, :f] = wg_f32[...].astype(BF16)
        wgu[:, f:] = wu_f32[...].astype(BF16)
        wd[...] = wd_f32[...].astype(BF16)

    @pl.when(jnp.logical_and(ct == 0, expert + 1 < ne))
    def _():
        for cp in w_copies(expert + 1):
            cp.start()

    def x_wait(sl):
        pltpu.make_async_copy(h_hbm.at[pl.ds(0, tc)], xs.at[sl], sem_x.at[sl]).wait()

    def o_wait(sl):
        pltpu.make_async_copy(ab.at[sl], out_hbm.at[pl.ds(0, tc)], sem_o.at[sl]).wait()

    x_wait(slot)

    @pl.when(jnp.logical_and(ct == 0, s > 0))
    def _():
        o_wait(1 - slot)

    for r in range(tc):
        x_copy(nxt, r, 1 - slot).start()
        a_copy(r).start()
    xu = xs[slot]
    x_lo = pltpu.bitcast(xu << 16, F32).astype(BF16)
    x_hi = pltpu.bitcast(xu & jnp.uint32(0xFFFF0000), F32).astype(BF16)
    half = xu.shape[1]
    ab2 = (jnp.dot(x_lo, wgu[:half], preferred_element_type=F32)
           + jnp.dot(x_hi, wgu[half:], preferred_element_type=F32))
    f = ab2.shape[1] // 2
    a, b = ab2[:, :f], ab2[:, f:]
    gate = jnp.concatenate([gate_ref[0]] * (f // LANES), axis=1)
    hm = (a * jax.nn.sigmoid(a) * b * gate).astype(BF16)
    y_scr[...] = jnp.dot(hm, wd[...], preferred_element_type=F32) * g2_ref[...]

    pltpu.make_async_copy(out_hbm.at[pl.ds(0, tc)], ab.at[slot], sem_a.at[slot]).wait()

    @pl.when(ct > 0)
    def _():
        o_wait(1 - slot)

    ab[slot] = ab[slot] + y_scr[...]
    for r in range(tc):
        o_copy(base, r, slot).start()

    @pl.when(s == last)
    def _():
        o_wait(slot)
        x_wait(1 - slot)


def moe_apply(h, x, idx, gate, w_gate, w_up, w_down, layer, g2):
    n, d = x.shape
    ne, cap = idx.shape
    f = w_down.shape[2]
    assert f % LANES == 0
    tc = _tile(cap, MOE_TC)
    gate_b = jnp.broadcast_to(gate[:, :, None], (ne, cap, LANES))
    kern = functools.partial(_moe_kernel, tc=tc, layer=layer)
    hbm = pl.BlockSpec(memory_space=pl.ANY)
    grid_spec = pltpu.PrefetchScalarGridSpec(
        num_scalar_prefetch=1,
        grid=(ne, cap // tc),
        in_specs=[pl.BlockSpec((1, tc, LANES), lambda e, c, idx: (e, c, 0)),
                  pl.BlockSpec((1, d), lambda e, c, idx: (0, 0)),
                  hbm, hbm, hbm, hbm, hbm],
        out_specs=hbm,
        scratch_shapes=[pltpu.VMEM((2, tc, d // 2), U32), pltpu.VMEM((2, tc, d), F32), pltpu.VMEM((tc, d), F32),
                        pltpu.VMEM((d, f), F32), pltpu.VMEM((d, f), F32), pltpu.VMEM((f, d), F32),
                        pltpu.VMEM((d, 2 * f), BF16), pltpu.VMEM((f, d), BF16),
                        pltpu.SemaphoreType.DMA((2,)), pltpu.SemaphoreType.DMA((2,)),
                        pltpu.SemaphoreType.DMA((2,)), pltpu.SemaphoreType.DMA((3,))],
    )
    return pl.pallas_call(
        kern,
        grid_spec=grid_spec,
        out_shape=jax.ShapeDtypeStruct((n, d), F32),
        input_output_aliases={7: 0},
        compiler_params=_params("arbitrary", "arbitrary"),
        name="moe_apply",
    )(idx.reshape(-1), gate_b, g2, h, w_gate, w_up, w_down, x)


def moe_layer(x, g, sh, sc, g2, w_r, w_gate, w_up, w_down, layer):
    n = x.shape[0]
    ne = w_r.shape[1]
    h, aff_t = router(x, g, sh, sc, w_r.T)
    idx, gate = topk_select(aff_t, EC_CAPACITY_FACTOR * n // ne)
    return moe_apply(h, x, idx, gate, w_gate, w_up, w_down, layer, g2)


def _rope_perm(width):
    quarter = HEAD_DIM // 4
    starts = jnp.array([0, 2, 4, 6, 1, 3, 5, 7]) * quarter
    blk = (starts[:, None] + jnp.arange(quarter)[None, :]).reshape(-1)
    return (jnp.arange(0, width, 2 * HEAD_DIM)[:, None] + blk[None, :]).reshape(-1)


def _permute_heads(w):
    rows, width = w.shape
    quarter = HEAD_DIM // 4
    w6 = w.reshape(rows, width // (2 * HEAD_DIM), 2, 2, 2, quarter)
    return jnp.transpose(w6, (0, 1, 4, 2, 3, 5)).reshape(rows, width)


def _rope_tables(n, n_ctx):
    quarter = HEAD_DIM // 4
    t = jnp.arange(n, dtype=I32)
    freqs = ROPE_THETA ** (-jnp.arange(quarter, dtype=F32) / quarter)
    ang_r = (t // GRID_W).astype(F32)[:, None] * freqs
    ang_c = (t % GRID_W).astype(F32)[:, None] * freqs
    cos = jnp.concatenate([jnp.cos(ang_r), jnp.cos(ang_c)] * 2, axis=1)
    sin = jnp.concatenate([jnp.sin(ang_r), jnp.sin(ang_c)] * 2, axis=1)
    cos = jnp.concatenate([cos, jnp.ones((n_ctx, HEAD_DIM), F32)], axis=0)
    sin = jnp.concatenate([sin, jnp.zeros((n_ctx, HEAD_DIM), F32)], axis=0)
    return cos, sin


def kernel(x, c, ctx, c_ctx, ada_w, ada_b, norm1_g, norm2_g, router_w, exp_w_gate, exp_w_up, exp_w_down,
           pool_w, pool_scale, gm_w_in, gm_v_g, gm_w_s, gm_b_s, gm_w_out, da_w_q, da_w_k, da_w_v, da_w_o,
           da_q_g, da_k_g, da_lam_q1, da_lam_k1, da_lam_q2, da_lam_k2, da_sub_g):
    bsz, n, d = x.shape
    assert bsz == 1 and c.shape[0] == 1
    n_ctx = ctx.shape[1]
    depth = ada_w.shape[0]
    xs, cs = x[0], ctx[0]

    s8 = jnp.concatenate([c, c_ctx[None], jnp.zeros((6, d), F32)], axis=0)
    mod = adaln(s8, ada_w, ada_b)

    def mods(i, row):
        return [mod[i, row:row + 1, k * d:(k + 1) * d] for k in range(6)]

    def row(v):
        return v.reshape(1, -1)

    for i in range(depth):
        kind, slot = i % N_MIXERS, i // N_MIXERS
        keep_ctx = any(j % N_MIXERS == 2 for j in range(i + 1, depth))
        sh1, sc1, g1, sh2, sc2, g2 = mods(i, 0)
        csh1, csc1, cg1, csh2, csc2, cg2 = mods(i, 1)
        n1, n2 = row(norm1_g[i]), row(norm2_g[i])
        streams = [(xs, sh1, sc1, g1)] + ([(cs, csh1, csc1, cg1)] if keep_ctx else [])

        if kind == 0:
            w = pool_w[slot].astype(BF16)
            outs = [pool_mixer(s, row_rstd(s), n1 * (1.0 + sc), sh, w, row(pool_scale[slot]), g)
                    for s, sh, sc, g in streams]
        elif kind == 1:
            w_in = gm_w_in[slot].astype(BF16)
            w_out = gm_w_out[slot].astype(BF16)
            w_s = gm_w_s[slot].astype(BF16)
            width = w_out.shape[0]
            bs_full = jnp.repeat(gm_b_s[slot].T, width // gm_w_s.shape[1], axis=1)
            outs = []
            for s, sh, sc, g in streams:
                uv, ssq = mm_gelu(norm_mod(s, n1, sh, sc, BF16), w_in)
                z = chunk_gate(uv, ssq, row(gm_v_g[slot]), w_s, bs_full)
                outs.append(mm_resid(z, w_out, s, g))
        else:
            assert not keep_ctx
            lam_init = 0.8 - 0.6 * math.exp(-0.3 * i)
            h_all = norm_mod_pair(xs, cs, n1, jnp.concatenate([sh1, csh1]), jnp.concatenate([sc1, csc1]), BF16)
            cos, sin = _rope_tables(n, n_ctx)
            gperm = (_rope_perm(2 * HEAD_DIM) % HEAD_DIM).reshape(2, HEAD_DIM)
            q_scale = HEAD_DIM ** -0.5 * math.log2(math.e)
            q = mm_qk(h_all, _permute_heads(da_w_q[slot]).astype(BF16), da_q_g[slot][gperm], cos, sin, q_scale, m=n)
            k = mm_qk(h_all, _permute_heads(da_w_k[slot]).astype(BF16), da_k_g[slot][gperm], cos, sin, 1.0)
            vt = mm_transposed(h_all, da_w_v[slot].T.astype(BF16))
            lamv = jnp.stack([da_lam_q1[slot], da_lam_k1[slot], da_lam_q2[slot], da_lam_k2[slot]])
            score_bound = (HEAD_DIM * q_scale * BOUND_SLACK) * jnp.max(jnp.abs(da_q_g[slot])) * jnp.max(jnp.abs(da_k_g[slot]))
            o = diff_attention(q, k, vt, lamv, row(da_sub_g[slot]), lam_init, score_bound)
            outs = [mm_resid(o, da_w_o[slot].astype(BF16), xs, g1)]

        xs = moe_layer(outs[0], n2, sh2, sc2, g2, router_w[i], exp_w_gate, exp_w_up, exp_w_down, i)
        if keep_ctx:
            cs = moe_layer(outs[1], n2, csh2, csc2, cg2, router_w[i], exp_w_gate, exp_w_up, exp_w_down, i)
    return xs[None]
```

```python
import functools
import math

import jax
import jax.numpy as jnp
from jax import lax
from jax.experimental import pallas as pl
from jax.experimental.pallas import tpu as pltpu

F32 = jnp.float32
BF16 = jnp.bfloat16
I32 = jnp.int32
U32 = jnp.uint32

NORM_EPS = 1e-6
LANES = 128
SUBLANES = 8
BF16_ROWS = 16
GRID_W = 64
CHUNK = 128
POOL_WINDOWS = (2, 4, 8, 16)
HEAD_DIM = 128
ROPE_THETA = 10000.0
EC_CAPACITY_FACTOR = 2
N_MIXERS = 3
VMEM_LIMIT = 56 * 1024 * 1024
ROW_TILE = 256
WIDE_TILE = 512
MM_TM, MM_TN = 1024, 1024
ADALN_TN = 1024
ATTN_TQ, ATTN_TK_MAX, ATTN_TC = 2048, 1536, 256
MOE_TC = 512
MOE_VMEM_LIMIT = 62 * 1024 * 1024
BOUND_SLACK = 1.01
MAX_FIXED_OFFSET = 60.0


def _params(*sem):
    return pltpu.CompilerParams(dimension_semantics=sem, vmem_limit_bytes=VMEM_LIMIT)


def _tile(n, t):
    t = min(n, t)
    assert n % t == 0, (n, t)
    return t


def _norm_mod(x, g, sh, sc):
    ms = jnp.mean(x * x, axis=-1, keepdims=True)
    return (x * lax.rsqrt(ms + NORM_EPS) * g) * (1.0 + sc) + sh


def _adaln_kernel(s_ref, w_ref, b_ref, o_ref):
    s = s_ref[...]
    s = s * jax.nn.sigmoid(s)
    o_ref[0] = jnp.dot(s, w_ref[0], preferred_element_type=F32) + b_ref[0]


def adaln(s8, ada_w, ada_b):
    depth, d, n6 = ada_w.shape
    tn = _tile(n6, ADALN_TN)
    return pl.pallas_call(
        _adaln_kernel,
        grid=(depth, n6 // tn),
        in_specs=[pl.BlockSpec((8, d), lambda l, j: (0, 0)),
                  pl.BlockSpec((1, d, tn), lambda l, j: (l, 0, j)),
                  pl.BlockSpec((1, 1, tn), lambda l, j: (l, 0, j))],
        out_specs=pl.BlockSpec((1, 8, tn), lambda l, j: (l, 0, j)),
        out_shape=jax.ShapeDtypeStruct((depth, 8, n6), F32),
        compiler_params=_params("parallel", "parallel"),
        name="adaln",
    )(s8, ada_w, ada_b.reshape(depth, 1, n6))


def _norm_mod_kernel(x_ref, g_ref, sh_ref, sc_ref, o_ref):
    o_ref[...] = _norm_mod(x_ref[...], g_ref[...], sh_ref[...], sc_ref[...]).astype(o_ref.dtype)


def norm_mod(x, g, sh, sc, dtype):
    n, d = x.shape
    tm = _tile(n, ROW_TILE)
    vec = pl.BlockSpec((1, d), lambda i: (0, 0))
    return pl.pallas_call(
        _norm_mod_kernel,
        grid=(n // tm,),
        in_specs=[pl.BlockSpec((tm, d), lambda i: (i, 0)), vec, vec, vec],
        out_specs=pl.BlockSpec((tm, d), lambda i: (i, 0)),
        out_shape=jax.ShapeDtypeStruct((n, d), dtype),
        compiler_params=_params("parallel"),
        name="norm_mod",
    )(x, g, sh, sc)


def _norm_mod_pair_kernel(x_ref, c_ref, g_ref, sh_ref, sc_ref, o_ref, *, nx):
    is_ctx = pl.program_id(0) >= nx
    src = jnp.where(is_ctx, c_ref[...], x_ref[...])
    sh = jnp.where(is_ctx, sh_ref[1:2], sh_ref[0:1])
    sc = jnp.where(is_ctx, sc_ref[1:2], sc_ref[0:1])
    o_ref[...] = _norm_mod(src, g_ref[...], sh, sc).astype(o_ref.dtype)


def norm_mod_pair(x, ctx, g, sh2, sc2, dtype):
    n, d = x.shape
    nc = ctx.shape[0]
    tm = _tile(nc, ROW_TILE)
    assert n % tm == 0
    nx = n // tm
    vec = pl.BlockSpec((1, d), lambda i: (0, 0))
    vec2 = pl.BlockSpec((2, d), lambda i: (0, 0))
    return pl.pallas_call(
        functools.partial(_norm_mod_pair_kernel, nx=nx),
        grid=(nx + nc // tm,),
        in_specs=[pl.BlockSpec((tm, d), lambda i: (jnp.minimum(i, nx - 1), 0)),
                  pl.BlockSpec((tm, d), lambda i: (jnp.maximum(i - nx, 0), 0)), vec, vec2, vec2],
        out_specs=pl.BlockSpec((tm, d), lambda i: (i, 0)),
        out_shape=jax.ShapeDtypeStruct((n + nc, d), dtype),
        compiler_params=_params("parallel"),
        name="norm_mod_pair",
    )(x, ctx, g, sh2, sc2)


def _rstd_kernel(x_ref, o_ref):
    x = x_ref[...]
    o_ref[...] = jnp.broadcast_to(lax.rsqrt(jnp.mean(x * x, axis=-1, keepdims=True) + NORM_EPS), o_ref.shape)


def row_rstd(x):
    n, d = x.shape
    tm = _tile(n, ROW_TILE)
    return pl.pallas_call(
        _rstd_kernel,
        grid=(n // tm,),
        in_specs=[pl.BlockSpec((tm, d), lambda i: (i, 0))],
        out_specs=pl.BlockSpec((tm, LANES), lambda i: (i, 0)),
        out_shape=jax.ShapeDtypeStruct((n, LANES), F32),
        compiler_params=_params("parallel"),
        name="row_rstd",
    )(x)


def _pool_kernel(cur_ref, prev_ref, next_ref, rc_ref, rp_ref, rn_ref, gs_ref, sh_ref, w_ref, ps_ref, g1_ref, o_ref,
                 *, n, tm):
    g = pl.program_id(0)
    i = pl.program_id(1)
    last = pl.num_programs(1) - 1
    reps = cur_ref.shape[1] // LANES

    def modulated(x, r):
        return x * jnp.concatenate([r] * reps, axis=1) * gs_ref[...] + sh_ref[...]

    t = i * tm + lax.broadcasted_iota(I32, (tm, 1), 0)

    for gi, win in enumerate(POOL_WINDOWS):
        @pl.when(g == gi)
        def _(win=win):
            x_cur = cur_ref[...]
            cur = modulated(x_cur, rc_ref[...])
            prev = jnp.where(i == 0, 0.0, modulated(prev_ref[...], rp_ref[...]))
            nxt = jnp.where(i == last, 0.0, modulated(next_ref[...], rn_ref[...]))
            ext = jnp.concatenate([prev, cur, nxt], axis=0)
            half = win // 2
            s = ext
            step = 1
            while step < win:
                m = s.shape[0] - step
                s = s[:m] + s[step:step + m]
                step *= 2
            wsum = s[SUBLANES - half:SUBLANES - half + tm]
            cnt = jnp.minimum(t + half, n) - jnp.maximum(t - half, 0)
            dlt = (wsum / cnt.astype(F32) - cur).astype(BF16)
            y = jnp.dot(dlt, w_ref[0], preferred_element_type=F32) * ps_ref[...]
            o_ref[...] = x_cur + g1_ref[...] * y


def pool_mixer(x, rstd, gs, sh, w, pscale, g1):
    n, d = x.shape
    ng, dg, _ = w.shape
    tm = _tile(n, WIDE_TILE)
    assert max(POOL_WINDOWS) // 2 <= SUBLANES
    halo_blocks = n // SUBLANES
    kern = functools.partial(_pool_kernel, n=n, tm=tm)
    col = pl.BlockSpec((1, dg), lambda g, i: (0, g))

    def halo(width, sel):
        per_tile = tm // SUBLANES
        before = pl.BlockSpec((SUBLANES, width), lambda g, i: (jnp.maximum(i * per_tile - 1, 0), sel(g)))
        after = pl.BlockSpec((SUBLANES, width),
                             lambda g, i: (jnp.minimum((i + 1) * per_tile, halo_blocks - 1), sel(g)))
        return pl.BlockSpec((tm, width), lambda g, i: (i, sel(g))), before, after

    return pl.pallas_call(
        kern,
        grid=(ng, n // tm),
        in_specs=[*halo(dg, lambda g: g), *halo(LANES, lambda g: 0), col, col,
                  pl.BlockSpec((1, dg, dg), lambda g, i: (g, 0, 0)), col, col],
        out_specs=pl.BlockSpec((tm, dg), lambda g, i: (i, g)),
        out_shape=jax.ShapeDtypeStruct((n, d), F32),
        compiler_params=_params("parallel", "parallel"),
        name="pool_mixer",
    )(x, x, x, rstd, rstd, rstd, gs, sh, w, pscale, g1)


def _mm_call(kern, a, w, extra, extra_specs, out_shape, out_specs, tm, tn, name, scratch=(), m=None):
    k = a.shape[1]
    m = a.shape[0] if m is None else m
    n = w.shape[1]
    return pl.pallas_call(
        kern,
        grid=(pl.cdiv(m, tm), n // tn),
        in_specs=[pl.BlockSpec((tm, k), lambda i, j: (i, 0)),
                  pl.BlockSpec((k, tn), lambda i, j: (0, j))] + list(extra_specs),
        out_specs=out_specs,
        out_shape=out_shape,
        scratch_shapes=list(scratch),
        compiler_params=_params("parallel", "arbitrary"),
        name=name,
    )(a, w, *extra)


def _mm_t_kernel(a_ref, wt_ref, o_ref):
    o_ref[...] = lax.dot_general(wt_ref[...], a_ref[...], (((1,), (1,)), ((), ())),
                                 preferred_element_type=F32).astype(o_ref.dtype)


def mm_transposed(a, wt, dtype=BF16):
    m, k = a.shape
    n = wt.shape[0]
    tm, tn = min(m, MM_TM), _tile(n, MM_TN)
    return pl.pallas_call(
        _mm_t_kernel,
        grid=(pl.cdiv(m, tm), n // tn),
        in_specs=[pl.BlockSpec((tm, k), lambda i, j: (i, 0)),
                  pl.BlockSpec((tn, k), lambda i, j: (j, 0))],
        out_specs=pl.BlockSpec((tn, tm), lambda i, j: (j, i)),
        out_shape=jax.ShapeDtypeStruct((n, m), dtype),
        compiler_params=_params("parallel", "arbitrary"),
        name="mm_transposed",
    )(a, wt)


def _mm_qk_kernel(a_ref, w_ref, g_ref, cos_ref, sin_ref, o_ref, *, scale):
    y = jnp.dot(a_ref[...], w_ref[...], preferred_element_type=F32)
    tn = y.shape[1]
    cos = cos_ref[...]
    sin = sin_ref[...]
    ga_cos, ga_sin = cos * g_ref[0:1], sin * g_ref[0:1]
    gb_cos, gb_sin = cos * g_ref[1:2], sin * g_ref[1:2]
    comp0 = lax.broadcasted_iota(I32, (1, HEAD_DIM), 1) < HEAD_DIM // 2
    for hd in range(tn // (2 * HEAD_DIM)):
        a = y[:, 2 * hd * HEAD_DIM:(2 * hd + 1) * HEAD_DIM]
        b = y[:, (2 * hd + 1) * HEAD_DIM:(2 * hd + 2) * HEAD_DIM]
        sq = a * a + b * b
        ssq0 = jnp.sum(jnp.where(comp0, sq, 0.0), axis=-1, keepdims=True)
        ssq1 = jnp.sum(jnp.where(comp0, 0.0, sq), axis=-1, keepdims=True)
        rstd = jnp.where(comp0, lax.rsqrt(ssq0 * (1.0 / HEAD_DIM) + NORM_EPS),
                         lax.rsqrt(ssq1 * (1.0 / HEAD_DIM) + NORM_EPS)) * scale
        o_ref[:, 2 * hd * HEAD_DIM:(2 * hd + 1) * HEAD_DIM] = ((a * ga_cos - b * gb_sin) * rstd).astype(o_ref.dtype)
        o_ref[:, (2 * hd + 1) * HEAD_DIM:(2 * hd + 2) * HEAD_DIM] = ((b * gb_cos + a * ga_sin) * rstd).astype(o_ref.dtype)


def mm_qk(a, w, gain, cos, sin, scale, m=None):
    m = a.shape[0] if m is None else m
    n = w.shape[1]
    tm, tn = min(m, MM_TM), _tile(n, MM_TN)
    kern = functools.partial(_mm_qk_kernel, scale=scale)
    tab = pl.BlockSpec((tm, HEAD_DIM), lambda i, j: (i, 0))
    return _mm_call(kern, a, w, (gain, cos, sin),
                    (pl.BlockSpec((2, HEAD_DIM), lambda i, j: (0, 0)), tab, tab),
                    jax.ShapeDtypeStruct((m, n), BF16),
                    pl.BlockSpec((tm, tn), lambda i, j: (i, j)), tm, tn, "mm_qk", m=m)


def _mm_gelu_kernel(a_ref, w_ref, o_ref, ssq_ref, *, nj_half):
    j = pl.program_id(1)
    y = jax.nn.gelu(jnp.dot(a_ref[...], w_ref[...], preferred_element_type=F32))
    o_ref[...] = y.astype(o_ref.dtype)

    @pl.when(j == nj_half)
    def _():
        ssq_ref[...] = jnp.zeros_like(ssq_ref)

    @pl.when(j >= nj_half)
    def _():
        ssq_ref[...] += jnp.sum(y * y, axis=-1, keepdims=True)


def mm_gelu(a, w):
    m, _ = a.shape
    n = w.shape[1]
    tm, tn = min(m, MM_TM), _tile(n // 2, MM_TN)
    kern = functools.partial(_mm_gelu_kernel, nj_half=(n // 2) // tn)
    return _mm_call(kern, a, w, (), (),
                    (jax.ShapeDtypeStruct((m, n), BF16), jax.ShapeDtypeStruct((m, LANES), F32)),
                    (pl.BlockSpec((tm, tn), lambda i, j: (i, j)),
                     pl.BlockSpec((tm, LANES), lambda i, j: (i, 0))), tm, tn, "mm_gelu")


def _mm_resid_kernel(a_ref, w_ref, x_ref, g_ref, o_ref):
    y = jnp.dot(a_ref[...], w_ref[...], preferred_element_type=F32)
    o_ref[...] = x_ref[...] + g_ref[...] * y


def mm_resid(a, w, x, g1):
    m, _ = a.shape
    n = w.shape[1]
    tm, tn = min(m, MM_TM), _tile(n, MM_TN)
    blk = pl.BlockSpec((tm, tn), lambda i, j: (i, j))
    return _mm_call(_mm_resid_kernel, a, w, (x, g1),
                    (blk, pl.BlockSpec((1, tn), lambda i, j: (0, j))),
                    jax.ShapeDtypeStruct((m, n), F32), blk, tm, tn, "mm_resid")


def _chunk_gate_kernel(u_ref, v_ref, ssq_ref, vg_ref, ws_ref, bs_ref, z_ref, *, width):
    tm = u_ref.shape[0]
    rstd = lax.rsqrt(ssq_ref[:, :1] * (1.0 / width) + NORM_EPS)

    def body(g, carry):
        col = pl.multiple_of(g * LANES, LANES)
        vg = vg_ref[:, pl.ds(col, LANES)]
        chunks = [slice(c * CHUNK, (c + 1) * CHUNK) for c in range(tm // CHUNK)]
        vn = jnp.concatenate([(v_ref[rows, pl.ds(col, LANES)].astype(F32) * rstd[rows] * vg).astype(BF16)
                              for rows in chunks], axis=1)
        sv = jnp.dot(ws_ref[g], vn, preferred_element_type=F32)
        bias = bs_ref[:, pl.ds(col, LANES)]
        for c, rows in enumerate(chunks):
            gate = sv[:, c * LANES:(c + 1) * LANES] + bias
            z_ref[rows, pl.ds(col, LANES)] = (u_ref[rows, pl.ds(col, LANES)].astype(F32) * gate).astype(BF16)
        return carry

    lax.fori_loop(0, width // LANES, body, 0, unroll=8)


def chunk_gate(uv, ssq, vg, ws, bs_full):
    n, w2 = uv.shape
    width = w2 // 2
    tm = _tile(n, WIDE_TILE)
    return pl.pallas_call(
        functools.partial(_chunk_gate_kernel, width=width),
        grid=(n // tm,),
        in_specs=[pl.BlockSpec((tm, width), lambda i: (i, 0)),
                  pl.BlockSpec((tm, width), lambda i: (i, 1)),
                  pl.BlockSpec((tm, LANES), lambda i: (i, 0)),
                  pl.BlockSpec((1, width), lambda i: (0, 0)),
                  pl.BlockSpec(ws.shape, lambda i: (0, 0, 0)),
                  pl.BlockSpec((CHUNK, width), lambda i: (0, 0))],
        out_specs=pl.BlockSpec((tm, width), lambda i: (i, 0)),
        out_shape=jax.ShapeDtypeStruct((n, width), BF16),
        compiler_params=_params("parallel"),
        name="chunk_gate",
    )(uv, uv, ssq, vg, ws, bs_full)


def _attn_kernel(lamv_ref, off_ref, q_ref, k_ref, vt_ref, sg_ref, o_ref, m_scr, l_scr, acc_scr, p_scr,
                 *, lam_init, tc, bounded):
    ki = pl.program_id(2)
    tk = k_ref.shape[0]
    nbuf = p_scr.shape[0]
    nchunk = q_ref.shape[0] // tc
    lane = lax.broadcasted_iota(I32, (1, 2 * HEAD_DIM), 1) % HEAD_DIM
    qcs = [q_ref[...] * jnp.where((lane < HEAD_DIM // 2) == (c == 0), 1.0, 0.0).astype(BF16) for c in range(2)]

    @pl.when(ki == 0)
    def _():
        l_scr[...] = jnp.zeros_like(l_scr)
        acc_scr[...] = jnp.zeros_like(acc_scr)
        if not bounded:
            m_scr[...] = jnp.full_like(m_scr, -jnp.inf)

    vt = vt_ref[...]
    kt = k_ref[...]
    scores = {}
    for c in range(2):
        for r in range(nchunk):
            scores[r, c] = lax.dot_general(kt, qcs[c][r * tc:(r + 1) * tc], (((1,), (1,)), ((), ())),
                                           preferred_element_type=F32)
    for r in range(nchunk):
        qs = slice(r * tc, (r + 1) * tc)
        for c in range(2):
            buf = (2 * r + c) % nbuf
            s = scores[r, c]
            if bounded:
                m_new = off_ref[:, :1]
            else:
                m_prev = m_scr[c, :, qs]
                m_new = jnp.maximum(m_prev, jnp.max(s, axis=0, keepdims=True))
            psum = jnp.zeros((BF16_ROWS, tc), F32)
            for g in range(tk // BF16_ROWS):
                keys = slice(BF16_ROWS * g, BF16_ROWS * (g + 1))
                pg = jnp.exp2(s[keys] - m_new)
                psum = psum + pg
                p_scr[buf, keys, :] = pg.astype(BF16)
            lsum = jnp.sum(psum, axis=0, keepdims=True)
            pv = jnp.dot(vt, p_scr[buf], preferred_element_type=F32)
            if bounded:
                l_scr[c, :, qs] += lsum
                acc_scr[c, :, qs] += pv
            else:
                alpha = jnp.exp2(m_prev - m_new)
                l_scr[c, :, qs] = alpha * l_scr[c, :, qs] + lsum
                acc_scr[c, :, qs] = alpha * acc_scr[c, :, qs] + pv
                m_scr[c, :, qs] = m_new

    @pl.when(ki == pl.num_programs(2) - 1)
    def _():
        lv = lamv_ref[...]
        lam = (jnp.exp(jnp.sum(lv[0:1] * lv[1:2], axis=-1, keepdims=True))
               - jnp.exp(jnp.sum(lv[2:3] * lv[3:4], axis=-1, keepdims=True)) + lam_init)
        ot = acc_scr[0] / l_scr[0] - lam * (acc_scr[1] / l_scr[1])
        ot = ot * lax.rsqrt(jnp.mean(ot * ot, axis=0, keepdims=True) + NORM_EPS)
        o_ref[...] = (ot.T * (sg_ref[...] * (1.0 - lam_init))).astype(o_ref.dtype)


def _key_tile(nk, cap):
    best = LANES
    for t in range(LANES, cap + 1, LANES):
        if nk % t == 0:
            best = t
    return best


def diff_attention(q, k, vt, lamv, sub_g, lam_init, score_bound):
    n, d = q.shape
    nk = k.shape[0]
    hw = 2 * HEAD_DIM
    heads = d // hw
    tq = _tile(n, ATTN_TQ)
    tk = _key_tile(nk, ATTN_TK_MAX)
    tc = min(tq, ATTN_TC)

    safe = score_bound <= MAX_FIXED_OFFSET
    off = jnp.full((1, LANES), score_bound, F32)

    def call(bounded):
        kern = functools.partial(_attn_kernel, lam_init=lam_init, tc=tc, bounded=bounded)
        return pl.pallas_call(
            kern,
            grid=(heads, n // tq, nk // tk),
            in_specs=[pl.BlockSpec((4, HEAD_DIM), lambda h, i, j: (0, 0)),
                      pl.BlockSpec((1, LANES), lambda h, i, j: (0, 0)),
                      pl.BlockSpec((tq, hw), lambda h, i, j: (i, h)),
                      pl.BlockSpec((tk, hw), lambda h, i, j: (j, h)),
                      pl.BlockSpec((hw, tk), lambda h, i, j: (h, j)),
                      pl.BlockSpec((1, hw), lambda h, i, j: (0, 0))],
            out_specs=pl.BlockSpec((tq, hw), lambda h, i, j: (i, h)),
            out_shape=jax.ShapeDtypeStruct((n, d), BF16),
            scratch_shapes=[pltpu.VMEM((2, 1, tq), F32), pltpu.VMEM((2, 1, tq), F32),
                            pltpu.VMEM((2, hw, tq), F32), pltpu.VMEM((2 * (tq // tc), tk, tc), BF16)],
            compiler_params=_params("parallel", "parallel", "arbitrary"),
            name="diff_attention_bounded" if bounded else "diff_attention_online",
        )(lamv, off, q, k, vt, sub_g)

    return lax.cond(safe, lambda: call(True), lambda: call(False))


def _router_kernel(x_ref, g_ref, sh_ref, sc_ref, wrt_ref, h_ref, aff_ref):
    h = _norm_mod(x_ref[...], g_ref[...], sh_ref[...], sc_ref[...])
    half = h.shape[1] // 2
    h_hi = h.astype(BF16)
    h_hi32 = h_hi.astype(F32)
    bits = pltpu.bitcast(h_hi32, U32)
    h_ref[...] = (bits[:, :half] >> 16) | (bits[:, half:] & jnp.uint32(0xFFFF0000))
    h_lo = (h - h_hi32).astype(BF16)
    w = wrt_ref[...]
    w_hi = w.astype(BF16)
    w_lo = (w - w_hi.astype(F32)).astype(BF16)
    dims = (((1,), (1,)), ((), ()))
    logits = (lax.dot_general(w_hi, h_hi, dims, preferred_element_type=F32)
              + lax.dot_general(w_lo, h_hi, dims, preferred_element_type=F32)
              + lax.dot_general(w_hi, h_lo, dims, preferred_element_type=F32))
    ex = jnp.exp(logits - jnp.max(logits, axis=0, keepdims=True))
    aff_ref[...] = ex / jnp.sum(ex, axis=0, keepdims=True)


def router(x, g, sh, sc, w_r_t):
    n, d = x.shape
    e = w_r_t.shape[0]
    tm = _tile(n, ROW_TILE)
    vec = pl.BlockSpec((1, d), lambda i: (0, 0))
    return pl.pallas_call(
        _router_kernel,
        grid=(n // tm,),
        in_specs=[pl.BlockSpec((tm, d), lambda i: (i, 0)), vec, vec, vec,
                  pl.BlockSpec((e, d), lambda i: (0, 0))],
        out_specs=(pl.BlockSpec((tm, d // 2), lambda i: (i, 0)), pl.BlockSpec((e, tm), lambda i: (0, i))),
        out_shape=(jax.ShapeDtypeStruct((n, d // 2), U32), jax.ShapeDtypeStruct((e, n), F32)),
        compiler_params=_params("parallel"),
        name="router",
    )(x, g, sh, sc, w_r_t)


def _split3(a):
    hi = a.astype(BF16)
    r1 = a - hi.astype(F32)
    mid = r1.astype(BF16)
    lo = (r1 - mid.astype(F32)).astype(BF16)
    return hi, mid, lo


def _topk_kernel(abt_ref, atb_ref, idx_ref, gate_ref, thr_scr, cut_scr, *, cap, cap_pad):
    ne, nb, _ = abt_ref.shape
    ntok = nb * LANES
    bits = pltpu.bitcast(abt_ref[...], I32)
    tok_bt = (lax.broadcasted_iota(I32, (1, nb, LANES), 1) * LANES
              + lax.broadcasted_iota(I32, (1, nb, LANES), 2))

    def count(ones):
        c = jnp.sum(ones, axis=1, keepdims=True)
        return jnp.sum(c, axis=2, keepdims=True)

    def thr_body(_, carry):
        lo, hi = carry
        mid = lo + ((hi - lo + 1) >> 1)
        ok = count(jnp.where(bits >= mid, 1, 0)) >= cap
        return jnp.where(ok, mid, lo), jnp.where(ok, hi, mid - 1)

    lo0 = jnp.zeros((ne, 1, 1), I32)
    hi0 = jnp.full((ne, 1, 1), 0x7F800000, I32)
    thr, _ = lax.fori_loop(0, 31, thr_body, (lo0, hi0))

    tok_eq = jnp.where(bits == thr, tok_bt, ntok)
    need = cap - count(jnp.where(bits > thr, 1, 0))

    def cut_body(_, carry):
        lo, hi = carry
        mid = (lo + hi) >> 1
        ok = count(jnp.where(tok_eq < mid, 1, 0)) >= need
        return jnp.where(ok, lo, mid), jnp.where(ok, mid, hi)

    _, cut = lax.fori_loop(0, int(math.log2(ntok)) + 1, cut_body,
                           (jnp.zeros((ne, 1, 1), I32), jnp.full((ne, 1, 1), ntok, I32)))
    thr_scr[...] = jnp.broadcast_to(thr, thr_scr.shape)
    cut_scr[...] = jnp.broadcast_to(cut, cut_scr.shape)

    r_i = lax.broadcasted_iota(I32, (LANES, LANES), 0)
    c_i = lax.broadcasted_iota(I32, (LANES, LANES), 1)
    tri_t = jnp.where(c_i <= r_i, 1.0, 0.0).astype(BF16)
    rb_i = lax.broadcasted_iota(I32, (nb, nb), 0)
    cb_i = lax.broadcasted_iota(I32, (nb, nb), 1)
    tri_b = jnp.where(cb_i <= rb_i, 1.0, 0.0).astype(BF16)
    tok_b = lax.broadcasted_iota(I32, (nb, LANES), 0) * LANES + lax.broadcasted_iota(I32, (nb, LANES), 1)
    tok_t = lax.broadcasted_iota(I32, (LANES, nb), 1) * LANES + lax.broadcasted_iota(I32, (LANES, nb), 0)
    slot = lax.broadcasted_iota(I32, (1, cap_pad), 1).astype(F32)
    blk_iota = lax.broadcasted_iota(I32, (nb, 1), 0).astype(F32)
    lane_iota = lax.broadcasted_iota(I32, (LANES, 1), 0).astype(F32)

    def select(b, tok, th, ct):
        return jnp.where(b > th, 1.0, jnp.where(b == th, jnp.where(tok < ct, 1.0, 0.0), 0.0))

    def per_expert(e, carry):
        th = thr_scr[e][:1, :1]
        ct = cut_scr[e][:1, :1]
        a_tb = atb_ref[e]
        sel_bt = select(pltpu.bitcast(abt_ref[e], I32), tok_b, th, ct)
        sel_tb = select(pltpu.bitcast(a_tb, I32), tok_t, th, ct)
        cw_t = jnp.dot(tri_t, sel_tb.astype(BF16), preferred_element_type=F32)
        tot = jnp.sum(sel_bt, axis=1, keepdims=True)
        incl = jnp.dot(tri_b, jnp.broadcast_to(tot, (nb, LANES)).astype(BF16),
                       preferred_element_type=F32)[:, :1]
        excl = incl - tot
        blk = jnp.sum(jnp.where(incl <= slot, 1.0, 0.0), axis=0, keepdims=True)
        onehot = jnp.where(blk_iota == blk, 1.0, 0.0)
        oh16 = onehot.astype(BF16)
        g_t = jnp.dot(cw_t.astype(BF16), oh16, preferred_element_type=F32)
        base = jnp.sum(onehot * excl, axis=0, keepdims=True)
        tl = jnp.sum(jnp.where(g_t + base <= slot, 1.0, 0.0), axis=0, keepdims=True)
        idx_ref[pl.ds(e, 1), :] = (blk * LANES + tl).astype(I32)
        hi, mid, lo = _split3(a_tb)
        rows = (jnp.dot(hi, oh16, preferred_element_type=F32)
                + jnp.dot(mid, oh16, preferred_element_type=F32)
                + jnp.dot(lo, oh16, preferred_element_type=F32))
        gate_ref[pl.ds(e, 1), :] = jnp.sum(jnp.where(lane_iota == tl, rows, 0.0), axis=0, keepdims=True)
        return carry

    lax.fori_loop(0, ne, per_expert, 0)


def topk_select(aff_t, cap):
    ne, n = aff_t.shape
    nb = max(LANES, -(-n // LANES))
    nb = -(-nb // LANES) * LANES
    cap_pad = max(LANES, cap)
    padded = jnp.pad(aff_t, ((0, 0), (0, nb * LANES - n)), constant_values=-1.0)
    a_bt = padded.reshape(ne, nb, LANES)
    a_tb = jnp.swapaxes(a_bt, 1, 2)
    kern = functools.partial(_topk_kernel, cap=cap, cap_pad=cap_pad)
    idx, gate = pl.pallas_call(
        kern,
        out_shape=(jax.ShapeDtypeStruct((ne, cap_pad), I32), jax.ShapeDtypeStruct((ne, cap_pad), F32)),
        scratch_shapes=[pltpu.VMEM((ne, 8, LANES), I32), pltpu.VMEM((ne, 8, LANES), I32)],
        compiler_params=pltpu.CompilerParams(vmem_limit_bytes=VMEM_LIMIT),
        name="topk_select",
    )(a_bt, a_tb)
    return idx[:, :cap], gate[:, :cap]


def _moe_kernel(idx_ref, gate_ref, g2_ref, h_hbm, wg_hbm, wu_hbm, wd_hbm, xin_hbm, out_hbm,
                xs, ab, y_scr, wg_f32, wu_f32, wd_f32, wgu, wd, sem_x, sem_a, sem_o, sem_w, *, tc, layer):
    del xin_hbm
    nct = pl.num_programs(1)
    ct = pl.program_id(1)
    s = pl.program_id(0) * nct + ct
    last = pl.num_programs(0) * nct - 1
    slot = s % 2
    base = s * tc
    nxt = jnp.minimum(s + 1, last) * tc

    def x_copy(b, r, sl):
        return pltpu.make_async_copy(h_hbm.at[pl.ds(idx_ref[b + r], 1)], xs.at[sl, pl.ds(r, 1)], sem_x.at[sl])

    def a_copy(r):
        return pltpu.make_async_copy(out_hbm.at[pl.ds(idx_ref[base + r], 1)], ab.at[slot, pl.ds(r, 1)],
                                     sem_a.at[slot])

    def o_copy(b, r, sl):
        return pltpu.make_async_copy(ab.at[sl, pl.ds(r, 1)], out_hbm.at[pl.ds(idx_ref[b + r], 1)], sem_o.at[sl])

    def rows(fn):
        def body(r, c):
            fn(r)
            return c
        lax.fori_loop(0, tc, body, 0, unroll=8)

    ne = pl.num_programs(0)
    expert = pl.program_id(0)

    def w_copies(e):
        return (pltpu.make_async_copy(wg_hbm.at[layer, e], wg_f32, sem_w.at[0]),
                pltpu.make_async_copy(wu_hbm.at[layer, e], wu_f32, sem_w.at[1]),
                pltpu.make_async_copy(wd_hbm.at[layer, e], wd_f32, sem_w.at[2]))

    @pl.when(s == 0)
    def _():
        for cp in w_copies(0):
            cp.start()
        rows(lambda r: x_copy(0, r, 0).start())

    @pl.when(ct == 0)
    def _():
        for cp in w_copies(expert):
            cp.wait()
        f = wd.shape[0]
        wgu[:, :f] = wg_f32[...].astype(BF16)
        wgu[:, f:] = wu_f32[...].astype(BF16)
        wd[...] = wd_f32[...].astype(BF16)

    @pl.when(jnp.logical_and(ct == 0, expert + 1 < ne))
    def _():
        for cp in w_copies(expert + 1):
            cp.start()

    def x_wait(sl):
        pltpu.make_async_copy(h_hbm.at[pl.ds(0, tc)], xs.at[sl], sem_x.at[sl]).wait()

    def o_wait(sl):
        pltpu.make_async_copy(ab.at[sl], out_hbm.at[pl.ds(0, tc)], sem_o.at[sl]).wait()

    x_wait(slot)

    @pl.when(jnp.logical_and(ct == 0, s > 0))
    def _():
        o_wait(1 - slot)

    for r in range(tc):
        x_copy(nxt, r, 1 - slot).start()
        a_copy(r).start()
    xu = xs[slot]
    x_lo = pltpu.bitcast(xu << 16, F32).astype(BF16)
    x_hi = pltpu.bitcast(xu & jnp.uint32(0xFFFF0000), F32).astype(BF16)
    half = xu.shape[1]
    ab2 = (jnp.dot(x_lo, wgu[:half], preferred_element_type=F32)
           + jnp.dot(x_hi, wgu[half:], preferred_element_type=F32))
    f = ab2.shape[1] // 2
    a, b = ab2[:, :f], ab2[:, f:]
    gate = jnp.concatenate([gate_ref[0]] * (f // LANES), axis=1)
    hm = (a * jax.nn.sigmoid(a) * b * gate).astype(BF16)
    y_scr[...] = jnp.dot(hm, wd[...], preferred_element_type=F32) * g2_ref[...]

    pltpu.make_async_copy(out_hbm.at[pl.ds(0, tc)], ab.at[slot], sem_a.at[slot]).wait()

    @pl.when(ct > 0)
    def _():
        o_wait(1 - slot)

    ab[slot] = ab[slot] + y_scr[...]
    for r in range(tc):
        o_copy(base, r, slot).start()

    @pl.when(s == last)
    def _():
        o_wait(slot)
        x_wait(1 - slot)


def moe_apply(h, x, idx, gate, w_gate, w_up, w_down, layer, g2):
    n, d = x.shape
    ne, cap = idx.shape
    f = w_down.shape[2]
    assert f % LANES == 0
    tc = _tile(cap, MOE_TC)
    gate_b = jnp.broadcast_to(gate[:, :, None], (ne, cap, LANES))
    kern = functools.partial(_moe_kernel, tc=tc, layer=layer)
    hbm = pl.BlockSpec(memory_space=pl.ANY)
    grid_spec = pltpu.PrefetchScalarGridSpec(
        num_scalar_prefetch=1,
        grid=(ne, cap // tc),
        in_specs=[pl.BlockSpec((1, tc, LANES), lambda e, c, idx: (e, c, 0)),
                  pl.BlockSpec((1, d), lambda e, c, idx: (0, 0)),
                  hbm, hbm, hbm, hbm, hbm],
        out_specs=hbm,
        scratch_shapes=[pltpu.VMEM((2, tc, d // 2), U32), pltpu.VMEM((2, tc, d), F32), pltpu.VMEM((tc, d), F32),
                        pltpu.VMEM((d, f), F32), pltpu.VMEM((d, f), F32), pltpu.VMEM((f, d), F32),
                        pltpu.VMEM((d, 2 * f), BF16), pltpu.VMEM((f, d), BF16),
                        pltpu.SemaphoreType.DMA((2,)), pltpu.SemaphoreType.DMA((2,)),
                        pltpu.SemaphoreType.DMA((2,)), pltpu.SemaphoreType.DMA((3,))],
    )
    return pl.pallas_call(
        kern,
        grid_spec=grid_spec,
        out_shape=jax.ShapeDtypeStruct((n, d), F32),
        input_output_aliases={7: 0},
        compiler_params=pltpu.CompilerParams(dimension_semantics=("arbitrary", "arbitrary"),
                                             vmem_limit_bytes=MOE_VMEM_LIMIT),
        name="moe_apply",
    )(idx.reshape(-1), gate_b, g2, h, w_gate, w_up, w_down, x)


def moe_layer(x, g, sh, sc, g2, w_r, w_gate, w_up, w_down, layer):
    n = x.shape[0]
    ne = w_r.shape[1]
    h, aff_t = router(x, g, sh, sc, w_r.T)
    idx, gate = topk_select(aff_t, EC_CAPACITY_FACTOR * n // ne)
    return moe_apply(h, x, idx, gate, w_gate, w_up, w_down, layer, g2)


def _rope_perm(width):
    quarter = HEAD_DIM // 4
    starts = jnp.array([0, 2, 4, 6, 1, 3, 5, 7]) * quarter
    blk = (starts[:, None] + jnp.arange(quarter)[None, :]).reshape(-1)
    return (jnp.arange(0, width, 2 * HEAD_DIM)[:, None] + blk[None, :]).reshape(-1)


def _permute_heads(w):
    rows, width = w.shape
    quarter = HEAD_DIM // 4
    w6 = w.reshape(rows, width // (2 * HEAD_DIM), 2, 2, 2, quarter)
    return jnp.transpose(w6, (0, 1, 4, 2, 3, 5)).reshape(rows, width)


def _rope_tables(n, n_ctx):
    quarter = HEAD_DIM // 4
    t = jnp.arange(n, dtype=I32)
    freqs = ROPE_THETA ** (-jnp.arange(quarter, dtype=F32) / quarter)
    ang_r = (t // GRID_W).astype(F32)[:, None] * freqs
    ang_c = (t % GRID_W).astype(F32)[:, None] * freqs
    cos = jnp.concatenate([jnp.cos(ang_r), jnp.cos(ang_c)] * 2, axis=1)
    sin = jnp.concatenate([jnp.sin(ang_r), jnp.sin(ang_c)] * 2, axis=1)
    cos = jnp.concatenate([cos, jnp.ones((n_ctx, HEAD_DIM), F32)], axis=0)
    sin = jnp.concatenate([sin, jnp.zeros((n_ctx, HEAD_DIM), F32)], axis=0)
    return cos, sin


def kernel(x, c, ctx, c_ctx, ada_w, ada_b, norm1_g, norm2_g, router_w, exp_w_gate, exp_w_up, exp_w_down,
           pool_w, pool_scale, gm_w_in, gm_v_g, gm_w_s, gm_b_s, gm_w_out, da_w_q, da_w_k, da_w_v, da_w_o,
           da_q_g, da_k_g, da_lam_q1, da_lam_k1, da_lam_q2, da_lam_k2, da_sub_g):
    bsz, n, d = x.shape
    assert bsz == 1 and c.shape[0] == 1
    n_ctx = ctx.shape[1]
    depth = ada_w.shape[0]
    xs, cs = x[0], ctx[0]

    s8 = jnp.concatenate([c, c_ctx[None], jnp.zeros((6, d), F32)], axis=0)
    mod = adaln(s8, ada_w, ada_b)

    def mods(i, row):
        return [mod[i, row:row + 1, k * d:(k + 1) * d] for k in range(6)]

    def row(v):
        return v.reshape(1, -1)

    for i in range(depth):
        kind, slot = i % N_MIXERS, i // N_MIXERS
        keep_ctx = any(j % N_MIXERS == 2 for j in range(i + 1, depth))
        sh1, sc1, g1, sh2, sc2, g2 = mods(i, 0)
        csh1, csc1, cg1, csh2, csc2, cg2 = mods(i, 1)
        n1, n2 = row(norm1_g[i]), row(norm2_g[i])
        streams = [(xs, sh1, sc1, g1)] + ([(cs, csh1, csc1, cg1)] if keep_ctx else [])

        if kind == 0:
            w = pool_w[slot].astype(BF16)
            outs = [pool_mixer(s, row_rstd(s), n1 * (1.0 + sc), sh, w, row(pool_scale[slot]), g)
                    for s, sh, sc, g in streams]
        elif kind == 1:
            w_in = gm_w_in[slot].astype(BF16)
            w_out = gm_w_out[slot].astype(BF16)
            w_s = gm_w_s[slot].astype(BF16)
            width = w_out.shape[0]
            bs_full = jnp.repeat(gm_b_s[slot].T, width // gm_w_s.shape[1], axis=1)
            outs = []
            for s, sh, sc, g in streams:
                uv, ssq = mm_gelu(norm_mod(s, n1, sh, sc, BF16), w_in)
                z = chunk_gate(uv, ssq, row(gm_v_g[slot]), w_s, bs_full)
                outs.append(mm_resid(z, w_out, s, g))
        else:
            assert not keep_ctx
            lam_init = 0.8 - 0.6 * math.exp(-0.3 * i)
            h_all = norm_mod_pair(xs, cs, n1, jnp.concatenate([sh1, csh1]), jnp.concatenate([sc1, csc1]), BF16)
            cos, sin = _rope_tables(n, n_ctx)
            gperm = (_rope_perm(2 * HEAD_DIM) % HEAD_DIM).reshape(2, HEAD_DIM)
            q_scale = HEAD_DIM ** -0.5 * math.log2(math.e)
            q = mm_qk(h_all, _permute_heads(da_w_q[slot]).astype(BF16), da_q_g[slot][gperm], cos, sin, q_scale, m=n)
            k = mm_qk(h_all, _permute_heads(da_w_k[slot]).astype(BF16), da_k_g[slot][gperm], cos, sin, 1.0)
            vt = mm_transposed(h_all, da_w_v[slot].T.astype(BF16))
            lamv = jnp.stack([da_lam_q1[slot], da_lam_k1[slot], da_lam_q2[slot], da_lam_k2[slot]])
            score_bound = (HEAD_DIM * q_scale * BOUND_SLACK) * jnp.max(jnp.abs(da_q_g[slot])) * jnp.max(jnp.abs(da_k_g[slot]))
            o = diff_attention(q, k, vt, lamv, row(da_sub_g[slot]), lam_init, score_bound)
            outs = [mm_resid(o, da_w_o[slot].astype(BF16), xs, g1)]

        xs = moe_layer(outs[0], n2, sh2, sc2, g2, router_w[i], exp_w_gate, exp_w_up, exp_w_down, i)
        if keep_ctx:
            cs = moe_layer(outs[1], n2, csh2, csc2, cg2, router_w[i], exp_w_gate, exp_w_up, exp_w_down, i)
    return xs[None]
```
